```python
import math
import jax, jax.numpy as jnp
from jax import lax
import numpy as np

D_MODEL = 1024
BATCH = 8
SEQ = 4096
DEPTH = 2

GRID_W = 64
CTX_LEN = 256
EPS = 1e-6
A_HEADS = 6
A_QK_DIM = 32
A_V_DIM = 2 * A_QK_DIM
ROPE_THETA = 10000.0
Q_BLOCK = 128
S5_GROUPS = 16
S5_GROUP_CH = 16
S5_STATE = 64
N_HEADS = 6
N_HEAD_DIM = 64
WIN_ROWS = 8
WIN_COLS = 16
A_QK_WIDTH = A_HEADS * 2 * A_QK_DIM
A_WIDTH = A_HEADS * A_V_DIM
B_WIDTH = S5_GROUPS * S5_GROUP_CH
C_WIDTH = N_HEADS * N_HEAD_DIM
MIX_WIDTH = A_WIDTH + B_WIDTH + C_WIDTH
Q_COLS = A_QK_WIDTH + C_WIDTH
KV_COLS = A_QK_WIDTH + A_WIDTH + 2 * C_WIDTH + B_WIDTH
IN_COLS = Q_COLS + KV_COLS
N_EXPERTS = 64
TOP_K = 8
N_GROUPS = 8
TOPK_GROUPS = 4
EXPERT_DIM = 256
SHARED_DIM = 256
ROUTED_SCALE = 2.5

kernel_name = "hymba_style_diffusion_hybrid_moe"

F32 = jnp.float32


def rms_norm(x, g):
    xf = x.astype(F32)
    y = xf * lax.rsqrt(jnp.mean(xf * xf, axis=-1, keepdims=True) + EPS)
    return (y * g.astype(F32)).astype(x.dtype)


def rope_2d(x, rows, cols):
    half = A_QK_DIM // 2
    nf = half // 2
    inv = ROPE_THETA ** (-jnp.arange(nf, dtype=F32) / nf)

    def rot(xa, pos):
        ang = pos[:, None] * inv
        cos = jnp.cos(ang)[:, None, None, :]
        sin = jnp.sin(ang)[:, None, None, :]
        x1, x2 = xa[..., :nf], xa[..., nf:]
        return jnp.concatenate([x1 * cos - x2 * sin, x1 * sin + x2 * cos], axis=-1)

    xf = x.astype(F32)
    return jnp.concatenate([rot(xf[..., :half], rows), rot(xf[..., half:], cols)], axis=-1).astype(x.dtype)


def split_q(p):
    b, t = p.shape[:2]
    qa = p[..., :A_QK_WIDTH].reshape(b, t, A_HEADS, 2, A_QK_DIM)
    qn = p[..., A_QK_WIDTH:].reshape(b, t, N_HEADS, N_HEAD_DIM)
    return qa, qn


def split_kv(p):
    b, t = p.shape[:2]
    o1 = A_QK_WIDTH
    o2 = o1 + A_WIDTH
    o3 = o2 + C_WIDTH
    o4 = o3 + C_WIDTH
    ka = p[..., :o1].reshape(b, t, A_HEADS, 2, A_QK_DIM)
    va = p[..., o1:o2].reshape(b, t, A_HEADS, A_V_DIM)
    kn = p[..., o2:o3].reshape(b, t, N_HEADS, N_HEAD_DIM)
    vn = p[..., o3:o4].reshape(b, t, N_HEADS, N_HEAD_DIM)
    u = p[..., o4:]
    return ka, va, kn, vn, u


def diff_attend(q, k, v, lam):
    s = jnp.einsum('bqhsd,bkhsd->bhsqk', q, k).astype(F32) * (A_QK_DIM ** -0.5)
    p = jax.nn.softmax(s, axis=-1)
    pd = p[:, :, 0] - lam * p[:, :, 1]
    return jnp.einsum('bhqk,bkhd->bqhd', pd.astype(v.dtype), v)


def diff_attn_mixer(q, k, v, k_ctx, v_ctx, q_ctx, g_q, g_k, lam_vec, g_sub, lam_init):
    b, L = q.shape[:2]
    pos = jnp.arange(L)
    rows = (pos // GRID_W).astype(F32)
    cols = (pos % GRID_W).astype(F32)
    q = rope_2d(rms_norm(q, g_q), rows, cols)
    k = rope_2d(rms_norm(k, g_k), rows, cols)
    k_ctx = rms_norm(k_ctx, g_k)
    lv = lam_vec.astype(F32)
    lam = jnp.exp(jnp.sum(lv[0] * lv[1])) - jnp.exp(jnp.sum(lv[2] * lv[3])) + lam_init
    k_all = jnp.concatenate([k, k_ctx], axis=1)
    v_all = jnp.concatenate([v, v_ctx], axis=1)
    nb = L // Q_BLOCK
    qb = jnp.moveaxis(q.reshape(b, nb, Q_BLOCK, A_HEADS, 2, A_QK_DIM), 1, 0)
    o = lax.map(lambda qq: diff_attend(qq, k_all, v_all, lam), qb)
    o = jnp.moveaxis(o, 0, 1).reshape(b, L, A_HEADS, A_V_DIM)

    def finish(t):
        return (rms_norm(t, g_sub) * (1.0 - lam_init)).reshape(t.shape[0], t.shape[1], A_WIDTH)

    out = finish(o)
    out_ctx = None
    if q_ctx is not None:
        out_ctx = finish(diff_attend(rms_norm(q_ctx, g_q), k_ctx, v_ctx, lam))
    return out, out_ctx


def _lin_combine(left, right):
    a1, b1 = left
    a2, b2 = right
    return a1 * a2, a2 * b1 + b2


def s5_mixer(u, u_ctx, a_re, a_im, log_dt, b_re, b_im, c_re, c_im, d_skip, w_glu, b_glu, need_ctx_out):
    lam = lax.complex(a_re.astype(F32), a_im.astype(F32))
    dt = jnp.exp(log_dt.astype(F32))[..., None]
    a_bar = jnp.exp(lam * dt)
    b_bar = ((a_bar - 1.0) / lam)[..., None] * lax.complex(b_re.astype(F32), b_im.astype(F32))
    c_mat = lax.complex(c_re.astype(F32), c_im.astype(F32))

    def drive(v):
        bv, t = v.shape[:2]
        vg = v.astype(F32).reshape(bv, t, S5_GROUPS, S5_GROUP_CH).astype(jnp.complex64)
        return jnp.einsum('btgp,kgnp->kbtgn', vg, b_bar)

    def scan(a, bu, reverse):
        _, s = lax.associative_scan(_lin_combine, (jnp.broadcast_to(a, bu.shape), bu), reverse=reverse, axis=1)
        return s

    def readout(s_f, s_b, v):
        bv, t = v.shape[:2]
        y = jnp.real(jnp.einsum('btgn,gpn->btgp', s_f, c_mat[0]) + jnp.einsum('btgn,gpn->btgp', s_b, c_mat[1]))
        y = y.reshape(bv, t, B_WIDTH) + d_skip.astype(F32) * v.astype(F32)
        g = jax.nn.gelu(y)
        return (g * jax.nn.sigmoid(g @ w_glu.astype(F32) + b_glu.astype(F32))).astype(v.dtype)

    bu_c = drive(u_ctx)
    s_cf = scan(a_bar[0], bu_c[0], False)
    s_cb = scan(a_bar[1], bu_c[1], True)
    bu = drive(u)
    bu_f = bu[0].at[:, 0].add(a_bar[0] * s_cf[:, -1])
    bu_b = bu[1].at[:, -1].add(a_bar[1] * s_cb[:, 0])
    out = readout(scan(a_bar[0], bu_f, False), scan(a_bar[1], bu_b, True), u)
    out_ctx = readout(s_cf, s_cb, u_ctx) if need_ctx_out else None
    return out, out_ctx


def na_mixer(q, k, v, k_ctx, v_ctx, q_ctx, g_q, g_k, rpb):
    b, L = q.shape[:2]
    R = L // GRID_W
    wr = min(WIN_ROWS, R)
    scale = N_HEAD_DIM ** -0.5
    q = rms_norm(q, g_q)
    k = rms_norm(k, g_k)
    k_ctx = rms_norm(k_ctx, g_k)

    def grid(t):
        return t.reshape(b, R, GRID_W, N_HEADS, N_HEAD_DIM).transpose(0, 3, 1, 2, 4)

    qg, kg, vg = grid(q), grid(k), grid(v)
    r_ids = jnp.arange(R)
    row_idx = jnp.clip(r_ids - wr // 2, 0, R - wr)[:, None] + jnp.arange(wr)[None, :]
    k_rows = kg[:, :, row_idx]
    v_rows = vg[:, :, row_idx]
    c_ids = jnp.arange(GRID_W)
    col_start = jnp.clip(c_ids - WIN_COLS // 2, 0, GRID_W - WIN_COLS)
    col_mask = (c_ids[None, :] >= col_start[:, None]) & (c_ids[None, :] < col_start[:, None] + WIN_COLS)
    dr = row_idx - r_ids[:, None] + (WIN_ROWS - 1)
    dc = jnp.clip(c_ids[None, :] - c_ids[:, None], -(WIN_COLS - 1), WIN_COLS - 1) + (WIN_COLS - 1)
    bias = rpb.astype(F32)[:, dr[:, None, :, None], dc[None, :, None, :]]
    s_loc = jnp.einsum('bhrqd,bhrjkd->bhrqjk', qg, k_rows).astype(F32) * scale + bias[None]
    s_loc = jnp.where(col_mask[:, None, :], s_loc, -jnp.inf)
    s_ctx = jnp.einsum('bhrqd,bchd->bhrqc', qg, k_ctx).astype(F32) * scale
    n_loc = wr * GRID_W
    p = jax.nn.softmax(jnp.concatenate([s_loc.reshape(b, N_HEADS, R, GRID_W, n_loc), s_ctx], axis=-1), axis=-1)
    p = p.astype(v.dtype)
    p_loc = p[..., :n_loc].reshape(b, N_HEADS, R, GRID_W, wr, GRID_W)
    o = jnp.einsum('bhrqjk,bhrjkd->bhrqd', p_loc, v_rows) + jnp.einsum('bhrqc,bchd->bhrqd', p[..., n_loc:], v_ctx)
    out = o.transpose(0, 2, 3, 1, 4).reshape(b, L, C_WIDTH)
    out_ctx = None
    if q_ctx is not None:
        sc = jnp.einsum('bqhd,bkhd->bhqk', rms_norm(q_ctx, g_q), k_ctx).astype(F32) * scale
        pc = jax.nn.softmax(sc, axis=-1).astype(v_ctx.dtype)
        out_ctx = jnp.einsum('bhqk,bkhd->bqhd', pc, v_ctx).reshape(q_ctx.shape[0], q_ctx.shape[1], C_WIDTH)
    return out, out_ctx


def moe_ffn(h, w_router, e_bias, w_gate, w_up, w_down, ws_gate, ws_up, ws_down):
    t = h.shape[0]
    scores = jax.nn.sigmoid(h.astype(F32) @ w_router.astype(F32))
    biased = scores + e_bias.astype(F32)
    grp = biased.reshape(t, N_GROUPS, N_EXPERTS // N_GROUPS)
    grp_score = lax.top_k(grp, 2)[0].sum(-1)
    _, g_idx = lax.top_k(grp_score, TOPK_GROUPS)
    g_mask = jax.nn.one_hot(g_idx, N_GROUPS, dtype=F32).sum(1) > 0
    e_mask = jnp.repeat(g_mask, N_EXPERTS // N_GROUPS, axis=-1)
    _, e_idx = lax.top_k(jnp.where(e_mask, biased, -jnp.inf), TOP_K)
    w = jnp.take_along_axis(scores, e_idx, axis=-1)
    w = w / jnp.sum(w, axis=-1, keepdims=True) * ROUTED_SCALE
    gates = jnp.einsum('tke,tk->et', jax.nn.one_hot(e_idx, N_EXPERTS, dtype=F32), w).astype(h.dtype)

    def expert(acc, p):
        wg, wu, wd, g = p
        return acc + ((jax.nn.silu(h @ wg) * (h @ wu)) @ wd) * g[:, None], None

    routed, _ = lax.scan(expert, jnp.zeros_like(h), (w_gate, w_up, w_down, gates))
    shared = (jax.nn.silu(h @ ws_gate) * (h @ ws_up)) @ ws_down
    return routed + shared


def setup_inputs(seed: int = 0) -> dict:
    key = jax.random.key(seed)
    ks = iter(jax.random.split(key, 40))
    D = D_MODEL
    Ld = DEPTH
    G, N, P = S5_GROUPS, S5_STATE, S5_GROUP_CH

    def nrm(shape, s):
        return jax.random.normal(next(ks), shape, F32) * s

    n_idx = jnp.arange(N, dtype=F32)
    return {
        "x": nrm((BATCH, SEQ, D), 1.0),
        "c": nrm((BATCH, D), 1.0),
        "ctx": nrm((BATCH, CTX_LEN, D), 1.0),
        "c_ctx": nrm((D,), 1.0),
        "w_mod": nrm((Ld, D, 6 * D), 0.5 * D ** -0.5),
        "b_mod": nrm((Ld, 6 * D), 0.01),
        "g_mix": 1.0 + nrm((Ld, D), 0.01),
        "g_ffn": 1.0 + nrm((Ld, D), 0.01),
        "w_in": nrm((Ld, D, IN_COLS), D ** -0.5),
        "w_out": nrm((Ld, MIX_WIDTH, D), MIX_WIDTH ** -0.5),
        "a_gq": 1.0 + nrm((Ld, A_QK_DIM), 0.01),
        "a_gk": 1.0 + nrm((Ld, A_QK_DIM), 0.01),
        "a_lambda": nrm((Ld, 4, A_QK_DIM), 0.1),
        "a_gsub": 1.0 + nrm((Ld, A_V_DIM), 0.01),
        "n_gq": 1.0 + nrm((Ld, N_HEAD_DIM), 0.01),
        "n_gk": 1.0 + nrm((Ld, N_HEAD_DIM), 0.01),
        "n_rpb": nrm((Ld, N_HEADS, 2 * WIN_ROWS - 1, 2 * WIN_COLS - 1), 0.1),
        "s5_a_re": -0.5 + nrm((Ld, 2, G, N), 0.01),
        "s5_a_im": jnp.pi * n_idx + nrm((Ld, 2, G, N), 0.01),
        "s5_log_dt": jax.random.uniform(next(ks), (Ld, 2, G), F32, math.log(1e-3), math.log(1e-1)),
        "s5_b_re": nrm((Ld, 2, G, N, P), (2 * P) ** -0.5),
        "s5_b_im": nrm((Ld, 2, G, N, P), (2 * P) ** -0.5),
        "s5_c_re": nrm((Ld, 2, G, P, N), (2 * N) ** -0.5),
        "s5_c_im": nrm((Ld, 2, G, P, N), (2 * N) ** -0.5),
        "s5_d": nrm((Ld, B_WIDTH), 1.0),
        "s5_w_glu": nrm((Ld, B_WIDTH, B_WIDTH), B_WIDTH ** -0.5),
        "s5_b_glu": nrm((Ld, B_WIDTH), 0.01),
        "w_router": nrm((Ld, D, N_EXPERTS), D ** -0.5),
        "e_bias": nrm((Ld, N_EXPERTS), 0.01),
        "w_gate": nrm((Ld, N_EXPERTS, D, EXPERT_DIM), D ** -0.5),
        "w_up": nrm((Ld, N_EXPERTS, D, EXPERT_DIM), D ** -0.5),
        "w_down": nrm((Ld, N_EXPERTS, EXPERT_DIM, D), EXPERT_DIM ** -0.5),
        "ws_gate": nrm((Ld, D, SHARED_DIM), D ** -0.5),
        "ws_up": nrm((Ld, D, SHARED_DIM), D ** -0.5),
        "ws_down": nrm((Ld, SHARED_DIM, D), SHARED_DIM ** -0.5),
    }


def reference(x, c, ctx, c_ctx, w_mod, b_mod, g_mix, g_ffn, w_in, w_out, a_gq, a_gk, a_lambda, a_gsub,
              n_gq, n_gk, n_rpb, s5_a_re, s5_a_im, s5_log_dt, s5_b_re, s5_b_im, s5_c_re, s5_c_im, s5_d,
              s5_w_glu, s5_b_glu, w_router, e_bias, w_gate, w_up, w_down, ws_gate, ws_up, ws_down):
    b, L, D = x.shape
    lc = ctx.shape[1]
    ctx_s = ctx
    cond = jax.nn.silu(c.astype(F32))
    cond_ctx = jax.nn.silu(c_ctx.astype(F32))
    for l in range(DEPTH):
        last = l == DEPTH - 1
        lam_init = 0.8 - 0.6 * math.exp(-0.3 * l)
        mod = (cond @ w_mod[l].astype(F32) + b_mod[l].astype(F32)).astype(x.dtype)
        sh1, sc1, gt1, sh2, sc2, gt2 = jnp.split(mod[:, None, :], 6, axis=-1)
        n_cm = 2 if last else 6
        mod_c = (cond_ctx @ w_mod[l][:, :n_cm * D].astype(F32) + b_mod[l][:n_cm * D].astype(F32)).astype(x.dtype)
        cm = jnp.split(mod_c, n_cm)

        h = rms_norm(x, g_mix[l]) * (1.0 + sc1) + sh1
        hc = rms_norm(ctx_s, g_mix[l]) * (1.0 + cm[1]) + cm[0]
        qa, qn = split_q(h @ w_in[l][:, :Q_COLS])
        ka, va, kn, vn, u = split_kv(h @ w_in[l][:, Q_COLS:])
        ka_c, va_c, kn_c, vn_c, u_c = split_kv(hc @ w_in[l][:, Q_COLS:])
        if last:
            qa_c, qn_c = None, None
        else:
            qa_c, qn_c = split_q(hc @ w_in[l][:, :Q_COLS])
        oa, oa_c = diff_attn_mixer(qa, ka, va, ka_c, va_c, qa_c, a_gq[l], a_gk[l], a_lambda[l], a_gsub[l], lam_init)
        ob, ob_c = s5_mixer(u, u_c, s5_a_re[l], s5_a_im[l], s5_log_dt[l], s5_b_re[l], s5_b_im[l],
                            s5_c_re[l], s5_c_im[l], s5_d[l], s5_w_glu[l], s5_b_glu[l], not last)
        on, on_c = na_mixer(qn, kn, vn, kn_c, vn_c, qn_c, n_gq[l], n_gk[l], n_rpb[l])
        x = x + gt1 * (jnp.concatenate([oa, ob, on], axis=-1) @ w_out[l])
        if not last:
            ctx_s = ctx_s + cm[2] * (jnp.concatenate([oa_c, ob_c, on_c], axis=-1) @ w_out[l])

        h2 = (rms_norm(x, g_ffn[l]) * (1.0 + sc2) + sh2).reshape(b * L, D)
        if last:
            tokens = h2
        else:
            h2c = (rms_norm(ctx_s, g_ffn[l]) * (1.0 + cm[4]) + cm[3]).reshape(b * lc, D)
            tokens = jnp.concatenate([h2, h2c], axis=0)
        y = moe_ffn(tokens, w_router[l], e_bias[l], w_gate[l], w_up[l], w_down[l], ws_gate[l], ws_up[l], ws_down[l])
        x = x + gt2 * y[:b * L].reshape(b, L, D)
        if not last:
            ctx_s = ctx_s + cm[5] * y[b * L:].reshape(b, lc, D)
    return x
```

```python
import functools
import math

import jax
import jax.numpy as jnp
from jax import lax
from jax.experimental import pallas as pl
from jax.experimental.pallas import tpu as pltpu

F32 = jnp.float32
BF16 = jnp.bfloat16

D_MODEL = 1024
GRID_W = 64
EPS = 1e-6
A_HEADS = 6
A_QK_DIM = 32
A_V_DIM = 64
ROPE_THETA = 10000.0
S5_GROUPS = 16
S5_GROUP_CH = 16
S5_STATE = 64
N_HEADS = 6
N_HEAD_DIM = 64
WIN_ROWS = 8
WIN_COLS = 16
HW = 384
B_WIDTH = 256
Q_COLS = 768
IN_COLS = 2560
N_EXPERTS = 64
TOP_K = 8
N_GROUPS = 8
TOPK_GROUPS = 4
EXPERT_DIM = 256
ROUTED_SCALE = 2.5

ROW_TILE = 256
NA_ROWS = 4
NA_KROWS = 12
S5_CHUNK = 128
MOE_TILE = 1024
MOE_EB = 5
NEG = -1e30
VMEM_LIMIT = 56 * 1024 * 1024


def _sigmoid(x):
    return 1.0 / (1.0 + jnp.exp(-x))


def _gelu_tanh(x):
    return 0.5 * x * (1.0 + jnp.tanh(math.sqrt(2.0 / math.pi) * (x + 0.044715 * (x * x * x))))


def _split_bf16(a):
    hi = a.astype(BF16)
    lo = (a - hi.astype(F32)).astype(BF16)
    return hi, lo


def _dot(a, b):
    return jnp.dot(a, b, preferred_element_type=F32)


def _dot3(a, b):
    ah, al = _split_bf16(a)
    bh, bl = _split_bf16(b)
    return _dot(ah, bh) + _dot(ah, bl) + _dot(al, bh)


def _cparams(sem):
    return pltpu.CompilerParams(dimension_semantics=sem, vmem_limit_bytes=VMEM_LIMIT)


def _mod_kernel(c_ref, w_ref, b_ref, o_ref):
    c = c_ref[...]
    cond = c * _sigmoid(c)
    o_ref[...] = _dot3(cond, w_ref[...]) + b_ref[...]


def _modulation(cond_rows, w_mod, b_mod):
    r, d = cond_rows.shape
    n = w_mod.shape[1]
    tn = 1536
    return pl.pallas_call(
        _mod_kernel,
        out_shape=jax.ShapeDtypeStruct((r, n), F32),
        grid=(n // tn,),
        in_specs=[pl.BlockSpec((r, d), lambda j: (0, 0)),
                  pl.BlockSpec((d, tn), lambda j: (0, j)),
                  pl.BlockSpec((1, tn), lambda j: (0, j))],
        out_specs=pl.BlockSpec((r, tn), lambda j: (0, j)),
        compiler_params=_cparams(("arbitrary",)),
        name="adaln_mod",
    )(cond_rows, w_mod, b_mod.reshape(1, n))


def _group_rms(t, ones_ref, gain_ref, group):
    ms = _dot((t * t).astype(BF16), ones_ref[...]) * (1.0 / group)
    return t * lax.rsqrt(ms + EPS) * gain_ref[...]


def _rope(t, cos_ref, sa_ref, sb_ref):
    up = pltpu.roll(t, HW - 8, 1)
    dn = pltpu.roll(t, 8, 1)
    return t * cos_ref[...] + up * sa_ref[...] + dn * sb_ref[...]


def _proj_kernel(x_ref, mod_ref, g_ref, w_ref, cos_ref, sa_ref, sb_ref, ones32_ref, ones64_ref,
                 gqa_ref, gka_ref, gqn_ref, gkn_ref,
                 qa_ref, kat_ref, va_ref, qn_ref, knt_ref, vn_ref, u_ref):
    x = x_ref[0]
    mod = mod_ref[0, 0]
    ms = jnp.mean(x * x, axis=-1, keepdims=True)
    h = x * lax.rsqrt(ms + EPS) * g_ref[...] * (1.0 + mod[1:2]) + mod[0:1]
    hb = h.astype(BF16)

    def sec(a, b):
        return _dot(hb, w_ref[:, a:b])

    qa = _rope(_group_rms(sec(0, 384), ones32_ref, gqa_ref, A_QK_DIM), cos_ref, sa_ref, sb_ref)
    qa_ref[0] = qa.astype(BF16)
    qn_ref[0] = _group_rms(sec(384, 768), ones64_ref, gqn_ref, N_HEAD_DIM).astype(BF16)
    ka = _rope(_group_rms(sec(768, 1152), ones32_ref, gka_ref, A_QK_DIM), cos_ref, sa_ref, sb_ref)
    kat_ref[0] = ka.T.astype(BF16)
    va_ref[0] = sec(1152, 1536).astype(BF16)
    kn = _group_rms(sec(1536, 1920), ones64_ref, gkn_ref, N_HEAD_DIM)
    knt_ref[0] = kn.T.astype(BF16)
    vn_ref[0] = sec(1920, 2304).astype(BF16)
    u_ref[0] = sec(2304, 2560)


def _in_projection(xs, modall, g_mix, w_in_bf, tables, gains, lc):
    b, s, d = xs.shape
    tm = ROW_TILE
    cos, sa, sb, ones32, ones64 = tables
    row = lambda i, bb: (bb, i, 0)
    tab = lambda i, bb: (i, 0)
    const2 = lambda i, bb: (0, 0)
    act = lambda w, dt: jax.ShapeDtypeStruct((b, s, w), dt)
    act_t = jax.ShapeDtypeStruct((b, HW, s), BF16)
    return pl.pallas_call(
        _proj_kernel,
        out_shape=(act(HW, BF16), act_t, act(HW, BF16), act(HW, BF16), act_t, act(HW, BF16), act(B_WIDTH, F32)),
        grid=(s // tm, b),
        in_specs=[pl.BlockSpec((1, tm, d), row),
                  pl.BlockSpec((1, 1, 6, d), lambda i, bb: (bb, jnp.minimum(i, 1), 0, 0)),
                  pl.BlockSpec((1, d), const2),
                  pl.BlockSpec((d, IN_COLS), const2),
                  pl.BlockSpec((tm, HW), tab), pl.BlockSpec((tm, HW), tab), pl.BlockSpec((tm, HW), tab),
                  pl.BlockSpec((HW, HW), const2), pl.BlockSpec((HW, HW), const2),
                  pl.BlockSpec((1, HW), const2), pl.BlockSpec((1, HW), const2),
                  pl.BlockSpec((1, HW), const2), pl.BlockSpec((1, HW), const2)],
        out_specs=(pl.BlockSpec((1, tm, HW), row),
                   pl.BlockSpec((1, HW, tm), lambda i, bb: (bb, 0, i)),
                   pl.BlockSpec((1, tm, HW), row),
                   pl.BlockSpec((1, tm, HW), row),
                   pl.BlockSpec((1, HW, tm), lambda i, bb: (bb, 0, i)),
                   pl.BlockSpec((1, tm, HW), row),
                   pl.BlockSpec((1, tm, B_WIDTH), row)),
        compiler_params=_cparams(("arbitrary", "arbitrary")),
        name="in_proj",
    )(xs, modall, g_mix.reshape(1, d), w_in_bf, cos, sa, sb, ones32, ones64, *gains)


def _rope_tables(s, lc):
    p = jnp.arange(s)
    pos = jnp.maximum(p - lc, 0)
    rows = (pos // GRID_W).astype(F32)
    cols = (pos % GRID_W).astype(F32)
    lane = jnp.arange(HW)
    j32 = lane % A_QK_DIM
    half = j32 // 16
    i16 = j32 % 16
    nf = 8
    inv = ROPE_THETA ** (-(i16 % nf).astype(F32) / nf)
    coord = jnp.where(half[None, :] == 0, rows[:, None], cols[:, None])
    ang = coord * inv[None, :]
    is_lat = (p >= lc)[:, None]
    second = (i16 >= nf)[None, :]
    cos = jnp.where(is_lat, jnp.cos(ang), 1.0)
    sin = jnp.where(is_lat, jnp.sin(ang), 0.0)
    sa = jnp.where(second, 0.0, -sin)
    sb = jnp.where(second, sin, 0.0)
    return cos.astype(F32), sa.astype(F32), sb.astype(F32)


def _block_ones(group):
    g = jnp.arange(HW) // group
    return (g[:, None] == g[None, :]).astype(BF16)


def _diff_attend(q_all, kt_ref, v_ref, lam, gsub_ref, sk):
    outs = []
    for h in range(A_HEADS):
        es, rs = [], []
        for sub in range(2):
            off = h * 2 * A_QK_DIM + sub * A_QK_DIM
            sc = _dot(q_all[:, off:off + A_QK_DIM], kt_ref[0, off:off + A_QK_DIM, 0:sk])
            e = jnp.exp(sc - jnp.max(sc, axis=-1, keepdims=True))
            es.append(e)
            rs.append(1.0 / jnp.sum(e, axis=-1, keepdims=True))
        pd = es[0] * rs[0] - es[1] * (lam * rs[1])
        o = _dot(pd.astype(BF16), v_ref[0, 0:sk, h * A_V_DIM:(h + 1) * A_V_DIM])
        outs.append(o * lax.rsqrt(jnp.mean(o * o, axis=-1, keepdims=True) + EPS))
    return (jnp.concatenate(outs, axis=-1) * gsub_ref[...]).astype(BF16)


def _diff_attn_kernel(q_ref, kt_ref, v_ref, lam_ref, gsub_ref, o_ref, *, lc, ctx_first):
    lam = lam_ref[...]
    s = kt_ref.shape[2]
    if ctx_first:
        i = pl.program_id(1)

        @pl.when(i == 0)
        def _():
            o_ref[0] = _diff_attend(q_ref[0], kt_ref, v_ref, lam, gsub_ref, lc)

        @pl.when(i > 0)
        def _():
            o_ref[0] = _diff_attend(q_ref[0], kt_ref, v_ref, lam, gsub_ref, s)
    else:
        o_ref[0] = _diff_attend(q_ref[0], kt_ref, v_ref, lam, gsub_ref, s)


def _diff_attention(qa, kat, va, lam, gsub_t, lc, with_ctx):
    b, s, _ = qa.shape
    tq = ROW_TILE
    off = 0 if with_ctx else lc // tq
    rows_out = s - off * tq
    return pl.pallas_call(
        functools.partial(_diff_attn_kernel, lc=lc, ctx_first=with_ctx),
        out_shape=jax.ShapeDtypeStruct((b, rows_out, HW), BF16),
        grid=(b, rows_out // tq),
        in_specs=[pl.BlockSpec((1, tq, HW), lambda bb, i: (bb, i + off, 0)),
                  pl.BlockSpec((1, HW, s), lambda bb, i: (bb, 0, 0)),
                  pl.BlockSpec((1, s, HW), lambda bb, i: (bb, 0, 0)),
                  pl.BlockSpec((1, 1), lambda bb, i: (0, 0)),
                  pl.BlockSpec((1, HW), lambda bb, i: (0, 0))],
        out_specs=pl.BlockSpec((1, tq, HW), lambda bb, i: (bb, i, 0)),
        compiler_params=_cparams(("arbitrary", "arbitrary")),
        name="diff_attn",
    )(qa, kat, va, lam, gsub_t)


def _na_ctx_attend(q_all, kt_ref, v_ref, lc):
    outs = []
    for h in range(N_HEADS):
        hs = slice(h * N_HEAD_DIM, (h + 1) * N_HEAD_DIM)
        sc = _dot(q_all[:, hs], kt_ref[0, hs, 0:lc])
        e = jnp.exp(sc - jnp.max(sc, axis=-1, keepdims=True))
        o = _dot(e.astype(BF16), v_ref[0, 0:lc, hs])
        outs.append(o * (1.0 / jnp.sum(e, axis=-1, keepdims=True)))
    return jnp.concatenate(outs, axis=-1).astype(BF16)


def _na_attend(q_all, kt_ref, v_ref, bias_ref, koff, lc):
    nk = NA_KROWS * GRID_W
    outs = []
    for h in range(N_HEADS):
        hs = slice(h * N_HEAD_DIM, (h + 1) * N_HEAD_DIM)
        q = q_all[:, hs]
        s_loc = _dot(q, kt_ref[0, hs, pl.ds(koff, nk)]) + bias_ref[0, h]
        s_ctx = _dot(q, kt_ref[0, hs, 0:lc])
        m = jnp.maximum(jnp.max(s_loc, axis=-1, keepdims=True), jnp.max(s_ctx, axis=-1, keepdims=True))
        e_loc = jnp.exp(s_loc - m)
        e_ctx = jnp.exp(s_ctx - m)
        den = jnp.sum(e_loc, axis=-1, keepdims=True) + jnp.sum(e_ctx, axis=-1, keepdims=True)
        o = _dot(e_loc.astype(BF16), v_ref[0, pl.ds(koff, nk), hs]) + _dot(e_ctx.astype(BF16), v_ref[0, 0:lc, hs])
        outs.append(o * (1.0 / den))
    return jnp.concatenate(outs, axis=-1).astype(BF16)


def _na_kernel(q_ref, kt_ref, v_ref, bias_ref, o_ref, *, lc, n_img_rows, ctx_first):
    i = pl.program_id(1)
    blk = i - 1 if ctx_first else i
    start_row = jnp.clip(NA_ROWS * blk - WIN_ROWS // 2, 0, n_img_rows - NA_KROWS)
    koff = pl.multiple_of(lc + start_row * GRID_W, 128)
    if ctx_first:
        @pl.when(i == 0)
        def _():
            o_ref[0] = _na_ctx_attend(q_ref[0], kt_ref, v_ref, lc)

        @pl.when(i > 0)
        def _():
            o_ref[0] = _na_attend(q_ref[0], kt_ref, v_ref, bias_ref, koff, lc)
    else:
        o_ref[0] = _na_attend(q_ref[0], kt_ref, v_ref, bias_ref, koff, lc)


def _na_attention(qn, knt, vn, bias, lc, with_ctx):
    b, s, _ = qn.shape
    tq = NA_ROWS * GRID_W
    assert tq == ROW_TILE and lc % tq == 0
    n_img_rows = (s - lc) // GRID_W
    nblk = n_img_rows // NA_ROWS
    off = 0 if with_ctx else lc // tq
    rows_out = s - off * tq
    first = 1 if with_ctx else 0

    def variant(bb, i):
        blk = i - first
        return (jnp.where(blk <= 0, 0, jnp.where(blk == nblk - 1, 2, 1)), 0, 0, 0)

    return pl.pallas_call(
        functools.partial(_na_kernel, lc=lc, n_img_rows=n_img_rows, ctx_first=with_ctx),
        out_shape=jax.ShapeDtypeStruct((b, rows_out, HW), BF16),
        grid=(b, rows_out // tq),
        in_specs=[pl.BlockSpec((1, tq, HW), lambda bb, i: (bb, i + off, 0)),
                  pl.BlockSpec((1, HW, s), lambda bb, i: (bb, 0, 0)),
                  pl.BlockSpec((1, s, HW), lambda bb, i: (bb, 0, 0)),
                  pl.BlockSpec((1, N_HEADS, tq, NA_KROWS * GRID_W), variant)],
        out_specs=pl.BlockSpec((1, tq, HW), lambda bb, i: (bb, i, 0)),
        compiler_params=_cparams(("arbitrary", "arbitrary")),
        name="nbr_attn",
    )(qn, knt, vn, bias)


def _na_bias(rpb, n_img_rows):
    a = jnp.arange(NA_ROWS)[:, None, None, None]
    cq = jnp.arange(GRID_W)[None, :, None, None]
    j = jnp.arange(NA_KROWS)[None, None, :, None]
    ck = jnp.arange(GRID_W)[None, None, None, :]
    cstart = jnp.clip(cq - WIN_COLS // 2, 0, GRID_W - WIN_COLS)
    colmask = (ck >= cstart) & (ck < cstart + WIN_COLS)
    dc = jnp.clip(ck - cq, -(WIN_COLS - 1), WIN_COLS - 1) + (WIN_COLS - 1)
    out = []
    for r0_minus_k, wstart in ((0, 0 * a), (WIN_ROWS // 2, a), (NA_KROWS - NA_ROWS, NA_KROWS - WIN_ROWS + 0 * a)):
        inwin = (j >= wstart) & (j < wstart + WIN_ROWS)
        dr = jnp.clip(j - r0_minus_k - a + (WIN_ROWS - 1), 0, 2 * WIN_ROWS - 2)
        dr_b, dc_b = jnp.broadcast_arrays(dr, dc)
        vals = rpb.astype(F32)[:, dr_b, dc_b]
        vals = jnp.where((inwin & colmask)[None], vals, NEG)
        out.append(vals.reshape(N_HEADS, NA_ROWS * GRID_W, NA_KROWS * GRID_W))
    return jnp.stack(out)


def _s5_kernel(u_ref, bm_ref, cm_ref, are_ref, aim_ref, y_ref, x_scr, st_scr, *, reverse):
    tc, nb, w = u_ref.shape
    ns = are_ref.shape[1]

    @pl.when(pl.program_id(0) == 0)
    def _():
        st_scr[...] = jnp.zeros_like(st_scr)

    u = u_ref[...].reshape(tc * nb, w).astype(BF16)
    x_scr[...] = _dot(u, bm_ref[...])
    a_re = are_ref[...]
    a_im = aim_ref[...]

    def step(t, carry):
        s_re, s_im = carry
        tt = (tc - 1 - t) if reverse else t
        rows = pl.ds(pl.multiple_of(tt * nb, nb), nb)
        n_re = a_re * s_re - a_im * s_im + x_scr[rows, 0:ns]
        n_im = a_re * s_im + a_im * s_re + x_scr[rows, ns:2 * ns]
        x_scr[rows, 0:ns] = n_re
        x_scr[rows, ns:2 * ns] = n_im
        return n_re, n_im

    s_re, s_im = lax.fori_loop(0, tc, step, (st_scr[:, 0:ns], st_scr[:, ns:2 * ns]), unroll=4)
    st_scr[:, 0:ns] = s_re
    st_scr[:, ns:2 * ns] = s_im
    y = _dot(x_scr[...].astype(BF16), cm_ref[...])
    y_ref[...] = y.reshape(tc, nb, w)


def _s5_scan(u_tm, bmat, cmat, a_re, a_im, lc, reverse):
    s, nb, w = u_tm.shape
    tc = S5_CHUNK
    nc, ncc = s // tc, lc // tc
    ns = a_re.shape[1]
    if reverse:
        chunk = lambda j: (jnp.where(j < ncc, ncc - 1 - j, nc - 1 - (j - ncc)), 0, 0)
    else:
        chunk = lambda j: (j, 0, 0)
    const = lambda j: (0, 0)
    return pl.pallas_call(
        functools.partial(_s5_kernel, reverse=reverse),
        out_shape=jax.ShapeDtypeStruct((s, nb, w), F32),
        grid=(nc,),
        in_specs=[pl.BlockSpec((tc, nb, w), chunk),
                  pl.BlockSpec((w, 2 * ns), const),
                  pl.BlockSpec((2 * ns, w), const),
                  pl.BlockSpec((nb, ns), const),
                  pl.BlockSpec((nb, ns), const)],
        out_specs=pl.BlockSpec((tc, nb, w), chunk),
        scratch_shapes=[pltpu.VMEM((tc * nb, 2 * ns), F32), pltpu.VMEM((nb, 2 * ns), F32)],
        compiler_params=_cparams(("arbitrary",)),
        name="s5_scan_bwd" if reverse else "s5_scan_fwd",
    )(u_tm, bmat, cmat, a_re, a_im)


def _s5_params(a_re, a_im, log_dt, b_re, b_im, c_re, c_im, nb):
    g, n, p = S5_GROUPS, S5_STATE, S5_GROUP_CH
    lam = lax.complex(a_re.astype(F32), a_im.astype(F32))
    dt = jnp.exp(log_dt.astype(F32))[..., None]
    a_bar = jnp.exp(lam * dt)
    b_bar = ((a_bar - 1.0) / lam)[..., None] * lax.complex(b_re.astype(F32), b_im.astype(F32))
    eye = jnp.eye(g, dtype=F32)
    out = []
    for k in range(2):
        bt = jnp.transpose(b_bar[k], (0, 2, 1))
        b_r = jnp.einsum('gpn,gh->gphn', jnp.real(bt), eye).reshape(g * p, g * n)
        b_i = jnp.einsum('gpn,gh->gphn', jnp.imag(bt), eye).reshape(g * p, g * n)
        bmat = jnp.concatenate([b_r, b_i], axis=1).astype(BF16)
        ct = jnp.transpose(c_re[k].astype(F32), (0, 2, 1))
        ci = jnp.transpose(c_im[k].astype(F32), (0, 2, 1))
        c_r = jnp.einsum('gnp,gh->gnhp', ct, eye).reshape(g * n, g * p)
        c_i = jnp.einsum('gnp,gh->gnhp', ci, eye).reshape(g * n, g * p)
        cmat = jnp.concatenate([c_r, -c_i], axis=0).astype(BF16)
        ar = jnp.broadcast_to(jnp.real(a_bar[k]).reshape(1, g * n), (nb, g * n))
        ai = jnp.broadcast_to(jnp.imag(a_bar[k]).reshape(1, g * n), (nb, g * n))
        out.append((bmat, cmat, ar, ai))
    return out


def _out_kernel(oa_ref, on_ref, y_ref, u_ref, x_ref, mod_ref, gffn_ref, wo_ref, dskip_ref, wglu_ref, bglu_ref,
                wr_ref, xo_ref, h2_ref, lg_ref):
    mod = mod_ref[0, 0]
    g = _gelu_tanh(y_ref[0] + dskip_ref[...] * u_ref[0])
    ob = g * _sigmoid(_dot(g.astype(BF16), wglu_ref[...]) + bglu_ref[...])
    mix = (_dot(oa_ref[0], wo_ref[0:HW, :]) + _dot(ob.astype(BF16), wo_ref[HW:HW + B_WIDTH, :])
           + _dot(on_ref[0], wo_ref[HW + B_WIDTH:, :]))
    x = x_ref[0] + mod[2:3] * mix
    xo_ref[0] = x
    ms = jnp.mean(x * x, axis=-1, keepdims=True)
    h2 = x * lax.rsqrt(ms + EPS) * gffn_ref[...] * (1.0 + mod[4:5]) + mod[3:4]
    h2_ref[0] = h2.astype(BF16)
    lg_ref[0] = _dot3(h2, wr_ref[...])


def _out_projection(oa, on, y, u, xs, modall, g_ffn, w_out_bf, d_skip, w_glu_bf, b_glu, w_router_pad, lc, with_ctx):
    b, s, d = xs.shape
    tm = ROW_TILE
    off = 0 if with_ctx else lc // tm
    rows_out = s - off * tm
    full = lambda bb, i: (bb, i + off, 0)
    outr = lambda bb, i: (bb, i, 0)
    const = lambda bb, i: (0, 0)
    ne = w_router_pad.shape[1]
    return pl.pallas_call(
        _out_kernel,
        out_shape=(jax.ShapeDtypeStruct((b, rows_out, d), F32),
                   jax.ShapeDtypeStruct((b, rows_out, d), BF16),
                   jax.ShapeDtypeStruct((b, rows_out, ne), F32)),
        grid=(b, rows_out // tm),
        in_specs=[pl.BlockSpec((1, tm, HW), outr),
                  pl.BlockSpec((1, tm, HW), outr),
                  pl.BlockSpec((1, tm, B_WIDTH), full),
                  pl.BlockSpec((1, tm, B_WIDTH), full),
                  pl.BlockSpec((1, tm, d), full),
                  pl.BlockSpec((1, 1, 6, d), lambda bb, i: (bb, jnp.minimum(i + off, 1), 0, 0)),
                  pl.BlockSpec((1, d), const),
                  pl.BlockSpec((d, d), const),
                  pl.BlockSpec((1, B_WIDTH), const),
                  pl.BlockSpec((B_WIDTH, B_WIDTH), const),
                  pl.BlockSpec((1, B_WIDTH), const),
                  pl.BlockSpec((d, ne), const)],
        out_specs=(pl.BlockSpec((1, tm, d), outr), pl.BlockSpec((1, tm, d), outr), pl.BlockSpec((1, tm, ne), outr)),
        compiler_params=_cparams(("arbitrary", "arbitrary")),
        name="out_proj",
    )(oa, on, y, u, xs, modall, g_ffn.reshape(1, d), w_out_bf, d_skip.reshape(1, -1), w_glu_bf,
      b_glu.reshape(1, -1), w_router_pad)


def _moe_kernel(h_ref, gates_ref, x_ref, gt_ref, wg_ref, wu_ref, wd_ref, o_ref, acc_ref):
    e = pl.program_id(1)

    @pl.when(e == 0)
    def _():
        acc_ref[...] = jnp.zeros_like(acc_ref)

    h = h_ref[...]
    gates = gates_ref[0]
    hid = []
    for j in range(MOE_EB):
        a = _dot(h, wg_ref[j])
        up = _dot(h, wu_ref[j])
        hid.append((a * _sigmoid(a) * up * gates[:, j:j + 1]).astype(BF16))
    hidden = jnp.concatenate(hid, axis=-1)
    acc_ref[...] += _dot(hidden, wd_ref[...])

    @pl.when(e == pl.num_programs(1) - 1)
    def _():
        nsub = h_ref.shape[0] // ROW_TILE
        for j in range(nsub):
            rows = slice(j * ROW_TILE, (j + 1) * ROW_TILE)
            o_ref[rows, :] = x_ref[rows, :] + gt_ref[0, j:j + 1, :] * acc_ref[rows, :]


def _moe_tile(t):
    return max(m for m in range(ROW_TILE, MOE_TILE + 1, ROW_TILE) if t % m == 0)


def _moe(h2, gates_blk, x, gt_rows, wg, wu, wd):
    t, d = h2.shape
    tm = _moe_tile(t)
    ne = wg.shape[0]
    nsteps = ne // MOE_EB
    nsub = tm // ROW_TILE
    return pl.pallas_call(
        _moe_kernel,
        out_shape=jax.ShapeDtypeStruct((t, d), F32),
        grid=(t // tm, nsteps),
        in_specs=[pl.BlockSpec((tm, d), lambda i, e: (i, 0)),
                  pl.BlockSpec((1, tm, 128), lambda i, e: (e, i, 0)),
                  pl.BlockSpec((tm, d), lambda i, e: (i, 0)),
                  pl.BlockSpec((1, nsub, d), lambda i, e: (i, 0, 0)),
                  pl.BlockSpec((MOE_EB, d, EXPERT_DIM), lambda i, e: (e, 0, 0)),
                  pl.BlockSpec((MOE_EB, d, EXPERT_DIM), lambda i, e: (e, 0, 0)),
                  pl.BlockSpec((MOE_EB * EXPERT_DIM, d), lambda i, e: (e, 0))],
        out_specs=pl.BlockSpec((tm, d), lambda i, e: (i, 0)),
        scratch_shapes=[pltpu.VMEM((tm, d), F32)],
        compiler_params=_cparams(("arbitrary", "arbitrary")),
        name="moe_ffn",
    )(h2, gates_blk, x, gt_rows, wg, wu, wd)


def _route(logits, e_bias):
    t = logits.shape[0]
    scores = jax.nn.sigmoid(logits)
    biased = scores + e_bias.astype(F32)
    grp = biased.reshape(t, N_GROUPS, N_EXPERTS // N_GROUPS)
    grp_score = lax.top_k(grp, 2)[0].sum(-1)
    _, g_idx = lax.top_k(grp_score, TOPK_GROUPS)
    g_mask = jax.nn.one_hot(g_idx, N_GROUPS, dtype=F32).sum(1) > 0
    e_mask = jnp.repeat(g_mask, N_EXPERTS // N_GROUPS, axis=-1)
    _, e_idx = lax.top_k(jnp.where(e_mask, biased, -jnp.inf), TOP_K)
    w = jnp.take_along_axis(scores, e_idx, axis=-1)
    w = w / jnp.sum(w, axis=-1, keepdims=True) * ROUTED_SCALE
    return jnp.einsum('tke,tk->te', jax.nn.one_hot(e_idx, N_EXPERTS, dtype=F32), w)


def kernel(x, c, ctx, c_ctx, w_mod, b_mod, g_mix, g_ffn, w_in, w_out, a_gq, a_gk, a_lambda, a_gsub, n_gq, n_gk, n_rpb, s5_a_re, s5_a_im, s5_log_dt, s5_b_re, s5_b_im, s5_c_re, s5_c_im, s5_d, s5_w_glu, s5_b_glu, w_router, e_bias, w_gate, w_up, w_down, ws_gate, ws_up, ws_down):
    b, l, d = x.shape
    lc = ctx.shape[1]
    s = lc + l
    depth = w_mod.shape[0]
    n_img_rows = l // GRID_W

    xs = jnp.concatenate([ctx, x], axis=1).astype(F32)
    cond_rows = jnp.zeros((16, d), F32).at[:b].set(c.astype(F32)).at[b].set(c_ctx.astype(F32))
    cos, sa, sb = _rope_tables(s, lc)
    tables = (cos, sa, sb, _block_ones(A_QK_DIM), _block_ones(N_HEAD_DIM))

    for layer in range(depth):
        last = layer == depth - 1
        with_ctx = not last
        lam_init = 0.8 - 0.6 * math.exp(-0.3 * layer)

        mod = _modulation(cond_rows, w_mod[layer].astype(F32), b_mod[layer].astype(F32))
        mod_lat = mod[:b].reshape(b, 1, 6, d)
        mod_ctx = jnp.broadcast_to(mod[b].reshape(1, 1, 6, d), (b, 1, 6, d))
        modall = jnp.concatenate([mod_ctx, mod_lat], axis=1)

        gains = ((jnp.tile(a_gq[layer].astype(F32), HW // A_QK_DIM) * A_QK_DIM ** -0.5).reshape(1, HW),
                 jnp.tile(a_gk[layer].astype(F32), HW // A_QK_DIM).reshape(1, HW),
                 (jnp.tile(n_gq[layer].astype(F32), N_HEADS) * N_HEAD_DIM ** -0.5).reshape(1, HW),
                 jnp.tile(n_gk[layer].astype(F32), N_HEADS).reshape(1, HW))
        qa, kat, va, qn, knt, vn, u = _in_projection(xs, modall, g_mix[layer].astype(F32), w_in[layer].astype(BF16),
                                                     tables, gains, lc)

        lv = a_lambda[layer].astype(F32)
        lam = (jnp.exp(jnp.sum(lv[0] * lv[1])) - jnp.exp(jnp.sum(lv[2] * lv[3])) + lam_init).reshape(1, 1)
        gsub_t = (jnp.tile(a_gsub[layer].astype(F32), A_HEADS) * (1.0 - lam_init)).reshape(1, HW)
        oa = _diff_attention(qa, kat, va, lam, gsub_t, lc, with_ctx)

        on = _na_attention(qn, knt, vn, _na_bias(n_rpb[layer], n_img_rows), lc, with_ctx)

        u_tm = jnp.transpose(u, (1, 0, 2))
        (bm_f, cm_f, ar_f, ai_f), (bm_b, cm_b, ar_b, ai_b) = _s5_params(
            s5_a_re[layer], s5_a_im[layer], s5_log_dt[layer], s5_b_re[layer], s5_b_im[layer],
            s5_c_re[layer], s5_c_im[layer], b)
        y_tm = _s5_scan(u_tm, bm_f, cm_f, ar_f, ai_f, lc, False) + _s5_scan(u_tm, bm_b, cm_b, ar_b, ai_b, lc, True)
        y = jnp.transpose(y_tm, (1, 0, 2))

        w_router_pad = jnp.zeros((d, 128), F32).at[:, :N_EXPERTS].set(w_router[layer].astype(F32))
        x_new, h2, logits = _out_projection(oa, on, y, u, xs, modall, g_ffn[layer].astype(F32),
                                            w_out[layer].astype(BF16), s5_d[layer].astype(F32),
                                            s5_w_glu[layer].astype(BF16), s5_b_glu[layer].astype(F32),
                                            w_router_pad, lc, with_ctx)
        rows = x_new.shape[1]
        t = b * rows
        gates = _route(logits.reshape(t, 128)[:, :N_EXPERTS], e_bias[layer])
        gates = jnp.concatenate([gates, jnp.ones((t, 1), F32)], axis=1)
        nsteps = (N_EXPERTS + 1) // MOE_EB
        gates_blk = jnp.zeros((nsteps, t, 128), F32).at[:, :, :MOE_EB].set(
            jnp.transpose(gates.reshape(t, nsteps, MOE_EB), (1, 0, 2)))

        gt2 = modall[:, :, 5, :]
        tiles_per_batch = rows // ROW_TILE
        if with_ctx:
            nctx = lc // ROW_TILE
            sel = (jnp.arange(tiles_per_batch) >= nctx).astype(jnp.int32)
        else:
            sel = jnp.ones((tiles_per_batch,), jnp.int32)
        gt_rows = gt2[:, sel, :].reshape(t // _moe_tile(t), _moe_tile(t) // ROW_TILE, d)

        wg = jnp.concatenate([w_gate[layer], ws_gate[layer][None]], axis=0).astype(BF16)
        wu = jnp.concatenate([w_up[layer], ws_up[layer][None]], axis=0).astype(BF16)
        wd = jnp.concatenate([w_down[layer], ws_down[layer][None]], axis=0).astype(BF16).reshape(-1, d)
        out = _moe(h2.reshape(t, d), gates_blk, x_new.reshape(t, d), gt_rows, wg, wu, wd)
        xs = out.reshape(b, rows, d)

    return xs.astype(x.dtype)
```

```python
import functools
import math

import jax
import jax.numpy as jnp
from jax import lax
from jax.experimental import pallas as pl
from jax.experimental.pallas import tpu as pltpu

F32 = jnp.float32
BF16 = jnp.bfloat16

D_MODEL = 1024
GRID_W = 64
EPS = 1e-6
A_HEADS = 6
A_QK_DIM = 32
A_V_DIM = 64
ROPE_THETA = 10000.0
S5_GROUPS = 16
S5_GROUP_CH = 16
S5_STATE = 64
N_HEADS = 6
N_HEAD_DIM = 64
WIN_ROWS = 8
WIN_COLS = 16
HW = 384
B_WIDTH = 256
Q_COLS = 768
IN_COLS = 2560
N_EXPERTS = 64
TOP_K = 8
N_GROUPS = 8
TOPK_GROUPS = 4
EXPERT_DIM = 256
ROUTED_SCALE = 2.5

ROW_TILE = 256
NA_ROWS = 4
NA_KROWS = 12
S5_CHUNK = 128
MOE_TILE = 1024
MOE_EB = 5
NEG = -1e30
VMEM_LIMIT = 56 * 1024 * 1024


def _sigmoid(x):
    return 1.0 / (1.0 + jnp.exp(-x))


def _gelu_tanh(x):
    return 0.5 * x * (1.0 + jnp.tanh(math.sqrt(2.0 / math.pi) * (x + 0.044715 * (x * x * x))))


def _split_bf16(a):
    hi = a.astype(BF16)
    lo = (a - hi.astype(F32)).astype(BF16)
    return hi, lo


def _dot(a, b):
    return jnp.dot(a, b, preferred_element_type=F32)


def _dot3(a, b):
    ah, al = _split_bf16(a)
    bh, bl = _split_bf16(b)
    return _dot(ah, bh) + _dot(ah, bl) + _dot(al, bh)


def _cparams(sem):
    return pltpu.CompilerParams(dimension_semantics=sem, vmem_limit_bytes=VMEM_LIMIT)


def _mod_kernel(c_ref, w_ref, b_ref, o_ref):
    c = c_ref[...]
    cond = c * _sigmoid(c)
    o_ref[...] = _dot3(cond, w_ref[...]) + b_ref[...]


def _modulation(cond_rows, w_mod, b_mod):
    r, d = cond_rows.shape
    n = w_mod.shape[1]
    tn = 1536
    return pl.pallas_call(
        _mod_kernel,
        out_shape=jax.ShapeDtypeStruct((r, n), F32),
        grid=(n // tn,),
        in_specs=[pl.BlockSpec((r, d), lambda j: (0, 0)),
                  pl.BlockSpec((d, tn), lambda j: (0, j)),
                  pl.BlockSpec((1, tn), lambda j: (0, j))],
        out_specs=pl.BlockSpec((r, tn), lambda j: (0, j)),
        compiler_params=_cparams(("arbitrary",)),
        name="adaln_mod",
    )(cond_rows, w_mod, b_mod.reshape(1, n))


def _group_rms(t, ones_ref, gain_ref, group):
    ms = _dot((t * t).astype(BF16), ones_ref[...]) * (1.0 / group)
    return t * lax.rsqrt(ms + EPS) * gain_ref[...]


def _rope(t, cos_ref, sa_ref, sb_ref):
    up = pltpu.roll(t, HW - 8, 1)
    dn = pltpu.roll(t, 8, 1)
    return t * cos_ref[...] + up * sa_ref[...] + dn * sb_ref[...]


def _proj_kernel(x_ref, mod_ref, g_ref, w_ref, cos_ref, sa_ref, sb_ref, ones32_ref, ones64_ref,
                 gqa_ref, gka_ref, gqn_ref, gkn_ref,
                 qa_ref, kat_ref, va_ref, qn_ref, knt_ref, vn_ref, u_ref):
    x = x_ref[0]
    mod = mod_ref[0, 0]
    ms = jnp.mean(x * x, axis=-1, keepdims=True)
    h = x * lax.rsqrt(ms + EPS) * g_ref[...] * (1.0 + mod[1:2]) + mod[0:1]
    hb = h.astype(BF16)

    def sec(a, b):
        return _dot(hb, w_ref[:, a:b])

    qa = _rope(_group_rms(sec(0, 384), ones32_ref, gqa_ref, A_QK_DIM), cos_ref, sa_ref, sb_ref)
    qa_ref[0] = qa.astype(BF16)
    qn_ref[0] = _group_rms(sec(384, 768), ones64_ref, gqn_ref, N_HEAD_DIM).astype(BF16)
    ka = _rope(_group_rms(sec(768, 1152), ones32_ref, gka_ref, A_QK_DIM), cos_ref, sa_ref, sb_ref)
    kat_ref[0] = ka.T.astype(BF16)
    va_ref[0] = sec(1152, 1536).astype(BF16)
    kn = _group_rms(sec(1536, 1920), ones64_ref, gkn_ref, N_HEAD_DIM)
    knt_ref[0] = kn.T.astype(BF16)
    vn_ref[0] = sec(1920, 2304).astype(BF16)
    u_ref[0] = sec(2304, 2560)


def _in_projection(xs, modall, g_mix, w_in_bf, tables, gains, lc):
    b, s, d = xs.shape
    tm = ROW_TILE
    cos, sa, sb, ones32, ones64 = tables
    row = lambda i, bb: (bb, i, 0)
    tab = lambda i, bb: (i, 0)
    const2 = lambda i, bb: (0, 0)
    act = lambda w, dt: jax.ShapeDtypeStruct((b, s, w), dt)
    act_t = jax.ShapeDtypeStruct((b, HW, s), BF16)
    return pl.pallas_call(
        _proj_kernel,
        out_shape=(act(HW, BF16), act_t, act(HW, BF16), act(HW, BF16), act_t, act(HW, BF16), act(B_WIDTH, F32)),
        grid=(s // tm, b),
        in_specs=[pl.BlockSpec((1, tm, d), row),
                  pl.BlockSpec((1, 1, 6, d), lambda i, bb: (bb, jnp.minimum(i, 1), 0, 0)),
                  pl.BlockSpec((1, d), const2),
                  pl.BlockSpec((d, IN_COLS), const2),
                  pl.BlockSpec((tm, HW), tab), pl.BlockSpec((tm, HW), tab), pl.BlockSpec((tm, HW), tab),
                  pl.BlockSpec((HW, HW), const2), pl.BlockSpec((HW, HW), const2),
                  pl.BlockSpec((1, HW), const2), pl.BlockSpec((1, HW), const2),
                  pl.BlockSpec((1, HW), const2), pl.BlockSpec((1, HW), const2)],
        out_specs=(pl.BlockSpec((1, tm, HW), row),
                   pl.BlockSpec((1, HW, tm), lambda i, bb: (bb, 0, i)),
                   pl.BlockSpec((1, tm, HW), row),
                   pl.BlockSpec((1, tm, HW), row),
                   pl.BlockSpec((1, HW, tm), lambda i, bb: (bb, 0, i)),
                   pl.BlockSpec((1, tm, HW), row),
                   pl.BlockSpec((1, tm, B_WIDTH), row)),
        compiler_params=_cparams(("arbitrary", "arbitrary")),
        name="in_proj",
    )(xs, modall, g_mix.reshape(1, d), w_in_bf, cos, sa, sb, ones32, ones64, *gains)


def _rope_tables(s, lc):
    p = jnp.arange(s)
    pos = jnp.maximum(p - lc, 0)
    rows = (pos // GRID_W).astype(F32)
    cols = (pos % GRID_W).astype(F32)
    lane = jnp.arange(HW)
    j32 = lane % A_QK_DIM
    half = j32 // 16
    i16 = j32 % 16
    nf = 8
    inv = ROPE_THETA ** (-(i16 % nf).astype(F32) / nf)
    coord = jnp.where(half[None, :] == 0, rows[:, None], cols[:, None])
    ang = coord * inv[None, :]
    is_lat = (p >= lc)[:, None]
    second = (i16 >= nf)[None, :]
    cos = jnp.where(is_lat, jnp.cos(ang), 1.0)
    sin = jnp.where(is_lat, jnp.sin(ang), 0.0)
    sa = jnp.where(second, 0.0, -sin)
    sb = jnp.where(second, sin, 0.0)
    return cos.astype(F32), sa.astype(F32), sb.astype(F32)


def _block_ones(group):
    g = jnp.arange(HW) // group
    return (g[:, None] == g[None, :]).astype(BF16)


def _diff_attend(q_all, kt_ref, v_ref, lam, gsub_ref, sk):
    outs = []
    for h in range(A_HEADS):
        es, rs = [], []
        for sub in range(2):
            off = h * 2 * A_QK_DIM + sub * A_QK_DIM
            sc = _dot(q_all[:, off:off + A_QK_DIM], kt_ref[0, off:off + A_QK_DIM, 0:sk])
            e = jnp.exp(sc - jnp.max(sc, axis=-1, keepdims=True))
            es.append(e)
            rs.append(1.0 / jnp.sum(e, axis=-1, keepdims=True))
        pd = es[0] * rs[0] - es[1] * (lam * rs[1])
        o = _dot(pd.astype(BF16), v_ref[0, 0:sk, h * A_V_DIM:(h + 1) * A_V_DIM])
        outs.append(o * lax.rsqrt(jnp.mean(o * o, axis=-1, keepdims=True) + EPS))
    return (jnp.concatenate(outs, axis=-1) * gsub_ref[...]).astype(BF16)


def _diff_attn_kernel(q_ref, kt_ref, v_ref, lam_ref, gsub_ref, o_ref, *, lc, ctx_first):
    lam = lam_ref[...]
    s = kt_ref.shape[2]
    if ctx_first:
        i = pl.program_id(1)

        @pl.when(i == 0)
        def _():
            o_ref[0] = _diff_attend(q_ref[0], kt_ref, v_ref, lam, gsub_ref, lc)

        @pl.when(i > 0)
        def _():
            o_ref[0] = _diff_attend(q_ref[0], kt_ref, v_ref, lam, gsub_ref, s)
    else:
        o_ref[0] = _diff_attend(q_ref[0], kt_ref, v_ref, lam, gsub_ref, s)


def _diff_attention(qa, kat, va, lam, gsub_t, lc, with_ctx):
    b, s, _ = qa.shape
    tq = ROW_TILE
    off = 0 if with_ctx else lc // tq
    rows_out = s - off * tq
    return pl.pallas_call(
        functools.partial(_diff_attn_kernel, lc=lc, ctx_first=with_ctx),
        out_shape=jax.ShapeDtypeStruct((b, rows_out, HW), BF16),
        grid=(b, rows_out // tq),
        in_specs=[pl.BlockSpec((1, tq, HW), lambda bb, i: (bb, i + off, 0)),
                  pl.BlockSpec((1, HW, s), lambda bb, i: (bb, 0, 0)),
                  pl.BlockSpec((1, s, HW), lambda bb, i: (bb, 0, 0)),
                  pl.BlockSpec((1, 1), lambda bb, i: (0, 0)),
                  pl.BlockSpec((1, HW), lambda bb, i: (0, 0))],
        out_specs=pl.BlockSpec((1, tq, HW), lambda bb, i: (bb, i, 0)),
        compiler_params=_cparams(("arbitrary", "arbitrary")),
        name="diff_attn",
    )(qa, kat, va, lam, gsub_t)


def _na_ctx_attend(q_all, kt_ref, v_ref, lc):
    outs = []
    for h in range(N_HEADS):
        hs = slice(h * N_HEAD_DIM, (h + 1) * N_HEAD_DIM)
        sc = _dot(q_all[:, hs], kt_ref[0, hs, 0:lc])
        e = jnp.exp(sc - jnp.max(sc, axis=-1, keepdims=True))
        o = _dot(e.astype(BF16), v_ref[0, 0:lc, hs])
        outs.append(o * (1.0 / jnp.sum(e, axis=-1, keepdims=True)))
    return jnp.concatenate(outs, axis=-1).astype(BF16)


def _na_attend(q_all, kt_ref, v_ref, bias_ref, koff, lc):
    nk = NA_KROWS * GRID_W
    outs = []
    for h in range(N_HEADS):
        hs = slice(h * N_HEAD_DIM, (h + 1) * N_HEAD_DIM)
        q = q_all[:, hs]
        s_loc = _dot(q, kt_ref[0, hs, pl.ds(koff, nk)]) + bias_ref[0, h]
        s_ctx = _dot(q, kt_ref[0, hs, 0:lc])
        m = jnp.maximum(jnp.max(s_loc, axis=-1, keepdims=True), jnp.max(s_ctx, axis=-1, keepdims=True))
        e_loc = jnp.exp(s_loc - m)
        e_ctx = jnp.exp(s_ctx - m)
        den = jnp.sum(e_loc, axis=-1, keepdims=True) + jnp.sum(e_ctx, axis=-1, keepdims=True)
        o = _dot(e_loc.astype(BF16), v_ref[0, pl.ds(koff, nk), hs]) + _dot(e_ctx.astype(BF16), v_ref[0, 0:lc, hs])
        outs.append(o * (1.0 / den))
    return jnp.concatenate(outs, axis=-1).astype(BF16)


def _na_kernel(q_ref, kt_ref, v_ref, bias_ref, o_ref, *, lc, n_img_rows, ctx_first):
    i = pl.program_id(1)
    blk = i - 1 if ctx_first else i
    start_row = jnp.clip(NA_ROWS * blk - WIN_ROWS // 2, 0, n_img_rows - NA_KROWS)
    koff = pl.multiple_of(lc + start_row * GRID_W, 128)
    if ctx_first:
        @pl.when(i == 0)
        def _():
            o_ref[0] = _na_ctx_attend(q_ref[0], kt_ref, v_ref, lc)

        @pl.when(i > 0)
        def _():
            o_ref[0] = _na_attend(q_ref[0], kt_ref, v_ref, bias_ref, koff, lc)
    else:
        o_ref[0] = _na_attend(q_ref[0], kt_ref, v_ref, bias_ref, koff, lc)


def _na_attention(qn, knt, vn, bias, lc, with_ctx):
    b, s, _ = qn.shape
    tq = NA_ROWS * GRID_W
    assert tq == ROW_TILE and lc % tq == 0
    n_img_rows = (s - lc) // GRID_W
    nblk = n_img_rows // NA_ROWS
    off = 0 if with_ctx else lc // tq
    rows_out = s - off * tq
    first = 1 if with_ctx else 0

    def variant(bb, i):
        blk = i - first
        return (jnp.where(blk <= 0, 0, jnp.where(blk == nblk - 1, 2, 1)), 0, 0, 0)

    return pl.pallas_call(
        functools.partial(_na_kernel, lc=lc, n_img_rows=n_img_rows, ctx_first=with_ctx),
        out_shape=jax.ShapeDtypeStruct((b, rows_out, HW), BF16),
        grid=(b, rows_out // tq),
        in_specs=[pl.BlockSpec((1, tq, HW), lambda bb, i: (bb, i + off, 0)),
                  pl.BlockSpec((1, HW, s), lambda bb, i: (bb, 0, 0)),
                  pl.BlockSpec((1, s, HW), lambda bb, i: (bb, 0, 0)),
                  pl.BlockSpec((1, N_HEADS, tq, NA_KROWS * GRID_W), variant)],
        out_specs=pl.BlockSpec((1, tq, HW), lambda bb, i: (bb, i, 0)),
        compiler_params=_cparams(("arbitrary", "arbitrary")),
        name="nbr_attn",
    )(qn, knt, vn, bias)


def _na_bias(rpb, n_img_rows):
    a = jnp.arange(NA_ROWS)[:, None, None, None]
    cq = jnp.arange(GRID_W)[None, :, None, None]
    j = jnp.arange(NA_KROWS)[None, None, :, None]
    ck = jnp.arange(GRID_W)[None, None, None, :]
    cstart = jnp.clip(cq - WIN_COLS // 2, 0, GRID_W - WIN_COLS)
    colmask = (ck >= cstart) & (ck < cstart + WIN_COLS)
    dc = jnp.clip(ck - cq, -(WIN_COLS - 1), WIN_COLS - 1) + (WIN_COLS - 1)
    out = []
    for r0_minus_k, wstart in ((0, 0 * a), (WIN_ROWS // 2, a), (NA_KROWS - NA_ROWS, NA_KROWS - WIN_ROWS + 0 * a)):
        inwin = (j >= wstart) & (j < wstart + WIN_ROWS)
        dr = jnp.clip(j - r0_minus_k - a + (WIN_ROWS - 1), 0, 2 * WIN_ROWS - 2)
        dr_b, dc_b = jnp.broadcast_arrays(dr, dc)
        vals = rpb.astype(F32)[:, dr_b, dc_b]
        vals = jnp.where((inwin & colmask)[None], vals, NEG)
        out.append(vals.reshape(N_HEADS, NA_ROWS * GRID_W, NA_KROWS * GRID_W))
    return jnp.stack(out)


def _s5_kernel(u_ref, bm_ref, cm_ref, are_ref, aim_ref, y_ref, x_scr, st_scr, *, reverse):
    tc, nb, w = u_ref.shape
    ns = are_ref.shape[1]

    @pl.when(pl.program_id(0) == 0)
    def _():
        st_scr[...] = jnp.zeros_like(st_scr)

    u = u_ref[...].reshape(tc * nb, w).astype(BF16)
    x_scr[...] = _dot(u, bm_ref[...])
    a_re = are_ref[...]
    a_im = aim_ref[...]

    def step(t, carry):
        s_re, s_im = carry
        tt = (tc - 1 - t) if reverse else t
        rows = pl.ds(pl.multiple_of(tt * nb, nb), nb)
        n_re = a_re * s_re - a_im * s_im + x_scr[rows, 0:ns]
        n_im = a_re * s_im + a_im * s_re + x_scr[rows, ns:2 * ns]
        x_scr[rows, 0:ns] = n_re
        x_scr[rows, ns:2 * ns] = n_im
        return n_re, n_im

    s_re, s_im = lax.fori_loop(0, tc, step, (st_scr[:, 0:ns], st_scr[:, ns:2 * ns]), unroll=4)
    st_scr[:, 0:ns] = s_re
    st_scr[:, ns:2 * ns] = s_im
    y = _dot(x_scr[...].astype(BF16), cm_ref[...])
    y_ref[...] = y.reshape(tc, nb, w)


def _s5_scan(u_tm, bmat, cmat, a_re, a_im, lc, reverse):
    s, nb, w = u_tm.shape
    tc = S5_CHUNK
    nc, ncc = s // tc, lc // tc
    ns = a_re.shape[1]
    if reverse:
        chunk = lambda j: (jnp.where(j < ncc, ncc - 1 - j, nc - 1 - (j - ncc)), 0, 0)
    else:
        chunk = lambda j: (j, 0, 0)
    const = lambda j: (0, 0)
    return pl.pallas_call(
        functools.partial(_s5_kernel, reverse=reverse),
        out_shape=jax.ShapeDtypeStruct((s, nb, w), F32),
        grid=(nc,),
        in_specs=[pl.BlockSpec((tc, nb, w), chunk),
                  pl.BlockSpec((w, 2 * ns), const),
                  pl.BlockSpec((2 * ns, w), const),
                  pl.BlockSpec((nb, ns), const),
                  pl.BlockSpec((nb, ns), const)],
        out_specs=pl.BlockSpec((tc, nb, w), chunk),
        scratch_shapes=[pltpu.VMEM((tc * nb, 2 * ns), F32), pltpu.VMEM((nb, 2 * ns), F32)],
        compiler_params=_cparams(("arbitrary",)),
        name="s5_scan_bwd" if reverse else "s5_scan_fwd",
    )(u_tm, bmat, cmat, a_re, a_im)


def _s5_params(a_re, a_im, log_dt, b_re, b_im, c_re, c_im, nb):
    g, n, p = S5_GROUPS, S5_STATE, S5_GROUP_CH
    lam = lax.complex(a_re.astype(F32), a_im.astype(F32))
    dt = jnp.exp(log_dt.astype(F32))[..., None]
    a_bar = jnp.exp(lam * dt)
    b_bar = ((a_bar - 1.0) / lam)[..., None] * lax.complex(b_re.astype(F32), b_im.astype(F32))
    eye = jnp.eye(g, dtype=F32)
    out = []
    for k in range(2):
        bt = jnp.transpose(b_bar[k], (0, 2, 1))
        b_r = jnp.einsum('gpn,gh->gphn', jnp.real(bt), eye).reshape(g * p, g * n)
        b_i = jnp.einsum('gpn,gh->gphn', jnp.imag(bt), eye).reshape(g * p, g * n)
        bmat = jnp.concatenate([b_r, b_i], axis=1).astype(BF16)
        ct = jnp.transpose(c_re[k].astype(F32), (0, 2, 1))
        ci = jnp.transpose(c_im[k].astype(F32), (0, 2, 1))
        c_r = jnp.einsum('gnp,gh->gnhp', ct, eye).reshape(g * n, g * p)
        c_i = jnp.einsum('gnp,gh->gnhp', ci, eye).reshape(g * n, g * p)
        cmat = jnp.concatenate([c_r, -c_i], axis=0).astype(BF16)
        ar = jnp.broadcast_to(jnp.real(a_bar[k]).reshape(1, g * n), (nb, g * n))
        ai = jnp.broadcast_to(jnp.imag(a_bar[k]).reshape(1, g * n), (nb, g * n))
        out.append((bmat, cmat, ar, ai))
    return out


def _out_kernel(oa_ref, on_ref, y_ref, u_ref, x_ref, mod_ref, gffn_ref, wo_ref, dskip_ref, wglu_ref, bglu_ref,
                wr_ref, xo_ref, h2_ref, lg_ref):
    mod = mod_ref[0, 0]
    g = _gelu_tanh(y_ref[0] + dskip_ref[...] * u_ref[0])
    ob = g * _sigmoid(_dot(g.astype(BF16), wglu_ref[...]) + bglu_ref[...])
    mix = (_dot(oa_ref[0], wo_ref[0:HW, :]) + _dot(ob.astype(BF16), wo_ref[HW:HW + B_WIDTH, :])
           + _dot(on_ref[0], wo_ref[HW + B_WIDTH:, :]))
    x = x_ref[0] + mod[2:3] * mix
    xo_ref[0] = x
    ms = jnp.mean(x * x, axis=-1, keepdims=True)
    h2 = x * lax.rsqrt(ms + EPS) * gffn_ref[...] * (1.0 + mod[4:5]) + mod[3:4]
    h2_ref[0] = h2.astype(BF16)
    nt = lambda a, bb: lax.dot_general(a, bb, (((1,), (1,)), ((), ())), preferred_element_type=F32)
    wh, wl = _split_bf16(wr_ref[...])
    hh, hl = _split_bf16(h2)
    lg_ref[0] = nt(wh, hh) + nt(wh, hl) + nt(wl, hh)


def _out_projection(oa, on, y, u, xs, modall, g_ffn, w_out_bf, d_skip, w_glu_bf, b_glu, w_router_pad, lc, with_ctx):
    b, s, d = xs.shape
    tm = ROW_TILE
    off = 0 if with_ctx else lc // tm
    rows_out = s - off * tm
    full = lambda bb, i: (bb, i + off, 0)
    outr = lambda bb, i: (bb, i, 0)
    const = lambda bb, i: (0, 0)
    ne = w_router_pad.shape[0]
    return pl.pallas_call(
        _out_kernel,
        out_shape=(jax.ShapeDtypeStruct((b, rows_out, d), F32),
                   jax.ShapeDtypeStruct((b, rows_out, d), BF16),
                   jax.ShapeDtypeStruct((b, ne, rows_out), F32)),
        grid=(b, rows_out // tm),
        in_specs=[pl.BlockSpec((1, tm, HW), outr),
                  pl.BlockSpec((1, tm, HW), outr),
                  pl.BlockSpec((1, tm, B_WIDTH), full),
                  pl.BlockSpec((1, tm, B_WIDTH), full),
                  pl.BlockSpec((1, tm, d), full),
                  pl.BlockSpec((1, 1, 6, d), lambda bb, i: (bb, jnp.minimum(i + off, 1), 0, 0)),
                  pl.BlockSpec((1, d), const),
                  pl.BlockSpec((d, d), const),
                  pl.BlockSpec((1, B_WIDTH), const),
                  pl.BlockSpec((B_WIDTH, B_WIDTH), const),
                  pl.BlockSpec((1, B_WIDTH), const),
                  pl.BlockSpec((ne, d), const)],
        out_specs=(pl.BlockSpec((1, tm, d), outr), pl.BlockSpec((1, tm, d), outr),
                   pl.BlockSpec((1, ne, tm), lambda bb, i: (bb, 0, i))),
        compiler_params=_cparams(("arbitrary", "arbitrary")),
        name="out_proj",
    )(oa, on, y, u, xs, modall, g_ffn.reshape(1, d), w_out_bf, d_skip.reshape(1, -1), w_glu_bf,
      b_glu.reshape(1, -1), w_router_pad)


def _moe_kernel(h_ref, gates_ref, x_ref, gt_ref, wg_ref, wu_ref, wd_ref, o_ref, acc_ref):
    e = pl.program_id(1)

    @pl.when(e == 0)
    def _():
        acc_ref[...] = jnp.zeros_like(acc_ref)

    h = h_ref[...]
    gates = gates_ref[0]
    hid = []
    for j in range(MOE_EB):
        a = _dot(h, wg_ref[j])
        up = _dot(h, wu_ref[j])
        hid.append((a * _sigmoid(a) * up * gates[:, j:j + 1]).astype(BF16))
    hidden = jnp.concatenate(hid, axis=-1)
    acc_ref[...] += _dot(hidden, wd_ref[...])

    @pl.when(e == pl.num_programs(1) - 1)
    def _():
        nsub = h_ref.shape[0] // ROW_TILE
        for j in range(nsub):
            rows = slice(j * ROW_TILE, (j + 1) * ROW_TILE)
            o_ref[rows, :] = x_ref[rows, :] + gt_ref[0, j:j + 1, :] * acc_ref[rows, :]


def _moe_tile(t):
    return max(m for m in range(ROW_TILE, MOE_TILE + 1, ROW_TILE) if t % m == 0)


def _moe(h2, gates_blk, x, gt_rows, wg, wu, wd):
    t, d = h2.shape
    tm = _moe_tile(t)
    ne = wg.shape[0]
    nsteps = ne // MOE_EB
    nsub = tm // ROW_TILE
    return pl.pallas_call(
        _moe_kernel,
        out_shape=jax.ShapeDtypeStruct((t, d), F32),
        grid=(t // tm, nsteps),
        in_specs=[pl.BlockSpec((tm, d), lambda i, e: (i, 0)),
                  pl.BlockSpec((1, tm, 128), lambda i, e: (e, i, 0)),
                  pl.BlockSpec((tm, d), lambda i, e: (i, 0)),
                  pl.BlockSpec((1, nsub, d), lambda i, e: (i, 0, 0)),
                  pl.BlockSpec((MOE_EB, d, EXPERT_DIM), lambda i, e: (e, 0, 0)),
                  pl.BlockSpec((MOE_EB, d, EXPERT_DIM), lambda i, e: (e, 0, 0)),
                  pl.BlockSpec((MOE_EB * EXPERT_DIM, d), lambda i, e: (e, 0))],
        out_specs=pl.BlockSpec((tm, d), lambda i, e: (i, 0)),
        scratch_shapes=[pltpu.VMEM((tm, d), F32)],
        compiler_params=_cparams(("arbitrary", "arbitrary")),
        name="moe_ffn",
    )(h2, gates_blk, x, gt_rows, wg, wu, wd)


def _route_kernel(lg_ref, bias_ref, o_ref):
    gsz = N_EXPERTS // N_GROUPS
    tn = lg_ref.shape[2]
    ninf = -jnp.inf
    jidx = lax.broadcasted_iota(jnp.int32, (gsz, tn), 0)
    scores, biased, gscore = [], [], []
    for g in range(N_GROUPS):
        rows = slice(g * gsz, (g + 1) * gsz)
        sc = _sigmoid(lg_ref[0, rows, :])
        bi = sc + bias_ref[rows, :]
        m1 = jnp.max(bi, axis=0, keepdims=True)
        first = jnp.min(jnp.where(bi == m1, jidx, gsz), axis=0, keepdims=True)
        m2 = jnp.max(jnp.where(jidx == first, ninf, bi), axis=0, keepdims=True)
        scores.append(sc)
        biased.append(bi)
        gscore.append(m1 + m2)
    masked = []
    for g in range(N_GROUPS):
        rank = jnp.zeros((1, tn), F32)
        for g2 in range(N_GROUPS):
            if g2 != g:
                ahead = (gscore[g2] >= gscore[g]) if g2 < g else (gscore[g2] > gscore[g])
                rank = rank + jnp.where(ahead, 1.0, 0.0)
        keep = jnp.broadcast_to(rank, (gsz, tn)) < TOPK_GROUPS
        masked.append(jnp.where(keep, biased[g], ninf))
    ranks = [jnp.zeros((gsz, tn), F32) for _ in range(N_GROUPS)]
    for g2 in range(N_GROUPS):
        for j2 in range(gsz):
            other = jnp.broadcast_to(masked[g2][j2:j2 + 1, :], (gsz, tn))
            for g in range(N_GROUPS):
                ge = jnp.where(other >= masked[g], 1.0, 0.0)
                gt = jnp.where(other > masked[g], 1.0, 0.0)
                if g2 < g:
                    ahead = ge
                elif g2 > g:
                    ahead = gt
                else:
                    ahead = jnp.where(jidx > j2, ge, gt)
                ranks[g] = ranks[g] + ahead
    picked = [jnp.where(ranks[g] < TOP_K, scores[g], 0.0) for g in range(N_GROUPS)]
    den = sum(jnp.sum(p, axis=0, keepdims=True) for p in picked)
    scale = ROUTED_SCALE / den
    shared_row = jnp.where(lax.broadcasted_iota(jnp.int32, (128 - N_EXPERTS, tn), 0) == 0, 1.0, 0.0)
    gates_t = jnp.concatenate([p * scale for p in picked] + [shared_row], axis=0)
    o_ref[...] = gates_t.T


def _route(logits_t, e_bias):
    b, ne, rows = logits_t.shape
    tn = ROW_TILE
    nt = rows // tn
    bias = jnp.broadcast_to(e_bias.astype(F32)[:, None], (N_EXPERTS, tn))
    return pl.pallas_call(
        _route_kernel,
        out_shape=jax.ShapeDtypeStruct((b * rows, ne), F32),
        grid=(b, nt),
        in_specs=[pl.BlockSpec((1, ne, tn), lambda bb, i: (bb, 0, i)),
                  pl.BlockSpec((N_EXPERTS, tn), lambda bb, i: (0, 0))],
        out_specs=pl.BlockSpec((tn, ne), lambda bb, i: (bb * nt + i, 0)),
        compiler_params=_cparams(("arbitrary", "arbitrary")),
        name="moe_route",
    )(logits_t, bias)


def kernel(x, c, ctx, c_ctx, w_mod, b_mod, g_mix, g_ffn, w_in, w_out, a_gq, a_gk, a_lambda, a_gsub, n_gq, n_gk, n_rpb, s5_a_re, s5_a_im, s5_log_dt, s5_b_re, s5_b_im, s5_c_re, s5_c_im, s5_d, s5_w_glu, s5_b_glu, w_router, e_bias, w_gate, w_up, w_down, ws_gate, ws_up, ws_down):
    b, l, d = x.shape
    lc = ctx.shape[1]
    s = lc + l
    depth = w_mod.shape[0]
    n_img_rows = l // GRID_W

    xs = jnp.concatenate([ctx, x], axis=1).astype(F32)
    cond_rows = jnp.zeros((16, d), F32).at[:b].set(c.astype(F32)).at[b].set(c_ctx.astype(F32))
    cos, sa, sb = _rope_tables(s, lc)
    tables = (cos, sa, sb, _block_ones(A_QK_DIM), _block_ones(N_HEAD_DIM))

    for layer in range(depth):
        last = layer == depth - 1
        with_ctx = not last
        lam_init = 0.8 - 0.6 * math.exp(-0.3 * layer)

        mod = _modulation(cond_rows, w_mod[layer].astype(F32), b_mod[layer].astype(F32))
        mod_lat = mod[:b].reshape(b, 1, 6, d)
        mod_ctx = jnp.broadcast_to(mod[b].reshape(1, 1, 6, d), (b, 1, 6, d))
        modall = jnp.concatenate([mod_ctx, mod_lat], axis=1)

        gains = ((jnp.tile(a_gq[layer].astype(F32), HW // A_QK_DIM) * A_QK_DIM ** -0.5).reshape(1, HW),
                 jnp.tile(a_gk[layer].astype(F32), HW // A_QK_DIM).reshape(1, HW),
                 (jnp.tile(n_gq[layer].astype(F32), N_HEADS) * N_HEAD_DIM ** -0.5).reshape(1, HW),
                 jnp.tile(n_gk[layer].astype(F32), N_HEADS).reshape(1, HW))
        qa, kat, va, qn, knt, vn, u = _in_projection(xs, modall, g_mix[layer].astype(F32), w_in[layer].astype(BF16),
                                                     tables, gains, lc)

        lv = a_lambda[layer].astype(F32)
        lam = (jnp.exp(jnp.sum(lv[0] * lv[1])) - jnp.exp(jnp.sum(lv[2] * lv[3])) + lam_init).reshape(1, 1)
        gsub_t = (jnp.tile(a_gsub[layer].astype(F32), A_HEADS) * (1.0 - lam_init)).reshape(1, HW)
        oa = _diff_attention(qa, kat, va, lam, gsub_t, lc, with_ctx)

        on = _na_attention(qn, knt, vn, _na_bias(n_rpb[layer], n_img_rows), lc, with_ctx)

        u_tm = jnp.transpose(u, (1, 0, 2))
        (bm_f, cm_f, ar_f, ai_f), (bm_b, cm_b, ar_b, ai_b) = _s5_params(
            s5_a_re[layer], s5_a_im[layer], s5_log_dt[layer], s5_b_re[layer], s5_b_im[layer],
            s5_c_re[layer], s5_c_im[layer], b)
        y_tm = _s5_scan(u_tm, bm_f, cm_f, ar_f, ai_f, lc, False) + _s5_scan(u_tm, bm_b, cm_b, ar_b, ai_b, lc, True)
        y = jnp.transpose(y_tm, (1, 0, 2))

        w_router_pad = jnp.zeros((128, d), F32).at[:N_EXPERTS].set(w_router[layer].astype(F32).T)
        x_new, h2, logits_t = _out_projection(oa, on, y, u, xs, modall, g_ffn[layer].astype(F32),
                                            w_out[layer].astype(BF16), s5_d[layer].astype(F32),
                                            s5_w_glu[layer].astype(BF16), s5_b_glu[layer].astype(F32),
                                            w_router_pad, lc, with_ctx)
        rows = x_new.shape[1]
        t = b * rows
        gates = _route(logits_t, e_bias[layer])[:, :N_EXPERTS + 1]
        nsteps = (N_EXPERTS + 1) // MOE_EB
        gates_blk = jnp.zeros((nsteps, t, 128), F32).at[:, :, :MOE_EB].set(
            jnp.transpose(gates.reshape(t, nsteps, MOE_EB), (1, 0, 2)))

        gt2 = modall[:, :, 5, :]
        tiles_per_batch = rows // ROW_TILE
        if with_ctx:
            nctx = lc // ROW_TILE
            sel = (jnp.arange(tiles_per_batch) >= nctx).astype(jnp.int32)
        else:
            sel = jnp.ones((tiles_per_batch,), jnp.int32)
        gt_rows = gt2[:, sel, :].reshape(t // _moe_tile(t), _moe_tile(t) // ROW_TILE, d)

        wg = jnp.concatenate([w_gate[layer], ws_gate[layer][None]], axis=0).astype(BF16)
        wu = jnp.concatenate([w_up[layer], ws_up[layer][None]], axis=0).astype(BF16)
        wd = jnp.concatenate([w_down[layer], ws_down[layer][None]], axis=0).astype(BF16).reshape(-1, d)
        out = _moe(h2.reshape(t, d), gates_blk, x_new.reshape(t, d), gt_rows, wg, wu, wd)
        xs = out.reshape(b, rows, d)

    return xs.astype(x.dtype)
```

```python
import functools
import math

import jax
import jax.numpy as jnp
from jax import lax
from jax.experimental import pallas as pl
from jax.experimental.pallas import tpu as pltpu

F32 = jnp.float32
BF16 = jnp.bfloat16

D_MODEL = 1024
GRID_W = 64
EPS = 1e-6
A_HEADS = 6
A_QK_DIM = 32
A_V_DIM = 64
ROPE_THETA = 10000.0
S5_GROUPS = 16
S5_GROUP_CH = 16
S5_STATE = 64
N_HEADS = 6
N_HEAD_DIM = 64
WIN_ROWS = 8
WIN_COLS = 16
HW = 384
B_WIDTH = 256
Q_COLS = 768
IN_COLS = 2560
N_EXPERTS = 64
TOP_K = 8
N_GROUPS = 8
TOPK_GROUPS = 4
EXPERT_DIM = 256
ROUTED_SCALE = 2.5

ROW_TILE = 256
NA_ROWS = 4
NA_KROWS = 12
S5_CHUNK = 128
MOE_TILE = 1024
MOE_EB = 4
NEG = -1e30
LOG2E = math.log2(math.e)
VMEM_LIMIT = 56 * 1024 * 1024


def _sigmoid(x):
    return 1.0 / (1.0 + jnp.exp(-x))


def _gelu_tanh(x):
    return 0.5 * x * (1.0 + jnp.tanh(math.sqrt(2.0 / math.pi) * (x + 0.044715 * (x * x * x))))


def _split_bf16(a):
    hi = a.astype(BF16)
    lo = (a - hi.astype(F32)).astype(BF16)
    return hi, lo


def _dot(a, b):
    return jnp.dot(a, b, preferred_element_type=F32)


def _dot3(a, b):
    ah, al = _split_bf16(a)
    bh, bl = _split_bf16(b)
    return _dot(ah, bh) + _dot(ah, bl) + _dot(al, bh)


def _cparams(sem):
    return pltpu.CompilerParams(dimension_semantics=sem, vmem_limit_bytes=VMEM_LIMIT)


def _mod_kernel(c_ref, w_ref, b_ref, o_ref):
    c = c_ref[...]
    cond = c * _sigmoid(c)
    o_ref[...] = _dot3(cond, w_ref[...]) + b_ref[...]


def _modulation(cond_rows, w_mod, b_mod):
    r, d = cond_rows.shape
    n = w_mod.shape[1]
    tn = 1536
    return pl.pallas_call(
        _mod_kernel,
        out_shape=jax.ShapeDtypeStruct((r, n), F32),
        grid=(n // tn,),
        in_specs=[pl.BlockSpec((r, d), lambda j: (0, 0)),
                  pl.BlockSpec((d, tn), lambda j: (0, j)),
                  pl.BlockSpec((1, tn), lambda j: (0, j))],
        out_specs=pl.BlockSpec((r, tn), lambda j: (0, j)),
        compiler_params=_cparams(("arbitrary",)),
        name="adaln_mod",
    )(cond_rows, w_mod, b_mod.reshape(1, n))


def _group_rms(t, ones_ref, gain_ref, group):
    ms = _dot((t * t).astype(BF16), ones_ref[...]) * (1.0 / group)
    return t * lax.rsqrt(ms + EPS) * gain_ref[...]


def _rope(t, cos_ref, sa_ref, sb_ref):
    up = pltpu.roll(t, HW - 8, 1)
    dn = pltpu.roll(t, 8, 1)
    return t * cos_ref[...] + up * sa_ref[...] + dn * sb_ref[...]


def _proj_kernel(x_ref, mod_ref, g_ref, w_ref, cos_ref, sa_ref, sb_ref, ones32_ref, ones64_ref,
                 gqa_ref, gka_ref, gqn_ref, gkn_ref,
                 qa_ref, kat_ref, va_ref, qn_ref, knt_ref, vn_ref, u_ref):
    x = x_ref[0]
    mod = mod_ref[0, 0]
    ms = jnp.mean(x * x, axis=-1, keepdims=True)
    h = x * lax.rsqrt(ms + EPS) * g_ref[...] * (1.0 + mod[1:2]) + mod[0:1]
    hb = h.astype(BF16)

    def sec(a, b):
        return _dot(hb, w_ref[:, a:b])

    qa = _rope(_group_rms(sec(0, 384), ones32_ref, gqa_ref, A_QK_DIM), cos_ref, sa_ref, sb_ref)
    qa_ref[0] = qa.astype(BF16)
    qn_ref[0] = _group_rms(sec(384, 768), ones64_ref, gqn_ref, N_HEAD_DIM).astype(BF16)
    ka = _rope(_group_rms(sec(768, 1152), ones32_ref, gka_ref, A_QK_DIM), cos_ref, sa_ref, sb_ref)
    kat_ref[0] = ka.T.astype(BF16)
    va_ref[0] = sec(1152, 1536).astype(BF16)
    kn = _group_rms(sec(1536, 1920), ones64_ref, gkn_ref, N_HEAD_DIM)
    knt_ref[0] = kn.T.astype(BF16)
    vn_ref[0] = sec(1920, 2304).astype(BF16)
    u_ref[0] = sec(2304, 2560)


def _in_projection(xs, modall, g_mix, w_in_bf, tables, gains, lc):
    b, s, d = xs.shape
    tm = ROW_TILE
    cos, sa, sb, ones32, ones64 = tables
    row = lambda i, bb: (bb, i, 0)
    tab = lambda i, bb: (i, 0)
    const2 = lambda i, bb: (0, 0)
    act = lambda w, dt: jax.ShapeDtypeStruct((b, s, w), dt)
    act_t = jax.ShapeDtypeStruct((b, HW, s), BF16)
    return pl.pallas_call(
        _proj_kernel,
        out_shape=(act(HW, BF16), act_t, act(HW, BF16), act(HW, BF16), act_t, act(HW, BF16), act(B_WIDTH, F32)),
        grid=(s // tm, b),
        in_specs=[pl.BlockSpec((1, tm, d), row),
                  pl.BlockSpec((1, 1, 6, d), lambda i, bb: (bb, jnp.minimum(i, 1), 0, 0)),
                  pl.BlockSpec((1, d), const2),
                  pl.BlockSpec((d, IN_COLS), const2),
                  pl.BlockSpec((tm, HW), tab), pl.BlockSpec((tm, HW), tab), pl.BlockSpec((tm, HW), tab),
                  pl.BlockSpec((HW, HW), const2), pl.BlockSpec((HW, HW), const2),
                  pl.BlockSpec((1, HW), const2), pl.BlockSpec((1, HW), const2),
                  pl.BlockSpec((1, HW), const2), pl.BlockSpec((1, HW), const2)],
        out_specs=(pl.BlockSpec((1, tm, HW), row),
                   pl.BlockSpec((1, HW, tm), lambda i, bb: (bb, 0, i)),
                   pl.BlockSpec((1, tm, HW), row),
                   pl.BlockSpec((1, tm, HW), row),
                   pl.BlockSpec((1, HW, tm), lambda i, bb: (bb, 0, i)),
                   pl.BlockSpec((1, tm, HW), row),
                   pl.BlockSpec((1, tm, B_WIDTH), row)),
        compiler_params=_cparams(("arbitrary", "arbitrary")),
        name="in_proj",
    )(xs, modall, g_mix.reshape(1, d), w_in_bf, cos, sa, sb, ones32, ones64, *gains)


def _rope_tables(s, lc):
    p = jnp.arange(s)
    pos = jnp.maximum(p - lc, 0)
    rows = (pos // GRID_W).astype(F32)
    cols = (pos % GRID_W).astype(F32)
    lane = jnp.arange(HW)
    j32 = lane % A_QK_DIM
    half = j32 // 16
    i16 = j32 % 16
    nf = 8
    inv = ROPE_THETA ** (-(i16 % nf).astype(F32) / nf)
    coord = jnp.where(half[None, :] == 0, rows[:, None], cols[:, None])
    ang = coord * inv[None, :]
    is_lat = (p >= lc)[:, None]
    second = (i16 >= nf)[None, :]
    cos = jnp.where(is_lat, jnp.cos(ang), 1.0)
    sin = jnp.where(is_lat, jnp.sin(ang), 0.0)
    sa = jnp.where(second, 0.0, -sin)
    sb = jnp.where(second, sin, 0.0)
    return cos.astype(F32), sa.astype(F32), sb.astype(F32)


def _block_ones(group):
    g = jnp.arange(HW) // group
    return (g[:, None] == g[None, :]).astype(BF16)


def _diff_attend(q_all, kt_ref, v_ref, lam, gsub_ref, sk):
    outs = []
    for h in range(A_HEADS):
        v = v_ref[0, 0:sk, h * A_V_DIM:(h + 1) * A_V_DIM]
        parts = []
        for sub in range(2):
            off = h * 2 * A_QK_DIM + sub * A_QK_DIM
            sc = _dot(q_all[:, off:off + A_QK_DIM], kt_ref[0, off:off + A_QK_DIM, 0:sk])
            e = jnp.exp2(sc - jnp.max(sc, axis=-1, keepdims=True))
            parts.append(_dot(e.astype(BF16), v) * (1.0 / jnp.sum(e, axis=-1, keepdims=True)))
        o = parts[0] - lam * parts[1]
        outs.append(o * lax.rsqrt(jnp.mean(o * o, axis=-1, keepdims=True) + EPS))
    return (jnp.concatenate(outs, axis=-1) * gsub_ref[...]).astype(BF16)


def _diff_attn_kernel(q_ref, kt_ref, v_ref, lam_ref, gsub_ref, o_ref, *, lc, ctx_first):
    lam = lam_ref[...]
    s = kt_ref.shape[2]
    if ctx_first:
        i = pl.program_id(1)

        @pl.when(i == 0)
        def _():
            o_ref[0] = _diff_attend(q_ref[0], kt_ref, v_ref, lam, gsub_ref, lc)

        @pl.when(i > 0)
        def _():
            o_ref[0] = _diff_attend(q_ref[0], kt_ref, v_ref, lam, gsub_ref, s)
    else:
        o_ref[0] = _diff_attend(q_ref[0], kt_ref, v_ref, lam, gsub_ref, s)


def _diff_attention(qa, kat, va, lam, gsub_t, lc, with_ctx):
    b, s, _ = qa.shape
    tq = ROW_TILE
    off = 0 if with_ctx else lc // tq
    rows_out = s - off * tq
    return pl.pallas_call(
        functools.partial(_diff_attn_kernel, lc=lc, ctx_first=with_ctx),
        out_shape=jax.ShapeDtypeStruct((b, rows_out, HW), BF16),
        grid=(b, rows_out // tq),
        in_specs=[pl.BlockSpec((1, tq, HW), lambda bb, i: (bb, i + off, 0)),
                  pl.BlockSpec((1, HW, s), lambda bb, i: (bb, 0, 0)),
                  pl.BlockSpec((1, s, HW), lambda bb, i: (bb, 0, 0)),
                  pl.BlockSpec((1, 1), lambda bb, i: (0, 0)),
                  pl.BlockSpec((1, HW), lambda bb, i: (0, 0))],
        out_specs=pl.BlockSpec((1, tq, HW), lambda bb, i: (bb, i, 0)),
        compiler_params=_cparams(("arbitrary", "arbitrary")),
        name="diff_attn",
    )(qa, kat, va, lam, gsub_t)


def _na_ctx_attend(q_all, kt_ref, v_ref, lc):
    outs = []
    for h in range(N_HEADS):
        hs = slice(h * N_HEAD_DIM, (h + 1) * N_HEAD_DIM)
        sc = _dot(q_all[:, hs], kt_ref[0, hs, 0:lc])
        e = jnp.exp2(sc - jnp.max(sc, axis=-1, keepdims=True))
        o = _dot(e.astype(BF16), v_ref[0, 0:lc, hs])
        outs.append(o * (1.0 / jnp.sum(e, axis=-1, keepdims=True)))
    return jnp.concatenate(outs, axis=-1).astype(BF16)


def _na_attend(q_all, kt_ref, v_ref, bias_ref, koff, lc):
    nk = NA_KROWS * GRID_W
    outs = []
    for h in range(N_HEADS):
        hs = slice(h * N_HEAD_DIM, (h + 1) * N_HEAD_DIM)
        q = q_all[:, hs]
        s_loc = _dot(q, kt_ref[0, hs, pl.ds(koff, nk)]) + bias_ref[0, h]
        s_ctx = _dot(q, kt_ref[0, hs, 0:lc])
        m = jnp.maximum(jnp.max(s_loc, axis=-1, keepdims=True), jnp.max(s_ctx, axis=-1, keepdims=True))
        e_loc = jnp.exp2(s_loc - m)
        e_ctx = jnp.exp2(s_ctx - m)
        den = jnp.sum(e_loc, axis=-1, keepdims=True) + jnp.sum(e_ctx, axis=-1, keepdims=True)
        o = _dot(e_loc.astype(BF16), v_ref[0, pl.ds(koff, nk), hs]) + _dot(e_ctx.astype(BF16), v_ref[0, 0:lc, hs])
        outs.append(o * (1.0 / den))
    return jnp.concatenate(outs, axis=-1).astype(BF16)


def _na_kernel(q_ref, kt_ref, v_ref, bias_ref, o_ref, *, lc, n_img_rows, ctx_first):
    i = pl.program_id(1)
    blk = i - 1 if ctx_first else i
    start_row = jnp.clip(NA_ROWS * blk - WIN_ROWS // 2, 0, n_img_rows - NA_KROWS)
    koff = pl.multiple_of(lc + start_row * GRID_W, 128)
    if ctx_first:
        @pl.when(i == 0)
        def _():
            o_ref[0] = _na_ctx_attend(q_ref[0], kt_ref, v_ref, lc)

        @pl.when(i > 0)
        def _():
            o_ref[0] = _na_attend(q_ref[0], kt_ref, v_ref, bias_ref, koff, lc)
    else:
        o_ref[0] = _na_attend(q_ref[0], kt_ref, v_ref, bias_ref, koff, lc)


def _na_attention(qn, knt, vn, bias, lc, with_ctx):
    b, s, _ = qn.shape
    tq = NA_ROWS * GRID_W
    assert tq == ROW_TILE and lc % tq == 0
    n_img_rows = (s - lc) // GRID_W
    nblk = n_img_rows // NA_ROWS
    off = 0 if with_ctx else lc // tq
    rows_out = s - off * tq
    first = 1 if with_ctx else 0

    def variant(bb, i):
        blk = i - first
        return (jnp.where(blk <= 0, 0, jnp.where(blk == nblk - 1, 2, 1)), 0, 0, 0)

    return pl.pallas_call(
        functools.partial(_na_kernel, lc=lc, n_img_rows=n_img_rows, ctx_first=with_ctx),
        out_shape=jax.ShapeDtypeStruct((b, rows_out, HW), BF16),
        grid=(b, rows_out // tq),
        in_specs=[pl.BlockSpec((1, tq, HW), lambda bb, i: (bb, i + off, 0)),
                  pl.BlockSpec((1, HW, s), lambda bb, i: (bb, 0, 0)),
                  pl.BlockSpec((1, s, HW), lambda bb, i: (bb, 0, 0)),
                  pl.BlockSpec((1, N_HEADS, tq, NA_KROWS * GRID_W), variant)],
        out_specs=pl.BlockSpec((1, tq, HW), lambda bb, i: (bb, i, 0)),
        compiler_params=_cparams(("arbitrary", "arbitrary")),
        name="nbr_attn",
    )(qn, knt, vn, bias)


def _na_bias(rpb, n_img_rows):
    a = jnp.arange(NA_ROWS)[:, None, None, None]
    cq = jnp.arange(GRID_W)[None, :, None, None]
    j = jnp.arange(NA_KROWS)[None, None, :, None]
    ck = jnp.arange(GRID_W)[None, None, None, :]
    cstart = jnp.clip(cq - WIN_COLS // 2, 0, GRID_W - WIN_COLS)
    colmask = (ck >= cstart) & (ck < cstart + WIN_COLS)
    dc = jnp.clip(ck - cq, -(WIN_COLS - 1), WIN_COLS - 1) + (WIN_COLS - 1)
    by_col = jnp.take(rpb.astype(F32), dc.reshape(-1), axis=2).reshape(N_HEADS, 2 * WIN_ROWS - 1, GRID_W, GRID_W)
    out = []
    for r0_minus_k, wstart in ((0, 0 * a), (WIN_ROWS // 2, a), (NA_KROWS - NA_ROWS, NA_KROWS - WIN_ROWS + 0 * a)):
        inwin = (j >= wstart) & (j < wstart + WIN_ROWS)
        dr = jnp.clip(j - r0_minus_k - a + (WIN_ROWS - 1), 0, 2 * WIN_ROWS - 2)
        vals = jnp.take(by_col, dr.reshape(-1), axis=1).reshape(N_HEADS, NA_ROWS, NA_KROWS, GRID_W, GRID_W)
        vals = jnp.transpose(vals, (0, 1, 3, 2, 4))
        vals = jnp.where((inwin & colmask)[None], vals * LOG2E, NEG)
        out.append(vals.reshape(N_HEADS, NA_ROWS * GRID_W, NA_KROWS * GRID_W))
    return jnp.stack(out)


def _s5_kernel(u_ref, bm_ref, cm_ref, are_ref, aim_ref, y_ref, x_scr, st_scr, *, reverse):
    tc, nb, w = u_ref.shape
    ns = are_ref.shape[1]

    @pl.when(pl.program_id(0) == 0)
    def _():
        st_scr[...] = jnp.zeros_like(st_scr)

    u = u_ref[...].reshape(tc * nb, w).astype(BF16)
    x_scr[...] = _dot(u, bm_ref[...])
    a_re = are_ref[...]
    a_im = aim_ref[...]

    def step(t, carry):
        s_re, s_im = carry
        tt = (tc - 1 - t) if reverse else t
        rows = pl.ds(pl.multiple_of(tt * nb, nb), nb)
        n_re = a_re * s_re - a_im * s_im + x_scr[rows, 0:ns]
        n_im = a_re * s_im + a_im * s_re + x_scr[rows, ns:2 * ns]
        x_scr[rows, 0:ns] = n_re
        x_scr[rows, ns:2 * ns] = n_im
        return n_re, n_im

    s_re, s_im = lax.fori_loop(0, tc, step, (st_scr[:, 0:ns], st_scr[:, ns:2 * ns]), unroll=4)
    st_scr[:, 0:ns] = s_re
    st_scr[:, ns:2 * ns] = s_im
    y = _dot(x_scr[...].astype(BF16), cm_ref[...])
    y_ref[...] = y.reshape(tc, nb, w)


def _s5_scan(u_tm, bmat, cmat, a_re, a_im, lc, reverse):
    s, nb, w = u_tm.shape
    tc = S5_CHUNK
    nc, ncc = s // tc, lc // tc
    ns = a_re.shape[1]
    if reverse:
        chunk = lambda j: (jnp.where(j < ncc, ncc - 1 - j, nc - 1 - (j - ncc)), 0, 0)
    else:
        chunk = lambda j: (j, 0, 0)
    const = lambda j: (0, 0)
    return pl.pallas_call(
        functools.partial(_s5_kernel, reverse=reverse),
        out_shape=jax.ShapeDtypeStruct((s, nb, w), F32),
        grid=(nc,),
        in_specs=[pl.BlockSpec((tc, nb, w), chunk),
                  pl.BlockSpec((w, 2 * ns), const),
                  pl.BlockSpec((2 * ns, w), const),
                  pl.BlockSpec((nb, ns), const),
                  pl.BlockSpec((nb, ns), const)],
        out_specs=pl.BlockSpec((tc, nb, w), chunk),
        scratch_shapes=[pltpu.VMEM((tc * nb, 2 * ns), F32), pltpu.VMEM((nb, 2 * ns), F32)],
        compiler_params=_cparams(("arbitrary",)),
        name="s5_scan_bwd" if reverse else "s5_scan_fwd",
    )(u_tm, bmat, cmat, a_re, a_im)


def _s5_params(a_re, a_im, log_dt, b_re, b_im, c_re, c_im, nb):
    g, n, p = S5_GROUPS, S5_STATE, S5_GROUP_CH
    lr, li = a_re.astype(F32), a_im.astype(F32)
    dt = jnp.exp(log_dt.astype(F32))[..., None]
    mag = jnp.exp(lr * dt)
    ab_r, ab_i = mag * jnp.cos(li * dt), mag * jnp.sin(li * dt)
    den = lr * lr + li * li
    cf_r = ((ab_r - 1.0) * lr + ab_i * li) / den
    cf_i = (ab_i * lr - (ab_r - 1.0) * li) / den
    br, bi = b_re.astype(F32), b_im.astype(F32)
    bb_r = cf_r[..., None] * br - cf_i[..., None] * bi
    bb_i = cf_r[..., None] * bi + cf_i[..., None] * br
    eye = jnp.eye(g, dtype=F32)
    out = []
    for k in range(2):
        b_r = jnp.einsum('gnp,gh->gphn', bb_r[k], eye).reshape(g * p, g * n)
        b_i = jnp.einsum('gnp,gh->gphn', bb_i[k], eye).reshape(g * p, g * n)
        bmat = jnp.concatenate([b_r, b_i], axis=1).astype(BF16)
        ct = jnp.transpose(c_re[k].astype(F32), (0, 2, 1))
        ci = jnp.transpose(c_im[k].astype(F32), (0, 2, 1))
        c_r = jnp.einsum('gnp,gh->gnhp', ct, eye).reshape(g * n, g * p)
        c_i = jnp.einsum('gnp,gh->gnhp', ci, eye).reshape(g * n, g * p)
        cmat = jnp.concatenate([c_r, -c_i], axis=0).astype(BF16)
        ar = jnp.broadcast_to(ab_r[k].reshape(1, g * n), (nb, g * n))
        ai = jnp.broadcast_to(ab_i[k].reshape(1, g * n), (nb, g * n))
        out.append((bmat, cmat, ar, ai))
    return out


def _out_kernel(oa_ref, on_ref, y_ref, u_ref, x_ref, mod_ref, gffn_ref, wo_ref, dskip_ref, wglu_ref, bglu_ref,
                wr_ref, xo_ref, h2_ref, lg_ref):
    mod = mod_ref[0, 0]
    g = _gelu_tanh(y_ref[0] + dskip_ref[...] * u_ref[0])
    ob = g * _sigmoid(_dot(g.astype(BF16), wglu_ref[...]) + bglu_ref[...])
    mix = (_dot(oa_ref[0], wo_ref[0:HW, :]) + _dot(ob.astype(BF16), wo_ref[HW:HW + B_WIDTH, :])
           + _dot(on_ref[0], wo_ref[HW + B_WIDTH:, :]))
    x = x_ref[0] + mod[2:3] * mix
    xo_ref[0] = x
    ms = jnp.mean(x * x, axis=-1, keepdims=True)
    h2 = x * lax.rsqrt(ms + EPS) * gffn_ref[...] * (1.0 + mod[4:5]) + mod[3:4]
    h2_ref[0] = h2.astype(BF16)
    nt = lambda a, bb: lax.dot_general(a, bb, (((1,), (1,)), ((), ())), preferred_element_type=F32)
    wh, wl = _split_bf16(wr_ref[...])
    hh, hl = _split_bf16(h2)
    lg_ref[0] = nt(wh, hh) + nt(wh, hl) + nt(wl, hh)


def _out_projection(oa, on, y, u, xs, modall, g_ffn, w_out_bf, d_skip, w_glu_bf, b_glu, w_router_pad, lc, with_ctx):
    b, s, d = xs.shape
    tm = ROW_TILE
    off = 0 if with_ctx else lc // tm
    rows_out = s - off * tm
    full = lambda bb, i: (bb, i + off, 0)
    outr = lambda bb, i: (bb, i, 0)
    const = lambda bb, i: (0, 0)
    ne = w_router_pad.shape[0]
    return pl.pallas_call(
        _out_kernel,
        out_shape=(jax.ShapeDtypeStruct((b, rows_out, d), F32),
                   jax.ShapeDtypeStruct((b, rows_out, d), BF16),
                   jax.ShapeDtypeStruct((b, ne, rows_out), F32)),
        grid=(b, rows_out // tm),
        in_specs=[pl.BlockSpec((1, tm, HW), outr),
                  pl.BlockSpec((1, tm, HW), outr),
                  pl.BlockSpec((1, tm, B_WIDTH), full),
                  pl.BlockSpec((1, tm, B_WIDTH), full),
                  pl.BlockSpec((1, tm, d), full),
                  pl.BlockSpec((1, 1, 6, d), lambda bb, i: (bb, jnp.minimum(i + off, 1), 0, 0)),
                  pl.BlockSpec((1, d), const),
                  pl.BlockSpec((d, d), const),
                  pl.BlockSpec((1, B_WIDTH), const),
                  pl.BlockSpec((B_WIDTH, B_WIDTH), const),
                  pl.BlockSpec((1, B_WIDTH), const),
                  pl.BlockSpec((ne, d), const)],
        out_specs=(pl.BlockSpec((1, tm, d), outr), pl.BlockSpec((1, tm, d), outr),
                   pl.BlockSpec((1, ne, tm), lambda bb, i: (bb, 0, i))),
        compiler_params=_cparams(("arbitrary", "arbitrary")),
        name="out_proj",
    )(oa, on, y, u, xs, modall, g_ffn.reshape(1, d), w_out_bf, d_skip.reshape(1, -1), w_glu_bf,
      b_glu.reshape(1, -1), w_router_pad)


def _swiglu(h, wg, wu):
    a = _dot(h, wg)
    return a * _sigmoid(a) * _dot(h, wu)


def _moe_kernel(h_ref, gates_ref, x_ref, gt_ref, wg_ref, wu_ref, wd_ref, sg_ref, su_ref, sd_ref, o_ref, acc_ref):
    e = pl.program_id(1)
    n_routed = pl.num_programs(1) - 1

    @pl.when(e == 0)
    def _():
        acc_ref[...] = jnp.zeros_like(acc_ref)

    @pl.when(e < n_routed)
    def _():
        h = h_ref[...]
        src = lax.broadcasted_iota(jnp.int32, (128, 128), 0)
        dst = lax.broadcasted_iota(jnp.int32, (128, 128), 1)
        sel = jnp.where(src == e * MOE_EB + dst, jnp.where(dst < MOE_EB, 1.0, 0.0), 0.0).astype(BF16)
        g = gates_ref[...]
        p1 = g.astype(BF16)
        r1 = g - p1.astype(F32)
        p2 = r1.astype(BF16)
        p3 = (r1 - p2.astype(F32)).astype(BF16)
        gsel = _dot(p1, sel) + _dot(p2, sel) + _dot(p3, sel)
        hid = [(_swiglu(h, wg_ref[j], wu_ref[j]) * gsel[:, j:j + 1]).astype(BF16) for j in range(MOE_EB)]
        acc_ref[...] += _dot(jnp.concatenate(hid, axis=-1), wd_ref[...])

    @pl.when(e == n_routed)
    def _():
        for j in range(h_ref.shape[0] // ROW_TILE):
            rows = slice(j * ROW_TILE, (j + 1) * ROW_TILE)
            hs = _swiglu(h_ref[rows, :], sg_ref[...], su_ref[...]).astype(BF16)
            y = acc_ref[rows, :] + _dot(hs, sd_ref[...])
            o_ref[rows, :] = x_ref[rows, :] + gt_ref[0, j:j + 1, :] * y


def _moe_tile(t):
    return max(m for m in range(ROW_TILE, MOE_TILE + 1, ROW_TILE) if t % m == 0)


def _moe(h2, gates, x, gt_rows, wg, wu, wd, sg, su, sd):
    t, d = h2.shape
    tm = _moe_tile(t)
    n_routed = wg.shape[0] // MOE_EB
    nsub = tm // ROW_TILE
    step = lambda e: jnp.minimum(e, n_routed - 1)
    const = lambda i, e: (0, 0)
    return pl.pallas_call(
        _moe_kernel,
        out_shape=jax.ShapeDtypeStruct((t, d), F32),
        grid=(t // tm, n_routed + 1),
        in_specs=[pl.BlockSpec((tm, d), lambda i, e: (i, 0)),
                  pl.BlockSpec((tm, 128), lambda i, e: (i, 0)),
                  pl.BlockSpec((tm, d), lambda i, e: (i, 0)),
                  pl.BlockSpec((1, nsub, d), lambda i, e: (i, 0, 0)),
                  pl.BlockSpec((MOE_EB, d, EXPERT_DIM), lambda i, e: (step(e), 0, 0)),
                  pl.BlockSpec((MOE_EB, d, EXPERT_DIM), lambda i, e: (step(e), 0, 0)),
                  pl.BlockSpec((MOE_EB * EXPERT_DIM, d), lambda i, e: (step(e), 0)),
                  pl.BlockSpec((d, EXPERT_DIM), const),
                  pl.BlockSpec((d, EXPERT_DIM), const),
                  pl.BlockSpec((EXPERT_DIM, d), const)],
        out_specs=pl.BlockSpec((tm, d), lambda i, e: (i, 0)),
        scratch_shapes=[pltpu.VMEM((tm, d), F32)],
        compiler_params=_cparams(("arbitrary", "arbitrary")),
        name="moe_ffn",
    )(h2, gates, x, gt_rows, wg, wu, wd, sg, su, sd)


def _route_kernel(lg_ref, bias_ref, o_ref):
    gsz = N_EXPERTS // N_GROUPS
    tn = lg_ref.shape[2]
    ninf = -jnp.inf
    jidx = lax.broadcasted_iota(jnp.int32, (gsz, tn), 0)
    scores, biased, gscore = [], [], []
    for g in range(N_GROUPS):
        rows = slice(g * gsz, (g + 1) * gsz)
        sc = _sigmoid(lg_ref[0, rows, :])
        bi = sc + bias_ref[rows, :]
        m1 = jnp.max(bi, axis=0, keepdims=True)
        first = jnp.min(jnp.where(bi == m1, jidx, gsz), axis=0, keepdims=True)
        m2 = jnp.max(jnp.where(jidx == first, ninf, bi), axis=0, keepdims=True)
        scores.append(sc)
        biased.append(bi)
        gscore.append(m1 + m2)
    masked = []
    for g in range(N_GROUPS):
        rank = jnp.zeros((1, tn), F32)
        for g2 in range(N_GROUPS):
            if g2 != g:
                ahead = (gscore[g2] >= gscore[g]) if g2 < g else (gscore[g2] > gscore[g])
                rank = rank + jnp.where(ahead, 1.0, 0.0)
        keep = jnp.broadcast_to(rank, (gsz, tn)) < TOPK_GROUPS
        masked.append(jnp.where(keep, biased[g], ninf))
    ranks = [jnp.zeros((gsz, tn), F32) for _ in range(N_GROUPS)]
    for g2 in range(N_GROUPS):
        for j2 in range(gsz):
            other = jnp.broadcast_to(masked[g2][j2:j2 + 1, :], (gsz, tn))
            for g in range(N_GROUPS):
                ge = jnp.where(other >= masked[g], 1.0, 0.0)
                gt = jnp.where(other > masked[g], 1.0, 0.0)
                if g2 < g:
                    ahead = ge
                elif g2 > g:
                    ahead = gt
                else:
                    ahead = jnp.where(jidx > j2, ge, gt)
                ranks[g] = ranks[g] + ahead
    picked = [jnp.where(ranks[g] < TOP_K, scores[g], 0.0) for g in range(N_GROUPS)]
    den = sum(jnp.sum(p, axis=0, keepdims=True) for p in picked)
    scale = ROUTED_SCALE / den
    shared_row = jnp.where(lax.broadcasted_iota(jnp.int32, (128 - N_EXPERTS, tn), 0) == 0, 1.0, 0.0)
    gates_t = jnp.concatenate([p * scale for p in picked] + [shared_row], axis=0)
    o_ref[...] = gates_t.T


def _route(logits_t, e_bias):
    b, ne, rows = logits_t.shape
    tn = ROW_TILE
    nt = rows // tn
    bias = jnp.broadcast_to(e_bias.astype(F32)[:, None], (N_EXPERTS, tn))
    return pl.pallas_call(
        _route_kernel,
        out_shape=jax.ShapeDtypeStruct((b * rows, ne), F32),
        grid=(b, nt),
        in_specs=[pl.BlockSpec((1, ne, tn), lambda bb, i: (bb, 0, i)),
                  pl.BlockSpec((N_EXPERTS, tn), lambda bb, i: (0, 0))],
        out_specs=pl.BlockSpec((tn, ne), lambda bb, i: (bb * nt + i, 0)),
        compiler_params=_cparams(("arbitrary", "arbitrary")),
        name="moe_route",
    )(logits_t, bias)


def kernel(x, c, ctx, c_ctx, w_mod, b_mod, g_mix, g_ffn, w_in, w_out, a_gq, a_gk, a_lambda, a_gsub, n_gq, n_gk, n_rpb, s5_a_re, s5_a_im, s5_log_dt, s5_b_re, s5_b_im, s5_c_re, s5_c_im, s5_d, s5_w_glu, s5_b_glu, w_router, e_bias, w_gate, w_up, w_down, ws_gate, ws_up, ws_down):
    b, l, d = x.shape
    lc = ctx.shape[1]
    s = lc + l
    depth = w_mod.shape[0]
    n_img_rows = l // GRID_W

    xs = jnp.concatenate([ctx, x], axis=1).astype(F32)
    cond_rows = jnp.zeros((16, d), F32).at[:b].set(c.astype(F32)).at[b].set(c_ctx.astype(F32))
    cos, sa, sb = _rope_tables(s, lc)
    tables = (cos, sa, sb, _block_ones(A_QK_DIM), _block_ones(N_HEAD_DIM))

    for layer in range(depth):
        last = layer == depth - 1
        with_ctx = not last
        lam_init = 0.8 - 0.6 * math.exp(-0.3 * layer)

        mod = _modulation(cond_rows, w_mod[layer].astype(F32), b_mod[layer].astype(F32))
        mod_lat = mod[:b].reshape(b, 1, 6, d)
        mod_ctx = jnp.broadcast_to(mod[b].reshape(1, 1, 6, d), (b, 1, 6, d))
        modall = jnp.concatenate([mod_ctx, mod_lat], axis=1)

        gains = ((jnp.tile(a_gq[layer].astype(F32), HW // A_QK_DIM) * (A_QK_DIM ** -0.5 * LOG2E)).reshape(1, HW),
                 jnp.tile(a_gk[layer].astype(F32), HW // A_QK_DIM).reshape(1, HW),
                 (jnp.tile(n_gq[layer].astype(F32), N_HEADS) * (N_HEAD_DIM ** -0.5 * LOG2E)).reshape(1, HW),
                 jnp.tile(n_gk[layer].astype(F32), N_HEADS).reshape(1, HW))
        qa, kat, va, qn, knt, vn, u = _in_projection(xs, modall, g_mix[layer].astype(F32), w_in[layer].astype(BF16),
                                                     tables, gains, lc)

        lv = a_lambda[layer].astype(F32)
        lam = (jnp.exp(jnp.sum(lv[0] * lv[1])) - jnp.exp(jnp.sum(lv[2] * lv[3])) + lam_init).reshape(1, 1)
        gsub_t = (jnp.tile(a_gsub[layer].astype(F32), A_HEADS) * (1.0 - lam_init)).reshape(1, HW)
        oa = _diff_attention(qa, kat, va, lam, gsub_t, lc, with_ctx)

        on = _na_attention(qn, knt, vn, _na_bias(n_rpb[layer], n_img_rows), lc, with_ctx)

        u_tm = jnp.transpose(u, (1, 0, 2))
        (bm_f, cm_f, ar_f, ai_f), (bm_b, cm_b, ar_b, ai_b) = _s5_params(
            s5_a_re[layer], s5_a_im[layer], s5_log_dt[layer], s5_b_re[layer], s5_b_im[layer],
            s5_c_re[layer], s5_c_im[layer], b)
        y_tm = _s5_scan(u_tm, bm_f, cm_f, ar_f, ai_f, lc, False) + _s5_scan(u_tm, bm_b, cm_b, ar_b, ai_b, lc, True)
        y = jnp.transpose(y_tm, (1, 0, 2))

        w_router_pad = jnp.zeros((128, d), F32).at[:N_EXPERTS].set(w_router[layer].astype(F32).T)
        x_new, h2, logits_t = _out_projection(oa, on, y, u, xs, modall, g_ffn[layer].astype(F32),
                                            w_out[layer].astype(BF16), s5_d[layer].astype(F32),
                                            s5_w_glu[layer].astype(BF16), s5_b_glu[layer].astype(F32),
                                            w_router_pad, lc, with_ctx)
        rows = x_new.shape[1]
        t = b * rows
        gates = _route(logits_t, e_bias[layer])

        gt2 = modall[:, :, 5, :]
        tiles_per_batch = rows // ROW_TILE
        if with_ctx:
            nctx = lc // ROW_TILE
            sel = (jnp.arange(tiles_per_batch) >= nctx).astype(jnp.int32)
        else:
            sel = jnp.ones((tiles_per_batch,), jnp.int32)
        gt_rows = gt2[:, sel, :].reshape(t // _moe_tile(t), _moe_tile(t) // ROW_TILE, d)

        out = _moe(h2.reshape(t, d), gates, x_new.reshape(t, d), gt_rows,
                   w_gate[layer].astype(BF16), w_up[layer].astype(BF16), w_down[layer].astype(BF16).reshape(-1, d),
                   ws_gate[layer].astype(BF16), ws_up[layer].astype(BF16), ws_down[layer].astype(BF16))
        xs = out.reshape(b, rows, d)

    return xs.astype(x.dtype)
```

```python
import functools
import math

import jax
import jax.numpy as jnp
from jax import lax
from jax.experimental import pallas as pl
from jax.experimental.pallas import tpu as pltpu

F32 = jnp.float32
BF16 = jnp.bfloat16

D_MODEL = 1024
GRID_W = 64
EPS = 1e-6
A_HEADS = 6
A_QK_DIM = 32
A_V_DIM = 64
ROPE_THETA = 10000.0
S5_GROUPS = 16
S5_GROUP_CH = 16
S5_STATE = 64
N_HEADS = 6
N_HEAD_DIM = 64
WIN_ROWS = 8
WIN_COLS = 16
HW = 384
B_WIDTH = 256
Q_COLS = 768
IN_COLS = 2560
N_EXPERTS = 64
TOP_K = 8
N_GROUPS = 8
TOPK_GROUPS = 4
EXPERT_DIM = 256
ROUTED_SCALE = 2.5

ROW_TILE = 256
NA_ROWS = 4
NA_KROWS = 12
S5_CHUNK = 128
MOE_TILE = 1024
MOE_EB = 4
NEG = -1e30
LOG2E = math.log2(math.e)
SHIFT_BOUND_LIMIT = 50.0
VMEM_LIMIT = 56 * 1024 * 1024


def _sigmoid(x):
    return 1.0 / (1.0 + jnp.exp(-x))


def _gelu_tanh(x):
    return 0.5 * x * (1.0 + jnp.tanh(math.sqrt(2.0 / math.pi) * (x + 0.044715 * (x * x * x))))


def _split_bf16(a):
    hi = a.astype(BF16)
    lo = (a - hi.astype(F32)).astype(BF16)
    return hi, lo


def _dot(a, b):
    return jnp.dot(a, b, preferred_element_type=F32)


def _dot3(a, b):
    ah, al = _split_bf16(a)
    bh, bl = _split_bf16(b)
    return _dot(ah, bh) + _dot(ah, bl) + _dot(al, bh)


def _cparams(sem):
    return pltpu.CompilerParams(dimension_semantics=sem, vmem_limit_bytes=VMEM_LIMIT)


def _mod_kernel(c_ref, w_ref, b_ref, o_ref):
    c = c_ref[...]
    cond = c * _sigmoid(c)
    o_ref[...] = _dot3(cond, w_ref[...]) + b_ref[...]


def _modulation(cond_rows, w_mod, b_mod):
    r, d = cond_rows.shape
    n = w_mod.shape[1]
    tn = 1536
    return pl.pallas_call(
        _mod_kernel,
        out_shape=jax.ShapeDtypeStruct((r, n), F32),
        grid=(n // tn,),
        in_specs=[pl.BlockSpec((r, d), lambda j: (0, 0)),
                  pl.BlockSpec((d, tn), lambda j: (0, j)),
                  pl.BlockSpec((1, tn), lambda j: (0, j))],
        out_specs=pl.BlockSpec((r, tn), lambda j: (0, j)),
        compiler_params=_cparams(("arbitrary",)),
        name="adaln_mod",
    )(cond_rows, w_mod, b_mod.reshape(1, n))


def _group_rms(t, ones_ref, gain_ref, group):
    ms = _dot((t * t).astype(BF16), ones_ref[...]) * (1.0 / group)
    return t * lax.rsqrt(ms + EPS) * gain_ref[...]


def _rope(t, cos_ref, sa_ref, sb_ref):
    up = pltpu.roll(t, HW - 8, 1)
    dn = pltpu.roll(t, 8, 1)
    return t * cos_ref[...] + up * sa_ref[...] + dn * sb_ref[...]


def _proj_kernel(x_ref, mod_ref, g_ref, w_ref, cos_ref, sa_ref, sb_ref, ones32_ref, ones64_ref,
                 gqa_ref, gka_ref, gqn_ref, gkn_ref,
                 qa_ref, kat_ref, va_ref, qn_ref, knt_ref, vn_ref, u_ref):
    x = x_ref[0]
    mod = mod_ref[0, 0]
    ms = jnp.mean(x * x, axis=-1, keepdims=True)
    h = x * lax.rsqrt(ms + EPS) * g_ref[...] * (1.0 + mod[1:2]) + mod[0:1]
    hb = h.astype(BF16)

    def sec(a, b):
        return _dot(hb, w_ref[:, a:b])

    qa = _rope(_group_rms(sec(0, 384), ones32_ref, gqa_ref, A_QK_DIM), cos_ref, sa_ref, sb_ref)
    qa_ref[0] = qa.astype(BF16)
    qn_ref[0] = _group_rms(sec(384, 768), ones64_ref, gqn_ref, N_HEAD_DIM).astype(BF16)
    ka = _rope(_group_rms(sec(768, 1152), ones32_ref, gka_ref, A_QK_DIM), cos_ref, sa_ref, sb_ref)
    kat_ref[0] = ka.T.astype(BF16)
    va_ref[0] = sec(1152, 1536).astype(BF16)
    kn = _group_rms(sec(1536, 1920), ones64_ref, gkn_ref, N_HEAD_DIM)
    knt_ref[0] = kn.T.astype(BF16)
    vn_ref[0] = sec(1920, 2304).astype(BF16)
    u_ref[0] = sec(2304, 2560)


def _in_projection(xs, modall, g_mix, w_in_bf, tables, gains, lc):
    b, s, d = xs.shape
    tm = ROW_TILE
    cos, sa, sb, ones32, ones64 = tables
    row = lambda i, bb: (bb, i, 0)
    tab = lambda i, bb: (i, 0)
    const2 = lambda i, bb: (0, 0)
    act = lambda w, dt: jax.ShapeDtypeStruct((b, s, w), dt)
    act_t = jax.ShapeDtypeStruct((b, HW, s), BF16)
    return pl.pallas_call(
        _proj_kernel,
        out_shape=(act(HW, BF16), act_t, act(HW, BF16), act(HW, BF16), act_t, act(HW, BF16), act(B_WIDTH, F32)),
        grid=(s // tm, b),
        in_specs=[pl.BlockSpec((1, tm, d), row),
                  pl.BlockSpec((1, 1, 6, d), lambda i, bb: (bb, jnp.minimum(i, 1), 0, 0)),
                  pl.BlockSpec((1, d), const2),
                  pl.BlockSpec((d, IN_COLS), const2),
                  pl.BlockSpec((tm, HW), tab), pl.BlockSpec((tm, HW), tab), pl.BlockSpec((tm, HW), tab),
                  pl.BlockSpec((HW, HW), const2), pl.BlockSpec((HW, HW), const2),
                  pl.BlockSpec((1, HW), const2), pl.BlockSpec((1, HW), const2),
                  pl.BlockSpec((1, HW), const2), pl.BlockSpec((1, HW), const2)],
        out_specs=(pl.BlockSpec((1, tm, HW), row),
                   pl.BlockSpec((1, HW, tm), lambda i, bb: (bb, 0, i)),
                   pl.BlockSpec((1, tm, HW), row),
                   pl.BlockSpec((1, tm, HW), row),
                   pl.BlockSpec((1, HW, tm), lambda i, bb: (bb, 0, i)),
                   pl.BlockSpec((1, tm, HW), row),
                   pl.BlockSpec((1, tm, B_WIDTH), row)),
        compiler_params=_cparams(("arbitrary", "arbitrary")),
        name="in_proj",
    )(xs, modall, g_mix.reshape(1, d), w_in_bf, cos, sa, sb, ones32, ones64, *gains)


def _rope_tables(s, lc):
    p = jnp.arange(s)
    pos = jnp.maximum(p - lc, 0)
    rows = (pos // GRID_W).astype(F32)
    cols = (pos % GRID_W).astype(F32)
    lane = jnp.arange(HW)
    j32 = lane % A_QK_DIM
    half = j32 // 16
    i16 = j32 % 16
    nf = 8
    inv = ROPE_THETA ** (-(i16 % nf).astype(F32) / nf)
    coord = jnp.where(half[None, :] == 0, rows[:, None], cols[:, None])
    ang = coord * inv[None, :]
    is_lat = (p >= lc)[:, None]
    second = (i16 >= nf)[None, :]
    cos = jnp.where(is_lat, jnp.cos(ang), 1.0)
    sin = jnp.where(is_lat, jnp.sin(ang), 0.0)
    sa = jnp.where(second, 0.0, -sin)
    sb = jnp.where(second, sin, 0.0)
    return cos.astype(F32), sa.astype(F32), sb.astype(F32)


def _block_ones(group):
    g = jnp.arange(HW) // group
    return (g[:, None] == g[None, :]).astype(BF16)


def _diff_attend(q_all, kt_ref, v_ref, lam, gsub_ref, sk, shifts):
    outs = []
    for h in range(A_HEADS):
        v = v_ref[0, 0:sk, h * A_V_DIM:(h + 1) * A_V_DIM]
        parts = []
        for sub in range(2):
            hs = 2 * h + sub
            off = hs * A_QK_DIM
            sc = _dot(q_all[:, off:off + A_QK_DIM], kt_ref[0, off:off + A_QK_DIM, 0:sk])
            shift = jnp.max(sc, axis=-1, keepdims=True) if shifts is None else shifts[hs]
            e = jnp.exp2(sc - shift)
            parts.append(_dot(e.astype(BF16), v) * (1.0 / jnp.sum(e, axis=-1, keepdims=True)))
        o = parts[0] - lam * parts[1]
        outs.append(o * lax.rsqrt(jnp.mean(o * o, axis=-1, keepdims=True) + EPS))
    return (jnp.concatenate(outs, axis=-1) * gsub_ref[...]).astype(BF16)


def _key_norm_max(kt_ref, sk, kmax_scr, row0):
    for hs in range(2 * A_HEADS):
        k = kt_ref[0, hs * A_QK_DIM:(hs + 1) * A_QK_DIM, 0:sk].astype(F32)
        sq = jnp.max(jnp.sum(k * k, axis=0, keepdims=True), axis=1, keepdims=True)
        kmax_scr[row0 + hs:row0 + hs + 1, :] = jnp.broadcast_to(jnp.sqrt(sq), (1, 128))


def _diff_attn_tile(q_all, kt_ref, v_ref, lam, gsub_ref, o_ref, kmax_scr, sk, row0):
    shifts = []
    for hs in range(2 * A_HEADS):
        qf = q_all[:, hs * A_QK_DIM:(hs + 1) * A_QK_DIM].astype(F32)
        qn = jnp.sqrt(jnp.sum(qf * qf, axis=-1, keepdims=True))
        shifts.append(qn * kmax_scr[row0 + hs:row0 + hs + 1, 0:1])
    worst = jnp.max(functools.reduce(jnp.maximum, shifts))
    small = worst <= SHIFT_BOUND_LIMIT

    @pl.when(small)
    def _():
        o_ref[0] = _diff_attend(q_all, kt_ref, v_ref, lam, gsub_ref, sk, shifts)

    @pl.when(jnp.logical_not(small))
    def _():
        o_ref[0] = _diff_attend(q_all, kt_ref, v_ref, lam, gsub_ref, sk, None)


def _diff_attn_kernel(q_ref, kt_ref, v_ref, lam_ref, gsub_ref, o_ref, kmax_scr, *, lc, ctx_first):
    lam = lam_ref[...]
    s = kt_ref.shape[2]
    i = pl.program_id(1)
    ctx_row0 = 2 * A_HEADS + 4

    @pl.when(i == 0)
    def _():
        _key_norm_max(kt_ref, s, kmax_scr, 0)
        if ctx_first:
            _key_norm_max(kt_ref, lc, kmax_scr, ctx_row0)

    if ctx_first:
        @pl.when(i == 0)
        def _():
            _diff_attn_tile(q_ref[0], kt_ref, v_ref, lam, gsub_ref, o_ref, kmax_scr, lc, ctx_row0)

        @pl.when(i > 0)
        def _():
            _diff_attn_tile(q_ref[0], kt_ref, v_ref, lam, gsub_ref, o_ref, kmax_scr, s, 0)
    else:
        _diff_attn_tile(q_ref[0], kt_ref, v_ref, lam, gsub_ref, o_ref, kmax_scr, s, 0)


def _diff_attention(qa, kat, va, lam, gsub_t, lc, with_ctx):
    b, s, _ = qa.shape
    tq = ROW_TILE
    off = 0 if with_ctx else lc // tq
    rows_out = s - off * tq
    return pl.pallas_call(
        functools.partial(_diff_attn_kernel, lc=lc, ctx_first=with_ctx),
        out_shape=jax.ShapeDtypeStruct((b, rows_out, HW), BF16),
        grid=(b, rows_out // tq),
        in_specs=[pl.BlockSpec((1, tq, HW), lambda bb, i: (bb, i + off, 0)),
                  pl.BlockSpec((1, HW, s), lambda bb, i: (bb, 0, 0)),
                  pl.BlockSpec((1, s, HW), lambda bb, i: (bb, 0, 0)),
                  pl.BlockSpec((1, 1), lambda bb, i: (0, 0)),
                  pl.BlockSpec((1, HW), lambda bb, i: (0, 0))],
        out_specs=pl.BlockSpec((1, tq, HW), lambda bb, i: (bb, i, 0)),
        scratch_shapes=[pltpu.VMEM((32, 128), F32)],
        compiler_params=_cparams(("arbitrary", "arbitrary")),
        name="diff_attn",
    )(qa, kat, va, lam, gsub_t)


def _na_ctx_attend(q_all, kt_ref, v_ref, lc):
    outs = []
    for h in range(N_HEADS):
        hs = slice(h * N_HEAD_DIM, (h + 1) * N_HEAD_DIM)
        sc = _dot(q_all[:, hs], kt_ref[0, hs, 0:lc])
        e = jnp.exp2(sc - jnp.max(sc, axis=-1, keepdims=True))
        o = _dot(e.astype(BF16), v_ref[0, 0:lc, hs])
        outs.append(o * (1.0 / jnp.sum(e, axis=-1, keepdims=True)))
    return jnp.concatenate(outs, axis=-1).astype(BF16)


def _na_attend(q_all, kt_ref, v_ref, bias_ref, koff, lc):
    nk = NA_KROWS * GRID_W
    outs = []
    for h in range(N_HEADS):
        hs = slice(h * N_HEAD_DIM, (h + 1) * N_HEAD_DIM)
        q = q_all[:, hs]
        s_loc = _dot(q, kt_ref[0, hs, pl.ds(koff, nk)]) + bias_ref[0, h]
        s_ctx = _dot(q, kt_ref[0, hs, 0:lc])
        m = jnp.maximum(jnp.max(s_loc, axis=-1, keepdims=True), jnp.max(s_ctx, axis=-1, keepdims=True))
        e_loc = jnp.exp2(s_loc - m)
        e_ctx = jnp.exp2(s_ctx - m)
        den = jnp.sum(e_loc, axis=-1, keepdims=True) + jnp.sum(e_ctx, axis=-1, keepdims=True)
        o = _dot(e_loc.astype(BF16), v_ref[0, pl.ds(koff, nk), hs]) + _dot(e_ctx.astype(BF16), v_ref[0, 0:lc, hs])
        outs.append(o * (1.0 / den))
    return jnp.concatenate(outs, axis=-1).astype(BF16)


def _na_kernel(q_ref, kt_ref, v_ref, bias_ref, o_ref, *, lc, n_img_rows, ctx_first):
    i = pl.program_id(1)
    blk = i - 1 if ctx_first else i
    start_row = jnp.clip(NA_ROWS * blk - WIN_ROWS // 2, 0, n_img_rows - NA_KROWS)
    koff = pl.multiple_of(lc + start_row * GRID_W, 128)
    if ctx_first:
        @pl.when(i == 0)
        def _():
            o_ref[0] = _na_ctx_attend(q_ref[0], kt_ref, v_ref, lc)

        @pl.when(i > 0)
        def _():
            o_ref[0] = _na_attend(q_ref[0], kt_ref, v_ref, bias_ref, koff, lc)
    else:
        o_ref[0] = _na_attend(q_ref[0], kt_ref, v_ref, bias_ref, koff, lc)


def _na_attention(qn, knt, vn, bias, lc, with_ctx):
    b, s, _ = qn.shape
    tq = NA_ROWS * GRID_W
    assert tq == ROW_TILE and lc % tq == 0
    n_img_rows = (s - lc) // GRID_W
    nblk = n_img_rows // NA_ROWS
    off = 0 if with_ctx else lc // tq
    rows_out = s - off * tq
    first = 1 if with_ctx else 0

    def variant(bb, i):
        blk = i - first
        return (jnp.where(blk <= 0, 0, jnp.where(blk == nblk - 1, 2, 1)), 0, 0, 0)

    return pl.pallas_call(
        functools.partial(_na_kernel, lc=lc, n_img_rows=n_img_rows, ctx_first=with_ctx),
        out_shape=jax.ShapeDtypeStruct((b, rows_out, HW), BF16),
        grid=(b, rows_out // tq),
        in_specs=[pl.BlockSpec((1, tq, HW), lambda bb, i: (bb, i + off, 0)),
                  pl.BlockSpec((1, HW, s), lambda bb, i: (bb, 0, 0)),
                  pl.BlockSpec((1, s, HW), lambda bb, i: (bb, 0, 0)),
                  pl.BlockSpec((1, N_HEADS, tq, NA_KROWS * GRID_W), variant)],
        out_specs=pl.BlockSpec((1, tq, HW), lambda bb, i: (bb, i, 0)),
        compiler_params=_cparams(("arbitrary", "arbitrary")),
        name="nbr_attn",
    )(qn, knt, vn, bias)


def _na_bias(rpb, n_img_rows):
    a = jnp.arange(NA_ROWS)[:, None, None, None]
    cq = jnp.arange(GRID_W)[None, :, None, None]
    j = jnp.arange(NA_KROWS)[None, None, :, None]
    ck = jnp.arange(GRID_W)[None, None, None, :]
    cstart = jnp.clip(cq - WIN_COLS // 2, 0, GRID_W - WIN_COLS)
    colmask = (ck >= cstart) & (ck < cstart + WIN_COLS)
    dc = jnp.clip(ck - cq, -(WIN_COLS - 1), WIN_COLS - 1) + (WIN_COLS - 1)
    by_col = jnp.take(rpb.astype(F32), dc.reshape(-1), axis=2).reshape(N_HEADS, 2 * WIN_ROWS - 1, GRID_W, GRID_W)
    out = []
    for r0_minus_k, wstart in ((0, 0 * a), (WIN_ROWS // 2, a), (NA_KROWS - NA_ROWS, NA_KROWS - WIN_ROWS + 0 * a)):
        inwin = (j >= wstart) & (j < wstart + WIN_ROWS)
        dr = jnp.clip(j - r0_minus_k - a + (WIN_ROWS - 1), 0, 2 * WIN_ROWS - 2)
        vals = jnp.take(by_col, dr.reshape(-1), axis=1).reshape(N_HEADS, NA_ROWS, NA_KROWS, GRID_W, GRID_W)
        vals = jnp.transpose(vals, (0, 1, 3, 2, 4))
        vals = jnp.where((inwin & colmask)[None], vals * LOG2E, NEG)
        out.append(vals.reshape(N_HEADS, NA_ROWS * GRID_W, NA_KROWS * GRID_W))
    return jnp.stack(out)


def _s5_kernel(uf_ref, ub_ref, bmf_ref, cmf_ref, arf_ref, aif_ref, bmb_ref, cmb_ref, arb_ref, aib_ref,
               yf_ref, yb_ref, xf_scr, xb_scr, st_scr):
    tc, nb, w = uf_ref.shape
    ns = arf_ref.shape[1]

    @pl.when(pl.program_id(0) == 0)
    def _():
        st_scr[...] = jnp.zeros_like(st_scr)

    xf_scr[...] = _dot(uf_ref[...].reshape(tc * nb, w).astype(BF16), bmf_ref[...])
    xb_scr[...] = _dot(ub_ref[...].reshape(tc * nb, w).astype(BF16), bmb_ref[...])

    def advance(x_scr, a_re, a_im, s_re, s_im, tt):
        rows = pl.ds(pl.multiple_of(tt * nb, nb), nb)
        n_re = a_re * s_re - a_im * s_im + x_scr[rows, 0:ns]
        n_im = a_re * s_im + a_im * s_re + x_scr[rows, ns:2 * ns]
        x_scr[rows, 0:ns] = n_re
        x_scr[rows, ns:2 * ns] = n_im
        return n_re, n_im

    def step(t, carry):
        f_re, f_im, b_re, b_im = carry
        f_re, f_im = advance(xf_scr, arf_ref[...], aif_ref[...], f_re, f_im, t)
        b_re, b_im = advance(xb_scr, arb_ref[...], aib_ref[...], b_re, b_im, tc - 1 - t)
        return f_re, f_im, b_re, b_im

    init = (st_scr[0, :, 0:ns], st_scr[0, :, ns:2 * ns], st_scr[1, :, 0:ns], st_scr[1, :, ns:2 * ns])
    f_re, f_im, b_re, b_im = lax.fori_loop(0, tc, step, init, unroll=2)
    st_scr[0, :, 0:ns] = f_re
    st_scr[0, :, ns:2 * ns] = f_im
    st_scr[1, :, 0:ns] = b_re
    st_scr[1, :, ns:2 * ns] = b_im
    yf_ref[...] = _dot(xf_scr[...].astype(BF16), cmf_ref[...]).reshape(tc, nb, w)
    yb_ref[...] = _dot(xb_scr[...].astype(BF16), cmb_ref[...]).reshape(tc, nb, w)


def _s5_scan(u_tm, fwd, bwd, lc):
    s, nb, w = u_tm.shape
    tc = S5_CHUNK
    nc, ncc = s // tc, lc // tc
    ns = fwd[2].shape[1]
    chunk_f = lambda j: (j, 0, 0)
    chunk_b = lambda j: (jnp.where(j < ncc, ncc - 1 - j, nc - 1 - (j - ncc)), 0, 0)
    const = lambda j: (0, 0)
    pspecs = [pl.BlockSpec((w, 2 * ns), const), pl.BlockSpec((2 * ns, w), const),
              pl.BlockSpec((nb, ns), const), pl.BlockSpec((nb, ns), const)]
    out = jax.ShapeDtypeStruct((s, nb, w), F32)
    return pl.pallas_call(
        _s5_kernel,
        out_shape=(out, out),
        grid=(nc,),
        in_specs=[pl.BlockSpec((tc, nb, w), chunk_f), pl.BlockSpec((tc, nb, w), chunk_b)] + pspecs + pspecs,
        out_specs=(pl.BlockSpec((tc, nb, w), chunk_f), pl.BlockSpec((tc, nb, w), chunk_b)),
        scratch_shapes=[pltpu.VMEM((tc * nb, 2 * ns), F32), pltpu.VMEM((tc * nb, 2 * ns), F32),
                        pltpu.VMEM((2, nb, 2 * ns), F32)],
        compiler_params=_cparams(("arbitrary",)),
        name="s5_scan",
    )(u_tm, u_tm, *fwd, *bwd)


def _s5_params(a_re, a_im, log_dt, b_re, b_im, c_re, c_im, nb):
    g, n, p = S5_GROUPS, S5_STATE, S5_GROUP_CH
    lr, li = a_re.astype(F32), a_im.astype(F32)
    dt = jnp.exp(log_dt.astype(F32))[..., None]
    mag = jnp.exp(lr * dt)
    ab_r, ab_i = mag * jnp.cos(li * dt), mag * jnp.sin(li * dt)
    den = lr * lr + li * li
    cf_r = ((ab_r - 1.0) * lr + ab_i * li) / den
    cf_i = (ab_i * lr - (ab_r - 1.0) * li) / den
    br, bi = b_re.astype(F32), b_im.astype(F32)
    bb_r = cf_r[..., None] * br - cf_i[..., None] * bi
    bb_i = cf_r[..., None] * bi + cf_i[..., None] * br
    eye = jnp.eye(g, dtype=F32)
    out = []
    for k in range(2):
        b_r = jnp.einsum('gnp,gh->gphn', bb_r[k], eye).reshape(g * p, g * n)
        b_i = jnp.einsum('gnp,gh->gphn', bb_i[k], eye).reshape(g * p, g * n)
        bmat = jnp.concatenate([b_r, b_i], axis=1).astype(BF16)
        ct = jnp.transpose(c_re[k].astype(F32), (0, 2, 1))
        ci = jnp.transpose(c_im[k].astype(F32), (0, 2, 1))
        c_r = jnp.einsum('gnp,gh->gnhp', ct, eye).reshape(g * n, g * p)
        c_i = jnp.einsum('gnp,gh->gnhp', ci, eye).reshape(g * n, g * p)
        cmat = jnp.concatenate([c_r, -c_i], axis=0).astype(BF16)
        ar = jnp.broadcast_to(ab_r[k].reshape(1, g * n), (nb, g * n))
        ai = jnp.broadcast_to(ab_i[k].reshape(1, g * n), (nb, g * n))
        out.append((bmat, cmat, ar, ai))
    return out


def _out_kernel(oa_ref, on_ref, y_ref, u_ref, x_ref, mod_ref, gffn_ref, wo_ref, dskip_ref, wglu_ref, bglu_ref,
                wr_ref, xo_ref, h2_ref, lg_ref):
    mod = mod_ref[0, 0]
    g = _gelu_tanh(y_ref[0] + dskip_ref[...] * u_ref[0])
    ob = g * _sigmoid(_dot(g.astype(BF16), wglu_ref[...]) + bglu_ref[...])
    mix = (_dot(oa_ref[0], wo_ref[0:HW, :]) + _dot(ob.astype(BF16), wo_ref[HW:HW + B_WIDTH, :])
           + _dot(on_ref[0], wo_ref[HW + B_WIDTH:, :]))
    x = x_ref[0] + mod[2:3] * mix
    xo_ref[0] = x
    ms = jnp.mean(x * x, axis=-1, keepdims=True)
    h2 = x * lax.rsqrt(ms + EPS) * gffn_ref[...] * (1.0 + mod[4:5]) + mod[3:4]
    h2_ref[0] = h2.astype(BF16)
    nt = lambda a, bb: lax.dot_general(a, bb, (((1,), (1,)), ((), ())), preferred_element_type=F32)
    wh, wl = _split_bf16(wr_ref[...])
    hh, hl = _split_bf16(h2)
    lg_ref[0] = nt(wh, hh) + nt(wh, hl) + nt(wl, hh)


def _out_projection(oa, on, y, u, xs, modall, g_ffn, w_out_bf, d_skip, w_glu_bf, b_glu, w_router_pad, lc, with_ctx):
    b, s, d = xs.shape
    tm = ROW_TILE
    off = 0 if with_ctx else lc // tm
    rows_out = s - off * tm
    full = lambda bb, i: (bb, i + off, 0)
    outr = lambda bb, i: (bb, i, 0)
    const = lambda bb, i: (0, 0)
    ne = w_router_pad.shape[0]
    return pl.pallas_call(
        _out_kernel,
        out_shape=(jax.ShapeDtypeStruct((b, rows_out, d), F32),
                   jax.ShapeDtypeStruct((b, rows_out, d), BF16),
                   jax.ShapeDtypeStruct((b, ne, rows_out), F32)),
        grid=(b, rows_out // tm),
        in_specs=[pl.BlockSpec((1, tm, HW), outr),
                  pl.BlockSpec((1, tm, HW), outr),
                  pl.BlockSpec((1, tm, B_WIDTH), full),
                  pl.BlockSpec((1, tm, B_WIDTH), full),
                  pl.BlockSpec((1, tm, d), full),
                  pl.BlockSpec((1, 1, 6, d), lambda bb, i: (bb, jnp.minimum(i + off, 1), 0, 0)),
                  pl.BlockSpec((1, d), const),
                  pl.BlockSpec((d, d), const),
                  pl.BlockSpec((1, B_WIDTH), const),
                  pl.BlockSpec((B_WIDTH, B_WIDTH), const),
                  pl.BlockSpec((1, B_WIDTH), const),
                  pl.BlockSpec((ne, d), const)],
        out_specs=(pl.BlockSpec((1, tm, d), outr), pl.BlockSpec((1, tm, d), outr),
                   pl.BlockSpec((1, ne, tm), lambda bb, i: (bb, 0, i))),
        compiler_params=_cparams(("arbitrary", "arbitrary")),
        name="out_proj",
    )(oa, on, y, u, xs, modall, g_ffn.reshape(1, d), w_out_bf, d_skip.reshape(1, -1), w_glu_bf,
      b_glu.reshape(1, -1), w_router_pad)


def _swiglu(h, wg, wu):
    a = _dot(h, wg)
    return a * _sigmoid(a) * _dot(h, wu)


def _moe_kernel(h_ref, gates_ref, x_ref, gt_ref, wg_ref, wu_ref, wd_ref, sg_ref, su_ref, sd_ref, o_ref, acc_ref):
    e = pl.program_id(1)
    n_routed = pl.num_programs(1) - 1

    @pl.when(e == 0)
    def _():
        acc_ref[...] = jnp.zeros_like(acc_ref)

    @pl.when(e < n_routed)
    def _():
        h = h_ref[...]
        src = lax.broadcasted_iota(jnp.int32, (128, 128), 0)
        dst = lax.broadcasted_iota(jnp.int32, (128, 128), 1)
        sel = jnp.where(src == e * MOE_EB + dst, jnp.where(dst < MOE_EB, 1.0, 0.0), 0.0).astype(BF16)
        g = gates_ref[...]
        p1 = g.astype(BF16)
        r1 = g - p1.astype(F32)
        p2 = r1.astype(BF16)
        p3 = (r1 - p2.astype(F32)).astype(BF16)
        gsel = _dot(p1, sel) + _dot(p2, sel) + _dot(p3, sel)
        hid = [(_swiglu(h, wg_ref[j], wu_ref[j]) * gsel[:, j:j + 1]).astype(BF16) for j in range(MOE_EB)]
        acc_ref[...] += _dot(jnp.concatenate(hid, axis=-1), wd_ref[...])

    @pl.when(e == n_routed)
    def _():
        for j in range(h_ref.shape[0] // ROW_TILE):
            rows = slice(j * ROW_TILE, (j + 1) * ROW_TILE)
            hs = _swiglu(h_ref[rows, :], sg_ref[...], su_ref[...]).astype(BF16)
            y = acc_ref[rows, :] + _dot(hs, sd_ref[...])
            o_ref[rows, :] = x_ref[rows, :] + gt_ref[0, j:j + 1, :] * y


def _moe_tile(t):
    return max(m for m in range(ROW_TILE, MOE_TILE + 1, ROW_TILE) if t % m == 0)


def _moe(h2, gates, x, gt_rows, wg, wu, wd, sg, su, sd):
    t, d = h2.shape
    tm = _moe_tile(t)
    n_routed = wg.shape[0] // MOE_EB
    nsub = tm // ROW_TILE
    step = lambda e: jnp.minimum(e, n_routed - 1)
    const = lambda i, e: (0, 0)
    return pl.pallas_call(
        _moe_kernel,
        out_shape=jax.ShapeDtypeStruct((t, d), F32),
        grid=(t // tm, n_routed + 1),
        in_specs=[pl.BlockSpec((tm, d), lambda i, e: (i, 0)),
                  pl.BlockSpec((tm, 128), lambda i, e: (i, 0)),
                  pl.BlockSpec((tm, d), lambda i, e: (i, 0)),
                  pl.BlockSpec((1, nsub, d), lambda i, e: (i, 0, 0)),
                  pl.BlockSpec((MOE_EB, d, EXPERT_DIM), lambda i, e: (step(e), 0, 0)),
                  pl.BlockSpec((MOE_EB, d, EXPERT_DIM), lambda i, e: (step(e), 0, 0)),
                  pl.BlockSpec((MOE_EB * EXPERT_DIM, d), lambda i, e: (step(e), 0)),
                  pl.BlockSpec((d, EXPERT_DIM), const),
                  pl.BlockSpec((d, EXPERT_DIM), const),
                  pl.BlockSpec((EXPERT_DIM, d), const)],
        out_specs=pl.BlockSpec((tm, d), lambda i, e: (i, 0)),
        scratch_shapes=[pltpu.VMEM((tm, d), F32)],
        compiler_params=_cparams(("arbitrary", "arbitrary")),
        name="moe_ffn",
    )(h2, gates, x, gt_rows, wg, wu, wd, sg, su, sd)


def _route_kernel(lg_ref, bias_ref, o_ref):
    gsz = N_EXPERTS // N_GROUPS
    tn = lg_ref.shape[2]
    ninf = -jnp.inf
    jidx = lax.broadcasted_iota(jnp.int32, (gsz, tn), 0)
    scores, biased, gscore = [], [], []
    for g in range(N_GROUPS):
        rows = slice(g * gsz, (g + 1) * gsz)
        sc = _sigmoid(lg_ref[0, rows, :])
        bi = sc + bias_ref[rows, :]
        m1 = jnp.max(bi, axis=0, keepdims=True)
        first = jnp.min(jnp.where(bi == m1, jidx, gsz), axis=0, keepdims=True)
        m2 = jnp.max(jnp.where(jidx == first, ninf, bi), axis=0, keepdims=True)
        scores.append(sc)
        biased.append(bi)
        gscore.append(m1 + m2)
    masked = []
    for g in range(N_GROUPS):
        rank = jnp.zeros((1, tn), F32)
        for g2 in range(N_GROUPS):
            if g2 != g:
                ahead = (gscore[g2] >= gscore[g]) if g2 < g else (gscore[g2] > gscore[g])
                rank = rank + jnp.where(ahead, 1.0, 0.0)
        keep = jnp.broadcast_to(rank, (gsz, tn)) < TOPK_GROUPS
        masked.append(jnp.where(keep, biased[g], ninf))
    ranks = [jnp.zeros((gsz, tn), F32) for _ in range(N_GROUPS)]
    for g2 in range(N_GROUPS):
        for j2 in range(gsz):
            other = jnp.broadcast_to(masked[g2][j2:j2 + 1, :], (gsz, tn))
            for g in range(N_GROUPS):
                ge = jnp.where(other >= masked[g], 1.0, 0.0)
                gt = jnp.where(other > masked[g], 1.0, 0.0)
                if g2 < g:
                    ahead = ge
                elif g2 > g:
                    ahead = gt
                else:
                    ahead = jnp.where(jidx > j2, ge, gt)
                ranks[g] = ranks[g] + ahead
    picked = [jnp.where(ranks[g] < TOP_K, scores[g], 0.0) for g in range(N_GROUPS)]
    den = sum(jnp.sum(p, axis=0, keepdims=True) for p in picked)
    scale = ROUTED_SCALE / den
    shared_row = jnp.where(lax.broadcasted_iota(jnp.int32, (128 - N_EXPERTS, tn), 0) == 0, 1.0, 0.0)
    gates_t = jnp.concatenate([p * scale for p in picked] + [shared_row], axis=0)
    o_ref[...] = gates_t.T


def _route(logits_t, e_bias):
    b, ne, rows = logits_t.shape
    tn = ROW_TILE
    nt = rows // tn
    bias = jnp.broadcast_to(e_bias.astype(F32)[:, None], (N_EXPERTS, tn))
    return pl.pallas_call(
        _route_kernel,
        out_shape=jax.ShapeDtypeStruct((b * rows, ne), F32),
        grid=(b, nt),
        in_specs=[pl.BlockSpec((1, ne, tn), lambda bb, i: (bb, 0, i)),
                  pl.BlockSpec((N_EXPERTS, tn), lambda bb, i: (0, 0))],
        out_specs=pl.BlockSpec((tn, ne), lambda bb, i: (bb * nt + i, 0)),
        compiler_params=_cparams(("arbitrary", "arbitrary")),
        name="moe_route",
    )(logits_t, bias)


def kernel(x, c, ctx, c_ctx, w_mod, b_mod, g_mix, g_ffn, w_in, w_out, a_gq, a_gk, a_lambda, a_gsub, n_gq, n_gk, n_rpb, s5_a_re, s5_a_im, s5_log_dt, s5_b_re, s5_b_im, s5_c_re, s5_c_im, s5_d, s5_w_glu, s5_b_glu, w_router, e_bias, w_gate, w_up, w_down, ws_gate, ws_up, ws_down):
    b, l, d = x.shape
    lc = ctx.shape[1]
    s = lc + l
    depth = w_mod.shape[0]
    n_img_rows = l // GRID_W

    xs = jnp.concatenate([ctx, x], axis=1).astype(F32)
    cond_rows = jnp.zeros((16, d), F32).at[:b].set(c.astype(F32)).at[b].set(c_ctx.astype(F32))
    cos, sa, sb = _rope_tables(s, lc)
    tables = (cos, sa, sb, _block_ones(A_QK_DIM), _block_ones(N_HEAD_DIM))

    for layer in range(depth):
        last = layer == depth - 1
        with_ctx = not last
        lam_init = 0.8 - 0.6 * math.exp(-0.3 * layer)

        mod = _modulation(cond_rows, w_mod[layer].astype(F32), b_mod[layer].astype(F32))
        mod_lat = mod[:b].reshape(b, 1, 6, d)
        mod_ctx = jnp.broadcast_to(mod[b].reshape(1, 1, 6, d), (b, 1, 6, d))
        modall = jnp.concatenate([mod_ctx, mod_lat], axis=1)

        gains = ((jnp.tile(a_gq[layer].astype(F32), HW // A_QK_DIM) * (A_QK_DIM ** -0.5 * LOG2E)).reshape(1, HW),
                 jnp.tile(a_gk[layer].astype(F32), HW // A_QK_DIM).reshape(1, HW),
                 (jnp.tile(n_gq[layer].astype(F32), N_HEADS) * (N_HEAD_DIM ** -0.5 * LOG2E)).reshape(1, HW),
                 jnp.tile(n_gk[layer].astype(F32), N_HEADS).reshape(1, HW))
        qa, kat, va, qn, knt, vn, u = _in_projection(xs, modall, g_mix[layer].astype(F32), w_in[layer].astype(BF16),
                                                     tables, gains, lc)

        lv = a_lambda[layer].astype(F32)
        lam = (jnp.exp(jnp.sum(lv[0] * lv[1])) - jnp.exp(jnp.sum(lv[2] * lv[3])) + lam_init).reshape(1, 1)
        gsub_t = (jnp.tile(a_gsub[layer].astype(F32), A_HEADS) * (1.0 - lam_init)).reshape(1, HW)
        oa = _diff_attention(qa, kat, va, lam, gsub_t, lc, with_ctx)

        on = _na_attention(qn, knt, vn, _na_bias(n_rpb[layer], n_img_rows), lc, with_ctx)

        u_tm = jnp.transpose(u, (1, 0, 2))
        s5_fwd, s5_bwd = _s5_params(s5_a_re[layer], s5_a_im[layer], s5_log_dt[layer], s5_b_re[layer],
                                    s5_b_im[layer], s5_c_re[layer], s5_c_im[layer], b)
        y_f, y_b = _s5_scan(u_tm, s5_fwd, s5_bwd, lc)
        y = jnp.transpose(y_f + y_b, (1, 0, 2))

        w_router_pad = jnp.zeros((128, d), F32).at[:N_EXPERTS].set(w_router[layer].astype(F32).T)
        x_new, h2, logits_t = _out_projection(oa, on, y, u, xs, modall, g_ffn[layer].astype(F32),
                                            w_out[layer].astype(BF16), s5_d[layer].astype(F32),
                                            s5_w_glu[layer].astype(BF16), s5_b_glu[layer].astype(F32),
                                            w_router_pad, lc, with_ctx)
        rows = x_new.shape[1]
        t = b * rows
        gates = _route(logits_t, e_bias[layer])

        gt2 = modall[:, :, 5, :]
        tiles_per_batch = rows // ROW_TILE
        if with_ctx:
            nctx = lc // ROW_TILE
            sel = (jnp.arange(tiles_per_batch) >= nctx).astype(jnp.int32)
        else:
            sel = jnp.ones((tiles_per_batch,), jnp.int32)
        gt_rows = gt2[:, sel, :].reshape(t // _moe_tile(t), _moe_tile(t) // ROW_TILE, d)

        out = _moe(h2.reshape(t, d), gates, x_new.reshape(t, d), gt_rows,
                   w_gate[layer].astype(BF16), w_up[layer].astype(BF16), w_down[layer].astype(BF16).reshape(-1, d),
                   ws_gate[layer].astype(BF16), ws_up[layer].astype(BF16), ws_down[layer].astype(BF16))
        xs = out.reshape(b, rows, d)

    return xs.astype(x.dtype)
```

```python
import functools
import math

import jax
import jax.numpy as jnp
from jax import lax
from jax.experimental import pallas as pl
from jax.experimental.pallas import tpu as pltpu

F32 = jnp.float32
BF16 = jnp.bfloat16

D_MODEL = 1024
GRID_W = 64
EPS = 1e-6
A_HEADS = 6
A_QK_DIM = 32
A_V_DIM = 64
ROPE_THETA = 10000.0
S5_GROUPS = 16
S5_GROUP_CH = 16
S5_STATE = 64
N_HEADS = 6
N_HEAD_DIM = 64
WIN_ROWS = 8
WIN_COLS = 16
HW = 384
B_WIDTH = 256
Q_COLS = 768
IN_COLS = 2560
N_EXPERTS = 64
TOP_K = 8
N_GROUPS = 8
TOPK_GROUPS = 4
EXPERT_DIM = 256
ROUTED_SCALE = 2.5

ROW_TILE = 256
NA_ROWS = 4
NA_KROWS = 12
S5_CHUNK = 128
MOE_TILE = 1024
MOE_EB = 4
NEG = -1e30
LOG2E = math.log2(math.e)
SCORE_BOUND_LIMIT = 50.0
BOUND_SLACK = 1.05
VMEM_LIMIT = 56 * 1024 * 1024


def _sigmoid(x):
    return 1.0 / (1.0 + jnp.exp(-x))


def _gelu_tanh(x):
    return 0.5 * x * (1.0 + jnp.tanh(math.sqrt(2.0 / math.pi) * (x + 0.044715 * (x * x * x))))


def _split_bf16(a):
    hi = a.astype(BF16)
    lo = (a - hi.astype(F32)).astype(BF16)
    return hi, lo


def _dot(a, b):
    return jnp.dot(a, b, preferred_element_type=F32)


def _dot3(a, b):
    ah, al = _split_bf16(a)
    bh, bl = _split_bf16(b)
    return _dot(ah, bh) + _dot(ah, bl) + _dot(al, bh)


def _cparams(sem):
    return pltpu.CompilerParams(dimension_semantics=sem, vmem_limit_bytes=VMEM_LIMIT)


def _mod_kernel(c_ref, w_ref, b_ref, o_ref):
    c = c_ref[...]
    cond = c * _sigmoid(c)
    o_ref[...] = _dot3(cond, w_ref[...]) + b_ref[...]


def _modulation(cond_rows, w_mod, b_mod):
    r, d = cond_rows.shape
    n = w_mod.shape[1]
    tn = 1536
    return pl.pallas_call(
        _mod_kernel,
        out_shape=jax.ShapeDtypeStruct((r, n), F32),
        grid=(n // tn,),
        in_specs=[pl.BlockSpec((r, d), lambda j: (0, 0)),
                  pl.BlockSpec((d, tn), lambda j: (0, j)),
                  pl.BlockSpec((1, tn), lambda j: (0, j))],
        out_specs=pl.BlockSpec((r, tn), lambda j: (0, j)),
        compiler_params=_cparams(("arbitrary",)),
        name="adaln_mod",
    )(cond_rows, w_mod, b_mod.reshape(1, n))


def _group_rms(t, ones_ref, gain_ref, group):
    ms = _dot((t * t).astype(BF16), ones_ref[...]) * (1.0 / group)
    return t * lax.rsqrt(ms + EPS) * gain_ref[...]


def _rope(t, cos_ref, sa_ref, sb_ref):
    up = pltpu.roll(t, HW - 8, 1)
    dn = pltpu.roll(t, 8, 1)
    return t * cos_ref[...] + up * sa_ref[...] + dn * sb_ref[...]


def _proj_kernel(x_ref, mod_ref, g_ref, w_ref, cos_ref, sa_ref, sb_ref, ones32_ref, ones64_ref,
                 gqa_ref, gka_ref, gqn_ref, gkn_ref,
                 qa_ref, kat_ref, va_ref, qn_ref, knt_ref, vn_ref, u_ref):
    x = x_ref[0]
    mod = mod_ref[0, 0]
    ms = jnp.mean(x * x, axis=-1, keepdims=True)
    h = x * lax.rsqrt(ms + EPS) * g_ref[...] * (1.0 + mod[1:2]) + mod[0:1]
    hb = h.astype(BF16)

    def sec(a, b):
        return _dot(hb, w_ref[:, a:b])

    qa = _rope(_group_rms(sec(0, 384), ones32_ref, gqa_ref, A_QK_DIM), cos_ref, sa_ref, sb_ref)
    qa_ref[0] = qa.astype(BF16)
    qn_ref[0] = _group_rms(sec(384, 768), ones64_ref, gqn_ref, N_HEAD_DIM).astype(BF16)
    ka = _rope(_group_rms(sec(768, 1152), ones32_ref, gka_ref, A_QK_DIM), cos_ref, sa_ref, sb_ref)
    kat_ref[0] = ka.T.astype(BF16)
    va_ref[0] = sec(1152, 1536).astype(BF16)
    kn = _group_rms(sec(1536, 1920), ones64_ref, gkn_ref, N_HEAD_DIM)
    knt_ref[0] = kn.T.astype(BF16)
    vn_ref[0] = sec(1920, 2304).astype(BF16)
    u_ref[0] = sec(2304, 2560)


def _in_projection(xs, modall, g_mix, w_in_bf, tables, gains, lc):
    b, s, d = xs.shape
    tm = ROW_TILE
    cos, sa, sb, ones32, ones64 = tables
    row = lambda i, bb: (bb, i, 0)
    tab = lambda i, bb: (i, 0)
    const2 = lambda i, bb: (0, 0)
    act = lambda w, dt: jax.ShapeDtypeStruct((b, s, w), dt)
    act_t = jax.ShapeDtypeStruct((b, HW, s), BF16)
    return pl.pallas_call(
        _proj_kernel,
        out_shape=(act(HW, BF16), act_t, act(HW, BF16), act(HW, BF16), act_t, act(HW, BF16), act(B_WIDTH, F32)),
        grid=(s // tm, b),
        in_specs=[pl.BlockSpec((1, tm, d), row),
                  pl.BlockSpec((1, 1, 6, d), lambda i, bb: (bb, jnp.minimum(i, 1), 0, 0)),
                  pl.BlockSpec((1, d), const2),
                  pl.BlockSpec((d, IN_COLS), const2),
                  pl.BlockSpec((tm, HW), tab), pl.BlockSpec((tm, HW), tab), pl.BlockSpec((tm, HW), tab),
                  pl.BlockSpec((HW, HW), const2), pl.BlockSpec((HW, HW), const2),
                  pl.BlockSpec((1, HW), const2), pl.BlockSpec((1, HW), const2),
                  pl.BlockSpec((1, HW), const2), pl.BlockSpec((1, HW), const2)],
        out_specs=(pl.BlockSpec((1, tm, HW), row),
                   pl.BlockSpec((1, HW, tm), lambda i, bb: (bb, 0, i)),
                   pl.BlockSpec((1, tm, HW), row),
                   pl.BlockSpec((1, tm, HW), row),
                   pl.BlockSpec((1, HW, tm), lambda i, bb: (bb, 0, i)),
                   pl.BlockSpec((1, tm, HW), row),
                   pl.BlockSpec((1, tm, B_WIDTH), row)),
        compiler_params=_cparams(("arbitrary", "arbitrary")),
        name="in_proj",
    )(xs, modall, g_mix.reshape(1, d), w_in_bf, cos, sa, sb, ones32, ones64, *gains)


def _rope_tables(s, lc):
    p = jnp.arange(s)
    pos = jnp.maximum(p - lc, 0)
    rows = (pos // GRID_W).astype(F32)
    cols = (pos % GRID_W).astype(F32)
    lane = jnp.arange(HW)
    j32 = lane % A_QK_DIM
    half = j32 // 16
    i16 = j32 % 16
    nf = 8
    inv = ROPE_THETA ** (-(i16 % nf).astype(F32) / nf)
    coord = jnp.where(half[None, :] == 0, rows[:, None], cols[:, None])
    ang = coord * inv[None, :]
    is_lat = (p >= lc)[:, None]
    second = (i16 >= nf)[None, :]
    cos = jnp.where(is_lat, jnp.cos(ang), 1.0)
    sin = jnp.where(is_lat, jnp.sin(ang), 0.0)
    sa = jnp.where(second, 0.0, -sin)
    sb = jnp.where(second, sin, 0.0)
    return cos.astype(F32), sa.astype(F32), sb.astype(F32)


def _block_ones(group):
    g = jnp.arange(HW) // group
    return (g[:, None] == g[None, :]).astype(BF16)


def _diff_attend(q_all, kt_ref, v_ref, lam, gsub_ref, sk, use_max):
    outs = []
    for h in range(A_HEADS):
        v = v_ref[0, 0:sk, h * A_V_DIM:(h + 1) * A_V_DIM]
        parts = []
        for sub in range(2):
            hs = 2 * h + sub
            off = hs * A_QK_DIM
            sc = _dot(q_all[:, off:off + A_QK_DIM], kt_ref[0, off:off + A_QK_DIM, 0:sk])
            e = jnp.exp2(sc - jnp.max(sc, axis=-1, keepdims=True)) if use_max else jnp.exp2(sc)
            parts.append(_dot(e.astype(BF16), v) * (1.0 / jnp.sum(e, axis=-1, keepdims=True)))
        o = parts[0] - lam * parts[1]
        outs.append(o * lax.rsqrt(jnp.mean(o * o, axis=-1, keepdims=True) + EPS))
    return (jnp.concatenate(outs, axis=-1) * gsub_ref[...]).astype(BF16)


def _diff_attn_kernel(q_ref, kt_ref, v_ref, lam_ref, gsub_ref, o_ref, *, lc, ctx_first, use_max):
    lam = lam_ref[...]
    s = kt_ref.shape[2]
    if ctx_first:
        i = pl.program_id(1)

        @pl.when(i == 0)
        def _():
            o_ref[0] = _diff_attend(q_ref[0], kt_ref, v_ref, lam, gsub_ref, lc, use_max)

        @pl.when(i > 0)
        def _():
            o_ref[0] = _diff_attend(q_ref[0], kt_ref, v_ref, lam, gsub_ref, s, use_max)
    else:
        o_ref[0] = _diff_attend(q_ref[0], kt_ref, v_ref, lam, gsub_ref, s, use_max)


def _diff_attention(qa, kat, va, lam, gsub_t, lc, with_ctx, use_max):
    b, s, _ = qa.shape
    tq = ROW_TILE
    off = 0 if with_ctx else lc // tq
    rows_out = s - off * tq
    return pl.pallas_call(
        functools.partial(_diff_attn_kernel, lc=lc, ctx_first=with_ctx, use_max=use_max),
        out_shape=jax.ShapeDtypeStruct((b, rows_out, HW), BF16),
        grid=(b, rows_out // tq),
        in_specs=[pl.BlockSpec((1, tq, HW), lambda bb, i: (bb, i + off, 0)),
                  pl.BlockSpec((1, HW, s), lambda bb, i: (bb, 0, 0)),
                  pl.BlockSpec((1, s, HW), lambda bb, i: (bb, 0, 0)),
                  pl.BlockSpec((1, 1), lambda bb, i: (0, 0)),
                  pl.BlockSpec((1, HW), lambda bb, i: (0, 0))],
        out_specs=pl.BlockSpec((1, tq, HW), lambda bb, i: (bb, i, 0)),
        compiler_params=_cparams(("arbitrary", "arbitrary")),
        name="diff_attn_rowmax" if use_max else "diff_attn",
    )(qa, kat, va, lam, gsub_t)


def _na_ctx_attend(q_all, kt_ref, v_ref, lc, use_max):
    outs = []
    for h in range(N_HEADS):
        hs = slice(h * N_HEAD_DIM, (h + 1) * N_HEAD_DIM)
        sc = _dot(q_all[:, hs], kt_ref[0, hs, 0:lc])
        e = jnp.exp2(sc - jnp.max(sc, axis=-1, keepdims=True)) if use_max else jnp.exp2(sc)
        o = _dot(e.astype(BF16), v_ref[0, 0:lc, hs])
        outs.append(o * (1.0 / jnp.sum(e, axis=-1, keepdims=True)))
    return jnp.concatenate(outs, axis=-1).astype(BF16)


def _na_attend(q_all, kt_ref, v_ref, bias_ref, koff, lc, use_max):
    nk = NA_KROWS * GRID_W
    outs = []
    for h in range(N_HEADS):
        hs = slice(h * N_HEAD_DIM, (h + 1) * N_HEAD_DIM)
        q = q_all[:, hs]
        s_loc = _dot(q, kt_ref[0, hs, pl.ds(koff, nk)]) + bias_ref[0, h]
        s_ctx = _dot(q, kt_ref[0, hs, 0:lc])
        if use_max:
            m = jnp.maximum(jnp.max(s_loc, axis=-1, keepdims=True), jnp.max(s_ctx, axis=-1, keepdims=True))
            s_loc, s_ctx = s_loc - m, s_ctx - m
        e_loc = jnp.exp2(s_loc)
        e_ctx = jnp.exp2(s_ctx)
        den = jnp.sum(e_loc, axis=-1, keepdims=True) + jnp.sum(e_ctx, axis=-1, keepdims=True)
        o = _dot(e_loc.astype(BF16), v_ref[0, pl.ds(koff, nk), hs]) + _dot(e_ctx.astype(BF16), v_ref[0, 0:lc, hs])
        outs.append(o * (1.0 / den))
    return jnp.concatenate(outs, axis=-1).astype(BF16)


def _na_kernel(q_ref, kt_ref, v_ref, bias_ref, o_ref, *, lc, n_img_rows, ctx_first, use_max):
    i = pl.program_id(1)
    blk = i - 1 if ctx_first else i
    start_row = jnp.clip(NA_ROWS * blk - WIN_ROWS // 2, 0, n_img_rows - NA_KROWS)
    koff = pl.multiple_of(lc + start_row * GRID_W, 128)
    if ctx_first:
        @pl.when(i == 0)
        def _():
            o_ref[0] = _na_ctx_attend(q_ref[0], kt_ref, v_ref, lc, use_max)

        @pl.when(i > 0)
        def _():
            o_ref[0] = _na_attend(q_ref[0], kt_ref, v_ref, bias_ref, koff, lc, use_max)
    else:
        o_ref[0] = _na_attend(q_ref[0], kt_ref, v_ref, bias_ref, koff, lc, use_max)


def _na_attention(qn, knt, vn, bias, lc, with_ctx, use_max):
    b, s, _ = qn.shape
    tq = NA_ROWS * GRID_W
    assert tq == ROW_TILE and lc % tq == 0
    n_img_rows = (s - lc) // GRID_W
    nblk = n_img_rows // NA_ROWS
    off = 0 if with_ctx else lc // tq
    rows_out = s - off * tq
    first = 1 if with_ctx else 0

    def variant(bb, i):
        blk = i - first
        return (jnp.where(blk <= 0, 0, jnp.where(blk == nblk - 1, 2, 1)), 0, 0, 0)

    return pl.pallas_call(
        functools.partial(_na_kernel, lc=lc, n_img_rows=n_img_rows, ctx_first=with_ctx, use_max=use_max),
        out_shape=jax.ShapeDtypeStruct((b, rows_out, HW), BF16),
        grid=(b, rows_out // tq),
        in_specs=[pl.BlockSpec((1, tq, HW), lambda bb, i: (bb, i + off, 0)),
                  pl.BlockSpec((1, HW, s), lambda bb, i: (bb, 0, 0)),
                  pl.BlockSpec((1, s, HW), lambda bb, i: (bb, 0, 0)),
                  pl.BlockSpec((1, N_HEADS, tq, NA_KROWS * GRID_W), variant)],
        out_specs=pl.BlockSpec((1, tq, HW), lambda bb, i: (bb, i, 0)),
        compiler_params=_cparams(("arbitrary", "arbitrary")),
        name="nbr_attn_rowmax" if use_max else "nbr_attn",
    )(qn, knt, vn, bias)


def _na_bias(rpb, n_img_rows):
    a = jnp.arange(NA_ROWS)[:, None, None, None]
    cq = jnp.arange(GRID_W)[None, :, None, None]
    j = jnp.arange(NA_KROWS)[None, None, :, None]
    ck = jnp.arange(GRID_W)[None, None, None, :]
    cstart = jnp.clip(cq - WIN_COLS // 2, 0, GRID_W - WIN_COLS)
    colmask = (ck >= cstart) & (ck < cstart + WIN_COLS)
    dc = jnp.clip(ck - cq, -(WIN_COLS - 1), WIN_COLS - 1) + (WIN_COLS - 1)
    by_col = jnp.take(rpb.astype(F32), dc.reshape(-1), axis=2).reshape(N_HEADS, 2 * WIN_ROWS - 1, GRID_W, GRID_W)
    out = []
    for r0_minus_k, wstart in ((0, 0 * a), (WIN_ROWS // 2, a), (NA_KROWS - NA_ROWS, NA_KROWS - WIN_ROWS + 0 * a)):
        inwin = (j >= wstart) & (j < wstart + WIN_ROWS)
        dr = jnp.clip(j - r0_minus_k - a + (WIN_ROWS - 1), 0, 2 * WIN_ROWS - 2)
        vals = jnp.take(by_col, dr.reshape(-1), axis=1).reshape(N_HEADS, NA_ROWS, NA_KROWS, GRID_W, GRID_W)
        vals = jnp.transpose(vals, (0, 1, 3, 2, 4))
        vals = jnp.where((inwin & colmask)[None], vals * LOG2E, NEG)
        out.append(vals.reshape(N_HEADS, NA_ROWS * GRID_W, NA_KROWS * GRID_W))
    return jnp.stack(out)


def _s5_kernel(uf_ref, ub_ref, bmf_ref, cmf_ref, arf_ref, aif_ref, bmb_ref, cmb_ref, arb_ref, aib_ref,
               yf_ref, yb_ref, xf_scr, xb_scr, st_scr):
    tc, nb, w = uf_ref.shape
    ns = arf_ref.shape[1]

    @pl.when(pl.program_id(0) == 0)
    def _():
        st_scr[...] = jnp.zeros_like(st_scr)

    xf_scr[...] = _dot(uf_ref[...].reshape(tc * nb, w).astype(BF16), bmf_ref[...])
    xb_scr[...] = _dot(ub_ref[...].reshape(tc * nb, w).astype(BF16), bmb_ref[...])

    def advance(x_scr, a_re, a_im, s_re, s_im, tt):
        rows = pl.ds(pl.multiple_of(tt * nb, nb), nb)
        n_re = a_re * s_re - a_im * s_im + x_scr[rows, 0:ns]
        n_im = a_re * s_im + a_im * s_re + x_scr[rows, ns:2 * ns]
        x_scr[rows, 0:ns] = n_re
        x_scr[rows, ns:2 * ns] = n_im
        return n_re, n_im

    def step(t, carry):
        f_re, f_im, b_re, b_im = carry
        f_re, f_im = advance(xf_scr, arf_ref[...], aif_ref[...], f_re, f_im, t)
        b_re, b_im = advance(xb_scr, arb_ref[...], aib_ref[...], b_re, b_im, tc - 1 - t)
        return f_re, f_im, b_re, b_im

    init = (st_scr[0, :, 0:ns], st_scr[0, :, ns:2 * ns], st_scr[1, :, 0:ns], st_scr[1, :, ns:2 * ns])
    f_re, f_im, b_re, b_im = lax.fori_loop(0, tc, step, init, unroll=2)
    st_scr[0, :, 0:ns] = f_re
    st_scr[0, :, ns:2 * ns] = f_im
    st_scr[1, :, 0:ns] = b_re
    st_scr[1, :, ns:2 * ns] = b_im
    yf_ref[...] = _dot(xf_scr[...].astype(BF16), cmf_ref[...]).reshape(tc, nb, w)
    yb_ref[...] = _dot(xb_scr[...].astype(BF16), cmb_ref[...]).reshape(tc, nb, w)


def _s5_scan(u_tm, fwd, bwd, lc):
    s, nb, w = u_tm.shape
    tc = S5_CHUNK
    nc, ncc = s // tc, lc // tc
    ns = fwd[2].shape[1]
    chunk_f = lambda j: (j, 0, 0)
    chunk_b = lambda j: (jnp.where(j < ncc, ncc - 1 - j, nc - 1 - (j - ncc)), 0, 0)
    const = lambda j: (0, 0)
    pspecs = [pl.BlockSpec((w, 2 * ns), const), pl.BlockSpec((2 * ns, w), const),
              pl.BlockSpec((nb, ns), const), pl.BlockSpec((nb, ns), const)]
    out = jax.ShapeDtypeStruct((s, nb, w), F32)
    return pl.pallas_call(
        _s5_kernel,
        out_shape=(out, out),
        grid=(nc,),
        in_specs=[pl.BlockSpec((tc, nb, w), chunk_f), pl.BlockSpec((tc, nb, w), chunk_b)] + pspecs + pspecs,
        out_specs=(pl.BlockSpec((tc, nb, w), chunk_f), pl.BlockSpec((tc, nb, w), chunk_b)),
        scratch_shapes=[pltpu.VMEM((tc * nb, 2 * ns), F32), pltpu.VMEM((tc * nb, 2 * ns), F32),
                        pltpu.VMEM((2, nb, 2 * ns), F32)],
        compiler_params=_cparams(("arbitrary",)),
        name="s5_scan",
    )(u_tm, u_tm, *fwd, *bwd)


def _s5_params(a_re, a_im, log_dt, b_re, b_im, c_re, c_im, nb):
    g, n, p = S5_GROUPS, S5_STATE, S5_GROUP_CH
    lr, li = a_re.astype(F32), a_im.astype(F32)
    dt = jnp.exp(log_dt.astype(F32))[..., None]
    mag = jnp.exp(lr * dt)
    ab_r, ab_i = mag * jnp.cos(li * dt), mag * jnp.sin(li * dt)
    den = lr * lr + li * li
    cf_r = ((ab_r - 1.0) * lr + ab_i * li) / den
    cf_i = (ab_i * lr - (ab_r - 1.0) * li) / den
    br, bi = b_re.astype(F32), b_im.astype(F32)
    bb_r = cf_r[..., None] * br - cf_i[..., None] * bi
    bb_i = cf_r[..., None] * bi + cf_i[..., None] * br
    eye = jnp.eye(g, dtype=F32)
    out = []
    for k in range(2):
        b_r = jnp.einsum('gnp,gh->gphn', bb_r[k], eye).reshape(g * p, g * n)
        b_i = jnp.einsum('gnp,gh->gphn', bb_i[k], eye).reshape(g * p, g * n)
        bmat = jnp.concatenate([b_r, b_i], axis=1).astype(BF16)
        ct = jnp.transpose(c_re[k].astype(F32), (0, 2, 1))
        ci = jnp.transpose(c_im[k].astype(F32), (0, 2, 1))
        c_r = jnp.einsum('gnp,gh->gnhp', ct, eye).reshape(g * n, g * p)
        c_i = jnp.einsum('gnp,gh->gnhp', ci, eye).reshape(g * n, g * p)
        cmat = jnp.concatenate([c_r, -c_i], axis=0).astype(BF16)
        ar = jnp.broadcast_to(ab_r[k].reshape(1, g * n), (nb, g * n))
        ai = jnp.broadcast_to(ab_i[k].reshape(1, g * n), (nb, g * n))
        out.append((bmat, cmat, ar, ai))
    return out


def _out_kernel(oa_ref, on_ref, y_ref, u_ref, x_ref, mod_ref, gffn_ref, wo_ref, dskip_ref, wglu_ref, bglu_ref,
                wr_ref, xo_ref, h2_ref, lg_ref):
    mod = mod_ref[0, 0]
    g = _gelu_tanh(y_ref[0] + dskip_ref[...] * u_ref[0])
    ob = g * _sigmoid(_dot(g.astype(BF16), wglu_ref[...]) + bglu_ref[...])
    mix = (_dot(oa_ref[0], wo_ref[0:HW, :]) + _dot(ob.astype(BF16), wo_ref[HW:HW + B_WIDTH, :])
           + _dot(on_ref[0], wo_ref[HW + B_WIDTH:, :]))
    x = x_ref[0] + mod[2:3] * mix
    xo_ref[0] = x
    ms = jnp.mean(x * x, axis=-1, keepdims=True)
    h2 = x * lax.rsqrt(ms + EPS) * gffn_ref[...] * (1.0 + mod[4:5]) + mod[3:4]
    h2_ref[0] = h2.astype(BF16)
    nt = lambda a, bb: lax.dot_general(a, bb, (((1,), (1,)), ((), ())), preferred_element_type=F32)
    wh, wl = _split_bf16(wr_ref[...])
    hh, hl = _split_bf16(h2)
    lg_ref[0] = nt(wh, hh) + nt(wh, hl) + nt(wl, hh)


def _out_projection(oa, on, y, u, xs, modall, g_ffn, w_out_bf, d_skip, w_glu_bf, b_glu, w_router_pad, lc, with_ctx):
    b, s, d = xs.shape
    tm = ROW_TILE
    off = 0 if with_ctx else lc // tm
    rows_out = s - off * tm
    full = lambda bb, i: (bb, i + off, 0)
    outr = lambda bb, i: (bb, i, 0)
    const = lambda bb, i: (0, 0)
    ne = w_router_pad.shape[0]
    return pl.pallas_call(
        _out_kernel,
        out_shape=(jax.ShapeDtypeStruct((b, rows_out, d), F32),
                   jax.ShapeDtypeStruct((b, rows_out, d), BF16),
                   jax.ShapeDtypeStruct((b, ne, rows_out), F32)),
        grid=(b, rows_out // tm),
        in_specs=[pl.BlockSpec((1, tm, HW), outr),
                  pl.BlockSpec((1, tm, HW), outr),
                  pl.BlockSpec((1, tm, B_WIDTH), full),
                  pl.BlockSpec((1, tm, B_WIDTH), full),
                  pl.BlockSpec((1, tm, d), full),
                  pl.BlockSpec((1, 1, 6, d), lambda bb, i: (bb, jnp.minimum(i + off, 1), 0, 0)),
                  pl.BlockSpec((1, d), const),
                  pl.BlockSpec((d, d), const),
                  pl.BlockSpec((1, B_WIDTH), const),
                  pl.BlockSpec((B_WIDTH, B_WIDTH), const),
                  pl.BlockSpec((1, B_WIDTH), const),
                  pl.BlockSpec((ne, d), const)],
        out_specs=(pl.BlockSpec((1, tm, d), outr), pl.BlockSpec((1, tm, d), outr),
                   pl.BlockSpec((1, ne, tm), lambda bb, i: (bb, 0, i))),
        compiler_params=_cparams(("arbitrary", "arbitrary")),
        name="out_proj",
    )(oa, on, y, u, xs, modall, g_ffn.reshape(1, d), w_out_bf, d_skip.reshape(1, -1), w_glu_bf,
      b_glu.reshape(1, -1), w_router_pad)


def _swiglu(h, wg, wu):
    a = _dot(h, wg)
    return a * _sigmoid(a) * _dot(h, wu)


def _moe_kernel(h_ref, gates_ref, x_ref, gt_ref, wg_ref, wu_ref, wd_ref, sg_ref, su_ref, sd_ref, o_ref, acc_ref):
    e = pl.program_id(1)
    n_routed = pl.num_programs(1) - 1

    @pl.when(e == 0)
    def _():
        acc_ref[...] = jnp.zeros_like(acc_ref)

    @pl.when(e < n_routed)
    def _():
        h = h_ref[...]
        src = lax.broadcasted_iota(jnp.int32, (128, 128), 0)
        dst = lax.broadcasted_iota(jnp.int32, (128, 128), 1)
        sel = jnp.where(src == e * MOE_EB + dst, jnp.where(dst < MOE_EB, 1.0, 0.0), 0.0).astype(BF16)
        g_hi, g_lo = _split_bf16(gates_ref[...])
        gsel = _dot(g_hi, sel) + _dot(g_lo, sel)
        hid = [(_swiglu(h, wg_ref[j], wu_ref[j]) * gsel[:, j:j + 1]).astype(BF16) for j in range(MOE_EB)]
        acc_ref[...] += _dot(jnp.concatenate(hid, axis=-1), wd_ref[...])

    @pl.when(e == n_routed)
    def _():
        for j in range(h_ref.shape[0] // ROW_TILE):
            rows = slice(j * ROW_TILE, (j + 1) * ROW_TILE)
            hs = _swiglu(h_ref[rows, :], sg_ref[...], su_ref[...]).astype(BF16)
            y = acc_ref[rows, :] + _dot(hs, sd_ref[...])
            o_ref[rows, :] = x_ref[rows, :] + gt_ref[0, j:j + 1, :] * y


def _moe_tile(t):
    return max(m for m in range(ROW_TILE, MOE_TILE + 1, ROW_TILE) if t % m == 0)


def _moe(h2, gates, x, gt_rows, wg, wu, wd, sg, su, sd):
    t, d = h2.shape
    tm = _moe_tile(t)
    n_routed = wg.shape[0] // MOE_EB
    nsub = tm // ROW_TILE
    step = lambda e: jnp.minimum(e, n_routed - 1)
    const = lambda i, e: (0, 0)
    return pl.pallas_call(
        _moe_kernel,
        out_shape=jax.ShapeDtypeStruct((t, d), F32),
        grid=(t // tm, n_routed + 1),
        in_specs=[pl.BlockSpec((tm, d), lambda i, e: (i, 0)),
                  pl.BlockSpec((tm, 128), lambda i, e: (i, 0)),
                  pl.BlockSpec((tm, d), lambda i, e: (i, 0)),
                  pl.BlockSpec((1, nsub, d), lambda i, e: (i, 0, 0)),
                  pl.BlockSpec((MOE_EB, d, EXPERT_DIM), lambda i, e: (step(e), 0, 0)),
                  pl.BlockSpec((MOE_EB, d, EXPERT_DIM), lambda i, e: (step(e), 0, 0)),
                  pl.BlockSpec((MOE_EB * EXPERT_DIM, d), lambda i, e: (step(e), 0)),
                  pl.BlockSpec((d, EXPERT_DIM), const),
                  pl.BlockSpec((d, EXPERT_DIM), const),
                  pl.BlockSpec((EXPERT_DIM, d), const)],
        out_specs=pl.BlockSpec((tm, d), lambda i, e: (i, 0)),
        scratch_shapes=[pltpu.VMEM((tm, d), F32)],
        compiler_params=_cparams(("arbitrary", "arbitrary")),
        name="moe_ffn",
    )(h2, gates, x, gt_rows, wg, wu, wd, sg, su, sd)


def _route_kernel(lg_ref, bias_ref, o_ref):
    gsz = N_EXPERTS // N_GROUPS
    tn = lg_ref.shape[2]
    ninf = -jnp.inf
    jidx = lax.broadcasted_iota(jnp.int32, (gsz, tn), 0)
    scores, biased, gscore = [], [], []
    for g in range(N_GROUPS):
        rows = slice(g * gsz, (g + 1) * gsz)
        sc = _sigmoid(lg_ref[0, rows, :])
        bi = sc + bias_ref[rows, :]
        m1 = jnp.max(bi, axis=0, keepdims=True)
        first = jnp.min(jnp.where(bi == m1, jidx, gsz), axis=0, keepdims=True)
        m2 = jnp.max(jnp.where(jidx == first, ninf, bi), axis=0, keepdims=True)
        scores.append(sc)
        biased.append(bi)
        gscore.append(m1 + m2)
    masked = []
    for g in range(N_GROUPS):
        rank = jnp.zeros((1, tn), F32)
        for g2 in range(N_GROUPS):
            if g2 != g:
                ahead = (gscore[g2] >= gscore[g]) if g2 < g else (gscore[g2] > gscore[g])
                rank = rank + jnp.where(ahead, 1.0, 0.0)
        keep = jnp.broadcast_to(rank, (gsz, tn)) < TOPK_GROUPS
        masked.append(jnp.where(keep, biased[g], ninf))
    ranks = [jnp.zeros((gsz, tn), F32) for _ in range(N_GROUPS)]
    for g2 in range(N_GROUPS):
        for j2 in range(gsz):
            other = jnp.broadcast_to(masked[g2][j2:j2 + 1, :], (gsz, tn))
            for g in range(N_GROUPS):
                ge = jnp.where(other >= masked[g], 1.0, 0.0)
                gt = jnp.where(other > masked[g], 1.0, 0.0)
                if g2 < g:
                    ahead = ge
                elif g2 > g:
                    ahead = gt
                else:
                    ahead = jnp.where(jidx > j2, ge, gt)
                ranks[g] = ranks[g] + ahead
    picked = [jnp.where(ranks[g] < TOP_K, scores[g], 0.0) for g in range(N_GROUPS)]
    den = sum(jnp.sum(p, axis=0, keepdims=True) for p in picked)
    scale = ROUTED_SCALE / den
    shared_row = jnp.where(lax.broadcasted_iota(jnp.int32, (128 - N_EXPERTS, tn), 0) == 0, 1.0, 0.0)
    gates_t = jnp.concatenate([p * scale for p in picked] + [shared_row], axis=0)
    o_ref[...] = gates_t.T


def _route(logits_t, e_bias):
    b, ne, rows = logits_t.shape
    tn = ROW_TILE
    nt = rows // tn
    bias = jnp.broadcast_to(e_bias.astype(F32)[:, None], (N_EXPERTS, tn))
    return pl.pallas_call(
        _route_kernel,
        out_shape=jax.ShapeDtypeStruct((b * rows, ne), F32),
        grid=(b, nt),
        in_specs=[pl.BlockSpec((1, ne, tn), lambda bb, i: (bb, 0, i)),
                  pl.BlockSpec((N_EXPERTS, tn), lambda bb, i: (0, 0))],
        out_specs=pl.BlockSpec((tn, ne), lambda bb, i: (bb * nt + i, 0)),
        compiler_params=_cparams(("arbitrary", "arbitrary")),
        name="moe_route",
    )(logits_t, bias)


def kernel(x, c, ctx, c_ctx, w_mod, b_mod, g_mix, g_ffn, w_in, w_out, a_gq, a_gk, a_lambda, a_gsub, n_gq, n_gk, n_rpb, s5_a_re, s5_a_im, s5_log_dt, s5_b_re, s5_b_im, s5_c_re, s5_c_im, s5_d, s5_w_glu, s5_b_glu, w_router, e_bias, w_gate, w_up, w_down, ws_gate, ws_up, ws_down):
    b, l, d = x.shape
    lc = ctx.shape[1]
    s = lc + l
    depth = w_mod.shape[0]
    n_img_rows = l // GRID_W

    xs = jnp.concatenate([ctx, x], axis=1).astype(F32)
    cond_rows = jnp.zeros((16, d), F32).at[:b].set(c.astype(F32)).at[b].set(c_ctx.astype(F32))
    cos, sa, sb = _rope_tables(s, lc)
    tables = (cos, sa, sb, _block_ones(A_QK_DIM), _block_ones(N_HEAD_DIM))

    for layer in range(depth):
        last = layer == depth - 1
        with_ctx = not last
        lam_init = 0.8 - 0.6 * math.exp(-0.3 * layer)

        mod = _modulation(cond_rows, w_mod[layer].astype(F32), b_mod[layer].astype(F32))
        mod_lat = mod[:b].reshape(b, 1, 6, d)
        mod_ctx = jnp.broadcast_to(mod[b].reshape(1, 1, 6, d), (b, 1, 6, d))
        modall = jnp.concatenate([mod_ctx, mod_lat], axis=1)

        gains = ((jnp.tile(a_gq[layer].astype(F32), HW // A_QK_DIM) * (A_QK_DIM ** -0.5 * LOG2E)).reshape(1, HW),
                 jnp.tile(a_gk[layer].astype(F32), HW // A_QK_DIM).reshape(1, HW),
                 (jnp.tile(n_gq[layer].astype(F32), N_HEADS) * (N_HEAD_DIM ** -0.5 * LOG2E)).reshape(1, HW),
                 jnp.tile(n_gk[layer].astype(F32), N_HEADS).reshape(1, HW))
        qa, kat, va, qn, knt, vn, u = _in_projection(xs, modall, g_mix[layer].astype(F32), w_in[layer].astype(BF16),
                                                     tables, gains, lc)

        lv = a_lambda[layer].astype(F32)
        lam = (jnp.exp(jnp.sum(lv[0] * lv[1])) - jnp.exp(jnp.sum(lv[2] * lv[3])) + lam_init).reshape(1, 1)
        gsub_t = (jnp.tile(a_gsub[layer].astype(F32), A_HEADS) * (1.0 - lam_init)).reshape(1, HW)
        a_bound = BOUND_SLACK * A_QK_DIM * jnp.max(jnp.abs(gains[0])) * jnp.max(jnp.abs(gains[1]))
        oa = lax.cond(a_bound <= SCORE_BOUND_LIMIT,
                      lambda *a: _diff_attention(*a, lc, with_ctx, False),
                      lambda *a: _diff_attention(*a, lc, with_ctx, True), qa, kat, va, lam, gsub_t)

        n_bound = (BOUND_SLACK * N_HEAD_DIM * jnp.max(jnp.abs(gains[2])) * jnp.max(jnp.abs(gains[3]))
                   + LOG2E * jnp.max(jnp.abs(n_rpb[layer].astype(F32))))
        on = lax.cond(n_bound <= SCORE_BOUND_LIMIT,
                      lambda *a: _na_attention(*a, lc, with_ctx, False),
                      lambda *a: _na_attention(*a, lc, with_ctx, True),
                      qn, knt, vn, _na_bias(n_rpb[layer], n_img_rows))

        u_tm = jnp.transpose(u, (1, 0, 2))
        s5_fwd, s5_bwd = _s5_params(s5_a_re[layer], s5_a_im[layer], s5_log_dt[layer], s5_b_re[layer],
                                    s5_b_im[layer], s5_c_re[layer], s5_c_im[layer], b)
        y_f, y_b = _s5_scan(u_tm, s5_fwd, s5_bwd, lc)
        y = jnp.transpose(y_f + y_b, (1, 0, 2))

        w_router_pad = jnp.zeros((128, d), F32).at[:N_EXPERTS].set(w_router[layer].astype(F32).T)
        x_new, h2, logits_t = _out_projection(oa, on, y, u, xs, modall, g_ffn[layer].astype(F32),
                                            w_out[layer].astype(BF16), s5_d[layer].astype(F32),
                                            s5_w_glu[layer].astype(BF16), s5_b_glu[layer].astype(F32),
                                            w_router_pad, lc, with_ctx)
        rows = x_new.shape[1]
        t = b * rows
        gates = _route(logits_t, e_bias[layer])

        gt2 = modall[:, :, 5, :]
        tiles_per_batch = rows // ROW_TILE
        if with_ctx:
            nctx = lc // ROW_TILE
            sel = (jnp.arange(tiles_per_batch) >= nctx).astype(jnp.int32)
        else:
            sel = jnp.ones((tiles_per_batch,), jnp.int32)
        gt_rows = gt2[:, sel, :].reshape(t // _moe_tile(t), _moe_tile(t) // ROW_TILE, d)

        out = _moe(h2.reshape(t, d), gates, x_new.reshape(t, d), gt_rows,
                   w_gate[layer].astype(BF16), w_up[layer].astype(BF16), w_down[layer].astype(BF16).reshape(-1, d),
                   ws_gate[layer].astype(BF16), ws_up[layer].astype(BF16), ws_down[layer].astype(BF16))
        xs = out.reshape(b, rows, d)

    return xs.astype(x.dtype)
```

```python
import functools
import math

import jax
import jax.numpy as jnp
from jax import lax
from jax.experimental import pallas as pl
from jax.experimental.pallas import tpu as pltpu

F32 = jnp.float32
BF16 = jnp.bfloat16

D_MODEL = 1024
GRID_W = 64
EPS = 1e-6
A_HEADS = 6
A_QK_DIM = 32
A_V_DIM = 64
ROPE_THETA = 10000.0
S5_GROUPS = 16
S5_GROUP_CH = 16
S5_STATE = 64
N_HEADS = 6
N_HEAD_DIM = 64
WIN_ROWS = 8
WIN_COLS = 16
HW = 384
B_WIDTH = 256
Q_COLS = 768
IN_COLS = 2560
N_EXPERTS = 64
TOP_K = 8
N_GROUPS = 8
TOPK_GROUPS = 4
EXPERT_DIM = 256
ROUTED_SCALE = 2.5

ROW_TILE = 256
NA_ROWS = 4
NA_KROWS = 12
S5_CHUNK = 128
MOE_TILE = 1024
MOE_EB = 4
SUBROWS = 8
NEG = -1e30
LOG2E = math.log2(math.e)
SCORE_BOUND_LIMIT = 50.0
BOUND_SLACK = 1.05
VMEM_LIMIT = 56 * 1024 * 1024


def _sigmoid(x):
    return 1.0 / (1.0 + jnp.exp(-x))


def _gelu_tanh(x):
    return 0.5 * x * (1.0 + jnp.tanh(math.sqrt(2.0 / math.pi) * (x + 0.044715 * (x * x * x))))


def _split_bf16(a):
    hi = a.astype(BF16)
    lo = (a - hi.astype(F32)).astype(BF16)
    return hi, lo


def _dot(a, b):
    return jnp.dot(a, b, preferred_element_type=F32)


def _dot3(a, b):
    ah, al = _split_bf16(a)
    bh, bl = _split_bf16(b)
    return _dot(ah, bh) + _dot(ah, bl) + _dot(al, bh)


def _cparams(sem):
    return pltpu.CompilerParams(dimension_semantics=sem, vmem_limit_bytes=VMEM_LIMIT)


def _mod_kernel(c_ref, w_ref, b_ref, o_ref):
    c = c_ref[...]
    cond = c * _sigmoid(c)
    o_ref[...] = _dot3(cond, w_ref[...]) + b_ref[...]


def _modulation(cond_rows, w_mod, b_mod):
    r, d = cond_rows.shape
    n = w_mod.shape[1]
    tn = 1536
    return pl.pallas_call(
        _mod_kernel,
        out_shape=jax.ShapeDtypeStruct((r, n), F32),
        grid=(n // tn,),
        in_specs=[pl.BlockSpec((r, d), lambda j: (0, 0)),
                  pl.BlockSpec((d, tn), lambda j: (0, j)),
                  pl.BlockSpec((1, tn), lambda j: (0, j))],
        out_specs=pl.BlockSpec((r, tn), lambda j: (0, j)),
        compiler_params=_cparams(("arbitrary",)),
        name="adaln_mod",
    )(cond_rows, w_mod, b_mod.reshape(1, n))


def _group_rms(t, ones_ref, gain_ref, group):
    ms = _dot((t * t).astype(BF16), ones_ref[...]) * (1.0 / group)
    return t * lax.rsqrt(ms + EPS) * gain_ref[...]


def _rope(t, cos_ref, sa_ref, sb_ref):
    up = pltpu.roll(t, HW - 8, 1)
    dn = pltpu.roll(t, 8, 1)
    return t * cos_ref[...] + up * sa_ref[...] + dn * sb_ref[...]


def _proj_kernel(x_ref, mod_ref, g_ref, w_ref, cos_ref, sa_ref, sb_ref, ones32_ref, ones64_ref,
                 gqa_ref, gka_ref, gqn_ref, gkn_ref,
                 qa_ref, kat_ref, va_ref, qn_ref, knt_ref, vn_ref, u_ref):
    x = x_ref[0]
    mod = mod_ref[0, 0]
    ms = jnp.mean(x * x, axis=-1, keepdims=True)
    h = x * lax.rsqrt(ms + EPS) * g_ref[...] * (1.0 + mod[1:2]) + mod[0:1]
    hb = h.astype(BF16)

    def sec(a, b):
        return _dot(hb, w_ref[:, a:b])

    qa = _rope(_group_rms(sec(0, 384), ones32_ref, gqa_ref, A_QK_DIM), cos_ref, sa_ref, sb_ref)
    qa_ref[0] = qa.astype(BF16)
    qn_ref[0] = _group_rms(sec(384, 768), ones64_ref, gqn_ref, N_HEAD_DIM).astype(BF16)
    ka = _rope(_group_rms(sec(768, 1152), ones32_ref, gka_ref, A_QK_DIM), cos_ref, sa_ref, sb_ref)
    kat_ref[0] = ka.T.astype(BF16)
    va_ref[0] = sec(1152, 1536).astype(BF16)
    kn = _group_rms(sec(1536, 1920), ones64_ref, gkn_ref, N_HEAD_DIM)
    knt_ref[0] = kn.T.astype(BF16)
    vn_ref[0] = sec(1920, 2304).astype(BF16)
    u_ref[0] = sec(2304, 2560)


def _in_projection(xs, modall, g_mix, w_in_bf, tables, gains, lc):
    b, s, d = xs.shape
    tm = ROW_TILE
    cos, sa, sb, ones32, ones64 = tables
    row = lambda i, bb: (bb, i, 0)
    tab = lambda i, bb: (i, 0)
    const2 = lambda i, bb: (0, 0)
    act = lambda w, dt: jax.ShapeDtypeStruct((b, s, w), dt)
    act_t = jax.ShapeDtypeStruct((b, HW, s), BF16)
    return pl.pallas_call(
        _proj_kernel,
        out_shape=(act(HW, BF16), act_t, act(HW, BF16), act(HW, BF16), act_t, act(HW, BF16), act(B_WIDTH, F32)),
        grid=(s // tm, b),
        in_specs=[pl.BlockSpec((1, tm, d), row),
                  pl.BlockSpec((1, 1, 6, d), lambda i, bb: (bb, jnp.minimum(i, 1), 0, 0)),
                  pl.BlockSpec((1, d), const2),
                  pl.BlockSpec((d, IN_COLS), const2),
                  pl.BlockSpec((tm, HW), tab), pl.BlockSpec((tm, HW), tab), pl.BlockSpec((tm, HW), tab),
                  pl.BlockSpec((HW, HW), const2), pl.BlockSpec((HW, HW), const2),
                  pl.BlockSpec((1, HW), const2), pl.BlockSpec((1, HW), const2),
                  pl.BlockSpec((1, HW), const2), pl.BlockSpec((1, HW), const2)],
        out_specs=(pl.BlockSpec((1, tm, HW), row),
                   pl.BlockSpec((1, HW, tm), lambda i, bb: (bb, 0, i)),
                   pl.BlockSpec((1, tm, HW), row),
                   pl.BlockSpec((1, tm, HW), row),
                   pl.BlockSpec((1, HW, tm), lambda i, bb: (bb, 0, i)),
                   pl.BlockSpec((1, tm, HW), row),
                   pl.BlockSpec((1, tm, B_WIDTH), row)),
        compiler_params=_cparams(("arbitrary", "arbitrary")),
        name="in_proj",
    )(xs, modall, g_mix.reshape(1, d), w_in_bf, cos, sa, sb, ones32, ones64, *gains)


def _rope_tables(s, lc):
    p = jnp.arange(s)
    pos = jnp.maximum(p - lc, 0)
    rows = (pos // GRID_W).astype(F32)
    cols = (pos % GRID_W).astype(F32)
    lane = jnp.arange(HW)
    j32 = lane % A_QK_DIM
    half = j32 // 16
    i16 = j32 % 16
    nf = 8
    inv = ROPE_THETA ** (-(i16 % nf).astype(F32) / nf)
    coord = jnp.where(half[None, :] == 0, rows[:, None], cols[:, None])
    ang = coord * inv[None, :]
    is_lat = (p >= lc)[:, None]
    second = (i16 >= nf)[None, :]
    cos = jnp.where(is_lat, jnp.cos(ang), 1.0)
    sin = jnp.where(is_lat, jnp.sin(ang), 0.0)
    sa = jnp.where(second, 0.0, -sin)
    sb = jnp.where(second, sin, 0.0)
    return cos.astype(F32), sa.astype(F32), sb.astype(F32)


def _block_ones(group):
    g = jnp.arange(HW) // group
    return (g[:, None] == g[None, :]).astype(BF16)


def _diff_attend(q_all, kt_ref, v_ref, lam, gsub_ref, sk, use_max):
    outs = []
    for h in range(A_HEADS):
        v = v_ref[0, 0:sk, h * A_V_DIM:(h + 1) * A_V_DIM]
        parts = []
        for sub in range(2):
            hs = 2 * h + sub
            off = hs * A_QK_DIM
            sc = _dot(q_all[:, off:off + A_QK_DIM], kt_ref[0, off:off + A_QK_DIM, 0:sk])
            e = jnp.exp2(sc - jnp.max(sc, axis=-1, keepdims=True)) if use_max else jnp.exp2(sc)
            parts.append(_dot(e.astype(BF16), v) * (1.0 / jnp.sum(e, axis=-1, keepdims=True)))
        o = parts[0] - lam * parts[1]
        outs.append(o * lax.rsqrt(jnp.mean(o * o, axis=-1, keepdims=True) + EPS))
    return (jnp.concatenate(outs, axis=-1) * gsub_ref[...]).astype(BF16)


def _diff_attn_kernel(q_ref, kt_ref, v_ref, lam_ref, gsub_ref, o_ref, *, lc, ctx_first, use_max):
    lam = lam_ref[...]
    s = kt_ref.shape[2]
    if ctx_first:
        i = pl.program_id(1)

        @pl.when(i == 0)
        def _():
            o_ref[0] = _diff_attend(q_ref[0], kt_ref, v_ref, lam, gsub_ref, lc, use_max)

        @pl.when(i > 0)
        def _():
            o_ref[0] = _diff_attend(q_ref[0], kt_ref, v_ref, lam, gsub_ref, s, use_max)
    else:
        o_ref[0] = _diff_attend(q_ref[0], kt_ref, v_ref, lam, gsub_ref, s, use_max)


def _diff_attention(qa, kat, va, lam, gsub_t, lc, with_ctx, use_max):
    b, s, _ = qa.shape
    tq = ROW_TILE
    off = 0 if with_ctx else lc // tq
    rows_out = s - off * tq
    return pl.pallas_call(
        functools.partial(_diff_attn_kernel, lc=lc, ctx_first=with_ctx, use_max=use_max),
        out_shape=jax.ShapeDtypeStruct((b, rows_out, HW), BF16),
        grid=(b, rows_out // tq),
        in_specs=[pl.BlockSpec((1, tq, HW), lambda bb, i: (bb, i + off, 0)),
                  pl.BlockSpec((1, HW, s), lambda bb, i: (bb, 0, 0)),
                  pl.BlockSpec((1, s, HW), lambda bb, i: (bb, 0, 0)),
                  pl.BlockSpec((1, 1), lambda bb, i: (0, 0)),
                  pl.BlockSpec((1, HW), lambda bb, i: (0, 0))],
        out_specs=pl.BlockSpec((1, tq, HW), lambda bb, i: (bb, i, 0)),
        compiler_params=_cparams(("arbitrary", "arbitrary")),
        name="diff_attn_rowmax" if use_max else "diff_attn",
    )(qa, kat, va, lam, gsub_t)


def _na_ctx_attend(q_all, kt_ref, v_ref, lc, use_max):
    outs = []
    for h in range(N_HEADS):
        hs = slice(h * N_HEAD_DIM, (h + 1) * N_HEAD_DIM)
        sc = _dot(q_all[:, hs], kt_ref[0, hs, 0:lc])
        e = jnp.exp2(sc - jnp.max(sc, axis=-1, keepdims=True)) if use_max else jnp.exp2(sc)
        o = _dot(e.astype(BF16), v_ref[0, 0:lc, hs])
        outs.append(o * (1.0 / jnp.sum(e, axis=-1, keepdims=True)))
    return jnp.concatenate(outs, axis=-1).astype(BF16)


def _na_attend(q_all, kt_ref, v_ref, bias_ref, koff, lc, use_max):
    nk = NA_KROWS * GRID_W
    outs = []
    for h in range(N_HEADS):
        hs = slice(h * N_HEAD_DIM, (h + 1) * N_HEAD_DIM)
        q = q_all[:, hs]
        s_loc = _dot(q, kt_ref[0, hs, pl.ds(koff, nk)]) + bias_ref[0, h]
        s_ctx = _dot(q, kt_ref[0, hs, 0:lc])
        if use_max:
            m = jnp.maximum(jnp.max(s_loc, axis=-1, keepdims=True), jnp.max(s_ctx, axis=-1, keepdims=True))
            s_loc, s_ctx = s_loc - m, s_ctx - m
        e_loc = jnp.exp2(s_loc)
        e_ctx = jnp.exp2(s_ctx)
        den = jnp.sum(e_loc, axis=-1, keepdims=True) + jnp.sum(e_ctx, axis=-1, keepdims=True)
        o = _dot(e_loc.astype(BF16), v_ref[0, pl.ds(koff, nk), hs]) + _dot(e_ctx.astype(BF16), v_ref[0, 0:lc, hs])
        outs.append(o * (1.0 / den))
    return jnp.concatenate(outs, axis=-1).astype(BF16)


def _na_kernel(q_ref, kt_ref, v_ref, bias_ref, o_ref, *, lc, n_img_rows, ctx_first, use_max):
    i = pl.program_id(1)
    blk = i - 1 if ctx_first else i
    start_row = jnp.clip(NA_ROWS * blk - WIN_ROWS // 2, 0, n_img_rows - NA_KROWS)
    koff = pl.multiple_of(lc + start_row * GRID_W, 128)
    if ctx_first:
        @pl.when(i == 0)
        def _():
            o_ref[0] = _na_ctx_attend(q_ref[0], kt_ref, v_ref, lc, use_max)

        @pl.when(i > 0)
        def _():
            o_ref[0] = _na_attend(q_ref[0], kt_ref, v_ref, bias_ref, koff, lc, use_max)
    else:
        o_ref[0] = _na_attend(q_ref[0], kt_ref, v_ref, bias_ref, koff, lc, use_max)


def _na_attention(qn, knt, vn, bias, lc, with_ctx, use_max):
    b, s, _ = qn.shape
    tq = NA_ROWS * GRID_W
    assert tq == ROW_TILE and lc % tq == 0
    n_img_rows = (s - lc) // GRID_W
    nblk = n_img_rows // NA_ROWS
    off = 0 if with_ctx else lc // tq
    rows_out = s - off * tq
    first = 1 if with_ctx else 0

    def variant(bb, i):
        blk = i - first
        return (jnp.where(blk <= 0, 0, jnp.where(blk == nblk - 1, 2, 1)), 0, 0, 0)

    return pl.pallas_call(
        functools.partial(_na_kernel, lc=lc, n_img_rows=n_img_rows, ctx_first=with_ctx, use_max=use_max),
        out_shape=jax.ShapeDtypeStruct((b, rows_out, HW), BF16),
        grid=(b, rows_out // tq),
        in_specs=[pl.BlockSpec((1, tq, HW), lambda bb, i: (bb, i + off, 0)),
                  pl.BlockSpec((1, HW, s), lambda bb, i: (bb, 0, 0)),
                  pl.BlockSpec((1, s, HW), lambda bb, i: (bb, 0, 0)),
                  pl.BlockSpec((1, N_HEADS, tq, NA_KROWS * GRID_W), variant)],
        out_specs=pl.BlockSpec((1, tq, HW), lambda bb, i: (bb, i, 0)),
        compiler_params=_cparams(("arbitrary", "arbitrary")),
        name="nbr_attn_rowmax" if use_max else "nbr_attn",
    )(qn, knt, vn, bias)


def _na_bias(rpb, n_img_rows):
    a = jnp.arange(NA_ROWS)[:, None, None, None]
    cq = jnp.arange(GRID_W)[None, :, None, None]
    j = jnp.arange(NA_KROWS)[None, None, :, None]
    ck = jnp.arange(GRID_W)[None, None, None, :]
    cstart = jnp.clip(cq - WIN_COLS // 2, 0, GRID_W - WIN_COLS)
    colmask = (ck >= cstart) & (ck < cstart + WIN_COLS)
    dc = jnp.clip(ck - cq, -(WIN_COLS - 1), WIN_COLS - 1) + (WIN_COLS - 1)
    by_col = jnp.take(rpb.astype(F32), dc.reshape(-1), axis=2).reshape(N_HEADS, 2 * WIN_ROWS - 1, GRID_W, GRID_W)
    out = []
    for r0_minus_k, wstart in ((0, 0 * a), (WIN_ROWS // 2, a), (NA_KROWS - NA_ROWS, NA_KROWS - WIN_ROWS + 0 * a)):
        inwin = (j >= wstart) & (j < wstart + WIN_ROWS)
        dr = jnp.clip(j - r0_minus_k - a + (WIN_ROWS - 1), 0, 2 * WIN_ROWS - 2)
        vals = jnp.take(by_col, dr.reshape(-1), axis=1).reshape(N_HEADS, NA_ROWS, NA_KROWS, GRID_W, GRID_W)
        vals = jnp.transpose(vals, (0, 1, 3, 2, 4))
        vals = jnp.where((inwin & colmask)[None], vals * LOG2E, NEG)
        out.append(vals.reshape(N_HEADS, NA_ROWS * GRID_W, NA_KROWS * GRID_W))
    return jnp.stack(out)


def _s5_kernel(uf_ref, ub_ref, bmf_ref, cmf_ref, arf_ref, aif_ref, bmb_ref, cmb_ref, arb_ref, aib_ref,
               yf_ref, yb_ref, xf_scr, xb_scr, st_scr):
    tc, nb, w = uf_ref.shape
    ns = arf_ref.shape[1]

    @pl.when(pl.program_id(0) == 0)
    def _():
        st_scr[...] = jnp.zeros_like(st_scr)

    xf_scr[...] = _dot(uf_ref[...].reshape(tc * nb, w).astype(BF16), bmf_ref[...])
    xb_scr[...] = _dot(ub_ref[...].reshape(tc * nb, w).astype(BF16), bmb_ref[...])

    def advance(x_scr, a_re, a_im, s_re, s_im, tt):
        rows = pl.ds(pl.multiple_of(tt * nb, nb), nb)
        n_re = a_re * s_re - a_im * s_im + x_scr[rows, 0:ns]
        n_im = a_re * s_im + a_im * s_re + x_scr[rows, ns:2 * ns]
        x_scr[rows, 0:ns] = n_re
        x_scr[rows, ns:2 * ns] = n_im
        return n_re, n_im

    def step(t, carry):
        f_re, f_im, b_re, b_im = carry
        f_re, f_im = advance(xf_scr, arf_ref[...], aif_ref[...], f_re, f_im, t)
        b_re, b_im = advance(xb_scr, arb_ref[...], aib_ref[...], b_re, b_im, tc - 1 - t)
        return f_re, f_im, b_re, b_im

    init = (st_scr[0, :, 0:ns], st_scr[0, :, ns:2 * ns], st_scr[1, :, 0:ns], st_scr[1, :, ns:2 * ns])
    f_re, f_im, b_re, b_im = lax.fori_loop(0, tc, step, init, unroll=2)
    st_scr[0, :, 0:ns] = f_re
    st_scr[0, :, ns:2 * ns] = f_im
    st_scr[1, :, 0:ns] = b_re
    st_scr[1, :, ns:2 * ns] = b_im
    yf_ref[...] = _dot(xf_scr[...].astype(BF16), cmf_ref[...]).reshape(tc, nb, w)
    yb_ref[...] = _dot(xb_scr[...].astype(BF16), cmb_ref[...]).reshape(tc, nb, w)


def _s5_scan(u_tm, fwd, bwd, lc):
    s, nb, w = u_tm.shape
    tc = S5_CHUNK
    nc, ncc = s // tc, lc // tc
    ns = fwd[2].shape[1]
    chunk_f = lambda j: (j, 0, 0)
    chunk_b = lambda j: (jnp.where(j < ncc, ncc - 1 - j, nc - 1 - (j - ncc)), 0, 0)
    const = lambda j: (0, 0)
    pspecs = [pl.BlockSpec((w, 2 * ns), const), pl.BlockSpec((2 * ns, w), const),
              pl.BlockSpec((nb, ns), const), pl.BlockSpec((nb, ns), const)]
    out = jax.ShapeDtypeStruct((s, nb, w), F32)
    return pl.pallas_call(
        _s5_kernel,
        out_shape=(out, out),
        grid=(nc,),
        in_specs=[pl.BlockSpec((tc, nb, w), chunk_f), pl.BlockSpec((tc, nb, w), chunk_b)] + pspecs + pspecs,
        out_specs=(pl.BlockSpec((tc, nb, w), chunk_f), pl.BlockSpec((tc, nb, w), chunk_b)),
        scratch_shapes=[pltpu.VMEM((tc * nb, 2 * ns), F32), pltpu.VMEM((tc * nb, 2 * ns), F32),
                        pltpu.VMEM((2, nb, 2 * ns), F32)],
        compiler_params=_cparams(("arbitrary",)),
        name="s5_scan",
    )(u_tm, u_tm, *fwd, *bwd)


def _s5_params(a_re, a_im, log_dt, b_re, b_im, c_re, c_im, nb):
    g, n, p = S5_GROUPS, S5_STATE, S5_GROUP_CH
    lr, li = a_re.astype(F32), a_im.astype(F32)
    dt = jnp.exp(log_dt.astype(F32))[..., None]
    mag = jnp.exp(lr * dt)
    ab_r, ab_i = mag * jnp.cos(li * dt), mag * jnp.sin(li * dt)
    den = lr * lr + li * li
    cf_r = ((ab_r - 1.0) * lr + ab_i * li) / den
    cf_i = (ab_i * lr - (ab_r - 1.0) * li) / den
    br, bi = b_re.astype(F32), b_im.astype(F32)
    bb_r = cf_r[..., None] * br - cf_i[..., None] * bi
    bb_i = cf_r[..., None] * bi + cf_i[..., None] * br
    eye = jnp.eye(g, dtype=F32)
    out = []
    for k in range(2):
        b_r = jnp.einsum('gnp,gh->gphn', bb_r[k], eye).reshape(g * p, g * n)
        b_i = jnp.einsum('gnp,gh->gphn', bb_i[k], eye).reshape(g * p, g * n)
        bmat = jnp.concatenate([b_r, b_i], axis=1).astype(BF16)
        ct = jnp.transpose(c_re[k].astype(F32), (0, 2, 1))
        ci = jnp.transpose(c_im[k].astype(F32), (0, 2, 1))
        c_r = jnp.einsum('gnp,gh->gnhp', ct, eye).reshape(g * n, g * p)
        c_i = jnp.einsum('gnp,gh->gnhp', ci, eye).reshape(g * n, g * p)
        cmat = jnp.concatenate([c_r, -c_i], axis=0).astype(BF16)
        ar = jnp.broadcast_to(ab_r[k].reshape(1, g * n), (nb, g * n))
        ai = jnp.broadcast_to(ab_i[k].reshape(1, g * n), (nb, g * n))
        out.append((bmat, cmat, ar, ai))
    return out


def _out_kernel(oa_ref, on_ref, y_ref, u_ref, x_ref, mod_ref, gffn_ref, wo_ref, dskip_ref, wglu_ref, bglu_ref,
                wr_ref, xo_ref, h2_ref, lg_ref):
    mod = mod_ref[0, 0]
    g = _gelu_tanh(y_ref[0] + dskip_ref[...] * u_ref[0])
    ob = g * _sigmoid(_dot(g.astype(BF16), wglu_ref[...]) + bglu_ref[...])
    mix = (_dot(oa_ref[0], wo_ref[0:HW, :]) + _dot(ob.astype(BF16), wo_ref[HW:HW + B_WIDTH, :])
           + _dot(on_ref[0], wo_ref[HW + B_WIDTH:, :]))
    x = x_ref[0] + mod[2:3] * mix
    xo_ref[0] = x
    ms = jnp.mean(x * x, axis=-1, keepdims=True)
    h2 = x * lax.rsqrt(ms + EPS) * gffn_ref[...] * (1.0 + mod[4:5]) + mod[3:4]
    h2_ref[0] = h2.astype(BF16)
    nt = lambda a, bb: lax.dot_general(a, bb, (((1,), (1,)), ((), ())), preferred_element_type=F32)
    wh, wl = _split_bf16(wr_ref[...])
    hh, hl = _split_bf16(h2)
    lg_ref[0] = nt(wh, hh) + nt(wh, hl) + nt(wl, hh)


def _out_projection(oa, on, y, u, xs, modall, g_ffn, w_out_bf, d_skip, w_glu_bf, b_glu, w_router_pad, lc, with_ctx):
    b, s, d = xs.shape
    tm = ROW_TILE
    off = 0 if with_ctx else lc // tm
    rows_out = s - off * tm
    full = lambda bb, i: (bb, i + off, 0)
    outr = lambda bb, i: (bb, i, 0)
    const = lambda bb, i: (0, 0)
    ne = w_router_pad.shape[0]
    return pl.pallas_call(
        _out_kernel,
        out_shape=(jax.ShapeDtypeStruct((b, rows_out, d), F32),
                   jax.ShapeDtypeStruct((b, rows_out, d), BF16),
                   jax.ShapeDtypeStruct((b, ne, rows_out), F32)),
        grid=(b, rows_out // tm),
        in_specs=[pl.BlockSpec((1, tm, HW), outr),
                  pl.BlockSpec((1, tm, HW), outr),
                  pl.BlockSpec((1, tm, B_WIDTH), full),
                  pl.BlockSpec((1, tm, B_WIDTH), full),
                  pl.BlockSpec((1, tm, d), full),
                  pl.BlockSpec((1, 1, 6, d), lambda bb, i: (bb, jnp.minimum(i + off, 1), 0, 0)),
                  pl.BlockSpec((1, d), const),
                  pl.BlockSpec((d, d), const),
                  pl.BlockSpec((1, B_WIDTH), const),
                  pl.BlockSpec((B_WIDTH, B_WIDTH), const),
                  pl.BlockSpec((1, B_WIDTH), const),
                  pl.BlockSpec((ne, d), const)],
        out_specs=(pl.BlockSpec((1, tm, d), outr), pl.BlockSpec((1, tm, d), outr),
                   pl.BlockSpec((1, ne, tm), lambda bb, i: (bb, 0, i))),
        compiler_params=_cparams(("arbitrary", "arbitrary")),
        name="out_proj",
    )(oa, on, y, u, xs, modall, g_ffn.reshape(1, d), w_out_bf, d_skip.reshape(1, -1), w_glu_bf,
      b_glu.reshape(1, -1), w_router_pad)


def _swiglu(h, wg, wu):
    a = _dot(h, wg)
    return a * _sigmoid(a) * _dot(h, wu)


def _moe_kernel(h_ref, gates_ref, x_ref, gt_ref, wg_ref, wu_ref, wd_ref, sg_ref, su_ref, sd_ref, o_ref, acc_ref):
    e = pl.program_id(1)
    n_routed = pl.num_programs(1) - 1

    @pl.when(e == 0)
    def _():
        acc_ref[...] = jnp.zeros_like(acc_ref)

    @pl.when(e < n_routed)
    def _():
        h = h_ref[...]
        src = lax.broadcasted_iota(jnp.int32, (128, 128), 0)
        dst = lax.broadcasted_iota(jnp.int32, (128, 128), 1)
        sel = jnp.where(src == e * MOE_EB + dst, jnp.where(dst < MOE_EB, 1.0, 0.0), 0.0).astype(BF16)
        g_hi, g_lo = _split_bf16(gates_ref[...])
        gsel = _dot(g_hi, sel) + _dot(g_lo, sel)
        hid = [(_swiglu(h, wg_ref[j], wu_ref[j]) * gsel[:, j:j + 1]).astype(BF16) for j in range(MOE_EB)]
        acc_ref[...] += _dot(jnp.concatenate(hid, axis=-1), wd_ref[...])

    @pl.when(e == n_routed)
    def _():
        for j in range(h_ref.shape[0] // ROW_TILE):
            rows = slice(j * ROW_TILE, (j + 1) * ROW_TILE)
            hs = _swiglu(h_ref[rows, :], sg_ref[...], su_ref[...]).astype(BF16)
            y = acc_ref[rows, :] + _dot(hs, sd_ref[...])
            o_ref[rows, :] = x_ref[rows, :] + gt_ref[0, j:j + 1, :] * y


def _moe_tile(t):
    return max(m for m in range(ROW_TILE, MOE_TILE + 1, ROW_TILE) if t % m == 0)


def _moe(h2, gates, x, gt_rows, wg, wu, wd, sg, su, sd):
    t, d = h2.shape
    tm = _moe_tile(t)
    n_routed = wg.shape[0] // MOE_EB
    nsub = tm // ROW_TILE
    step = lambda e: jnp.minimum(e, n_routed - 1)
    const = lambda i, e: (0, 0)
    return pl.pallas_call(
        _moe_kernel,
        out_shape=jax.ShapeDtypeStruct((t, d), F32),
        grid=(t // tm, n_routed + 1),
        in_specs=[pl.BlockSpec((tm, d), lambda i, e: (i, 0)),
                  pl.BlockSpec((tm, 128), lambda i, e: (i, 0)),
                  pl.BlockSpec((tm, d), lambda i, e: (i, 0)),
                  pl.BlockSpec((1, nsub, d), lambda i, e: (i, 0, 0)),
                  pl.BlockSpec((MOE_EB, d, EXPERT_DIM), lambda i, e: (step(e), 0, 0)),
                  pl.BlockSpec((MOE_EB, d, EXPERT_DIM), lambda i, e: (step(e), 0, 0)),
                  pl.BlockSpec((MOE_EB * EXPERT_DIM, d), lambda i, e: (step(e), 0)),
                  pl.BlockSpec((d, EXPERT_DIM), const),
                  pl.BlockSpec((d, EXPERT_DIM), const),
                  pl.BlockSpec((EXPERT_DIM, d), const)],
        out_specs=pl.BlockSpec((tm, d), lambda i, e: (i, 0)),
        scratch_shapes=[pltpu.VMEM((tm, d), F32)],
        compiler_params=_cparams(("arbitrary", "arbitrary")),
        name="moe_ffn",
    )(h2, gates, x, gt_rows, wg, wu, wd, sg, su, sd)


def _route_kernel(lg_ref, bias_ref, mask_ref, gates_ref, cnt_ref):
    gsz = N_EXPERTS // N_GROUPS
    tn = lg_ref.shape[2]
    ninf = -jnp.inf
    jidx = lax.broadcasted_iota(jnp.int32, (gsz, tn), 0)
    scores, biased, gscore = [], [], []
    for g in range(N_GROUPS):
        rows = slice(g * gsz, (g + 1) * gsz)
        sc = _sigmoid(lg_ref[0, rows, :])
        bi = sc + bias_ref[rows, :]
        m1 = jnp.max(bi, axis=0, keepdims=True)
        first = jnp.min(jnp.where(bi == m1, jidx, gsz), axis=0, keepdims=True)
        m2 = jnp.max(jnp.where(jidx == first, ninf, bi), axis=0, keepdims=True)
        scores.append(sc)
        biased.append(bi)
        gscore.append(m1 + m2)
    masked = []
    for g in range(N_GROUPS):
        rank = jnp.zeros((1, tn), F32)
        for g2 in range(N_GROUPS):
            if g2 != g:
                ahead = (gscore[g2] >= gscore[g]) if g2 < g else (gscore[g2] > gscore[g])
                rank = rank + jnp.where(ahead, 1.0, 0.0)
        keep = jnp.broadcast_to(rank, (gsz, tn)) < TOPK_GROUPS
        masked.append(jnp.where(keep, biased[g], ninf))
    ranks = [jnp.zeros((gsz, tn), F32) for _ in range(N_GROUPS)]
    for g2 in range(N_GROUPS):
        for j2 in range(gsz):
            other = jnp.broadcast_to(masked[g2][j2:j2 + 1, :], (gsz, tn))
            for g in range(N_GROUPS):
                ge = jnp.where(other >= masked[g], 1.0, 0.0)
                gt = jnp.where(other > masked[g], 1.0, 0.0)
                if g2 < g:
                    ahead = ge
                elif g2 > g:
                    ahead = gt
                else:
                    ahead = jnp.where(jidx > j2, ge, gt)
                ranks[g] = ranks[g] + ahead
    chosen = [jnp.where(ranks[g] < TOP_K, 1.0, 0.0) for g in range(N_GROUPS)]
    picked = [chosen[g] * scores[g] for g in range(N_GROUPS)]
    den = sum(jnp.sum(p, axis=0, keepdims=True) for p in picked)
    scale = ROUTED_SCALE / den
    mask_t = jnp.concatenate(chosen, axis=0)
    mask_ref[...] = mask_t
    gates_ref[...] = jnp.concatenate([p * scale for p in picked], axis=0)

    @pl.when((pl.program_id(0) == 0) & (pl.program_id(1) == 0))
    def _():
        cnt_ref[...] = jnp.zeros_like(cnt_ref)

    cnt_ref[...] += jnp.broadcast_to(jnp.sum(mask_t, axis=1, keepdims=True), cnt_ref.shape)


def _route(logits_t, e_bias):
    b, ne, rows = logits_t.shape
    tn = ROW_TILE
    nt = rows // tn
    t = b * rows
    bias = jnp.broadcast_to(e_bias.astype(F32)[:, None], (N_EXPERTS, tn))
    col = lambda bb, i: (0, bb * nt + i)
    return pl.pallas_call(
        _route_kernel,
        out_shape=(jax.ShapeDtypeStruct((N_EXPERTS, t), F32), jax.ShapeDtypeStruct((N_EXPERTS, t), F32),
                   jax.ShapeDtypeStruct((N_EXPERTS, 128), F32)),
        grid=(b, nt),
        in_specs=[pl.BlockSpec((1, ne, tn), lambda bb, i: (bb, 0, i)),
                  pl.BlockSpec((N_EXPERTS, tn), lambda bb, i: (0, 0))],
        out_specs=(pl.BlockSpec((N_EXPERTS, tn), col), pl.BlockSpec((N_EXPERTS, tn), col),
                   pl.BlockSpec((N_EXPERTS, 128), lambda bb, i: (0, 0))),
        compiler_params=_cparams(("arbitrary", "arbitrary")),
        name="moe_route",
    )(logits_t, bias)


def _pos_kernel(mask_ref, gates_ref, off_ref, pos_ref, gtm_ref, base_scr):
    tn = mask_ref.shape[1]

    @pl.when(pl.program_id(0) == 0)
    def _():
        base_scr[...] = jnp.zeros_like(base_scr)

    m = mask_ref[...]
    mb = m.astype(BF16)
    upper = jnp.where(lax.broadcasted_iota(jnp.int32, (tn, tn), 0) <= lax.broadcasted_iota(jnp.int32, (tn, tn), 1),
                      1.0, 0.0).astype(BF16)
    incl = _dot(mb, upper)
    posf = off_ref[:, 0:1] + base_scr[:, 0:1] + incl - 1.0
    base_scr[...] += jnp.broadcast_to(incl[:, tn - 1:tn], base_scr.shape)
    lower = jnp.where(lax.broadcasted_iota(jnp.int32, (N_EXPERTS, N_EXPERTS), 1)
                      < lax.broadcasted_iota(jnp.int32, (N_EXPERTS, N_EXPERTS), 0), 1.0, 0.0).astype(BF16)
    slot = _dot(lower, mb)
    g = gates_ref[...]
    pos_rows, gate_rows = [], []
    for k in range(TOP_K):
        sel = jnp.where(slot == k, m, 0.0)
        pos_rows.append(jnp.sum(posf * sel, axis=0, keepdims=True))
        gate_rows.append(jnp.sum(g * sel, axis=0, keepdims=True))
    pos_ref[0] = jnp.concatenate(pos_rows, axis=0).astype(jnp.int32)
    gate_t = jnp.concatenate(gate_rows + [jnp.zeros((128 - TOP_K, tn), F32)], axis=0)
    gtm_ref[...] = gate_t.T


def _positions(mask_t, gates_t, offsets):
    ne, t = mask_t.shape
    tn = ROW_TILE
    nt = t // tn
    off = jnp.broadcast_to(offsets.astype(F32)[:, None], (ne, 128))
    pos, gtm = pl.pallas_call(
        _pos_kernel,
        out_shape=(jax.ShapeDtypeStruct((nt, TOP_K, tn), jnp.int32), jax.ShapeDtypeStruct((t, 128), F32)),
        grid=(nt,),
        in_specs=[pl.BlockSpec((ne, tn), lambda i: (0, i)), pl.BlockSpec((ne, tn), lambda i: (0, i)),
                  pl.BlockSpec((ne, 128), lambda i: (0, 0))],
        out_specs=(pl.BlockSpec((1, TOP_K, tn), lambda i: (i, 0, 0)), pl.BlockSpec((tn, 128), lambda i: (i, 0))),
        scratch_shapes=[pltpu.VMEM((ne, 128), F32)],
        compiler_params=_cparams(("arbitrary",)),
        name="moe_positions",
    )(mask_t, gates_t, off)
    return pos.reshape(nt, TOP_K * tn), gtm


def _row_copies(n, make):
    def issue(jj, c):
        for p in range(2):
            make(jj * 2 + p).start(priority=p)
        return c

    lax.fori_loop(0, n // 2, issue, 0, unroll=4)

    def drain(j, c):
        make(0).wait()
        return c

    lax.fori_loop(0, n, drain, 0, unroll=8)


def _dispatch_kernel(pad_start, pad_cnt, pos_hbm, h_ref, hs_hbm, pos_smem, rows_scr, zero_scr, sem_idx, sem_rows):
    i = pl.program_id(0)
    tok = h_ref.shape[0]
    idx_copy = pltpu.make_async_copy(pos_hbm.at[i], pos_smem, sem_idx)
    idx_copy.start()
    for j in range(SUBROWS):
        rows_scr[pl.ds(j, tok, stride=SUBROWS), :] = h_ref[:, j * 128:(j + 1) * 128].astype(F32)

    @pl.when(i == 0)
    def _():
        zero_scr[...] = jnp.zeros_like(zero_scr)

        def per_expert(e, c):
            def zero_row(r):
                dst = pl.multiple_of((pad_start[e] + r) * SUBROWS, SUBROWS)
                return pltpu.make_async_copy(zero_scr, hs_hbm.at[pl.ds(dst, SUBROWS), :], sem_rows)

            lax.fori_loop(0, pad_cnt[e], lambda r, cc: (zero_row(r).start(), cc)[1], 0)
            lax.fori_loop(0, pad_cnt[e], lambda r, cc: (zero_row(0).wait(), cc)[1], 0)
            return c

        lax.fori_loop(0, N_EXPERTS, per_expert, 0)

    idx_copy.wait()

    def row_copy(j):
        src = pl.multiple_of(lax.bitwise_and(j, tok - 1) * SUBROWS, SUBROWS)
        dst = pl.multiple_of(pos_smem[j] * SUBROWS, SUBROWS)
        return pltpu.make_async_copy(rows_scr.at[pl.ds(src, SUBROWS), :], hs_hbm.at[pl.ds(dst, SUBROWS), :], sem_rows)

    _row_copies(pos_smem.shape[0], row_copy)


def _dispatch(h2, pos, pad_start, pad_cnt, n_rows):
    t, d = h2.shape
    tok = ROW_TILE
    assert d == SUBROWS * 128 and tok & (tok - 1) == 0
    return pl.pallas_call(
        _dispatch_kernel,
        out_shape=jax.ShapeDtypeStruct((n_rows * SUBROWS, 128), F32),
        grid_spec=pltpu.PrefetchScalarGridSpec(
            num_scalar_prefetch=2,
            grid=(t // tok,),
            in_specs=[pl.BlockSpec(memory_space=pl.ANY),
                      pl.BlockSpec((tok, d), lambda i, ps, pc: (i, 0))],
            out_specs=pl.BlockSpec(memory_space=pl.ANY),
            scratch_shapes=[pltpu.SMEM((pos.shape[1],), jnp.int32), pltpu.VMEM((tok * SUBROWS, 128), F32),
                            pltpu.VMEM((SUBROWS, 128), F32), pltpu.SemaphoreType.DMA, pltpu.SemaphoreType.DMA]),
        compiler_params=_cparams(("arbitrary",)),
        name="moe_dispatch",
    )(pad_start, pad_cnt, pos, h2)


def _expert_kernel(tile_expert, n_used, hs_ref, wg_ref, wu_ref, wd_ref, os_ref):
    i = pl.program_id(0)
    rows = hs_ref.shape[0] // SUBROWS

    @pl.when(i < n_used[0])
    def _():
        x = jnp.concatenate([hs_ref[pl.ds(j, rows, stride=SUBROWS), :] for j in range(SUBROWS)], axis=1).astype(BF16)
        o = _dot(_swiglu(x, wg_ref[0], wu_ref[0]).astype(BF16), wd_ref[0])
        for j in range(SUBROWS):
            os_ref[pl.ds(j, rows, stride=SUBROWS), :] = o[:, j * 128:(j + 1) * 128]

    @pl.when(i >= n_used[0])
    def _():
        os_ref[...] = jnp.zeros_like(os_ref)


def _expert_ffn(hs, tile_expert, n_used, wg, wu, wd):
    n_tiles = hs.shape[0] // (ROW_TILE * SUBROWS)
    d, f = wg.shape[1], wg.shape[2]
    blk = (ROW_TILE * SUBROWS, 128)
    return pl.pallas_call(
        _expert_kernel,
        out_shape=jax.ShapeDtypeStruct(hs.shape, F32),
        grid_spec=pltpu.PrefetchScalarGridSpec(
            num_scalar_prefetch=2,
            grid=(n_tiles,),
            in_specs=[pl.BlockSpec(blk, lambda i, te, nu: (jnp.minimum(i, nu[0] - 1), 0)),
                      pl.BlockSpec((1, d, f), lambda i, te, nu: (te[i], 0, 0)),
                      pl.BlockSpec((1, d, f), lambda i, te, nu: (te[i], 0, 0)),
                      pl.BlockSpec((1, f, d), lambda i, te, nu: (te[i], 0, 0))],
            out_specs=pl.BlockSpec(blk, lambda i, te, nu: (i, 0))),
        compiler_params=_cparams(("arbitrary",)),
        name="moe_experts",
    )(tile_expert, n_used, hs, wg, wu, wd)


def _combine_kernel(pos_hbm, os_hbm, gtm_ref, h_ref, x_ref, gt_ref, sg_ref, su_ref, sd_ref, o_ref,
                    pos_smem, buf, sem_idx, sem_rows):
    i = pl.program_id(0)
    tok = h_ref.shape[0]
    idx_copy = pltpu.make_async_copy(pos_hbm.at[i], pos_smem, sem_idx)
    idx_copy.start()
    idx_copy.wait()

    def row_copy(j):
        src = pl.multiple_of(pos_smem[j] * SUBROWS, SUBROWS)
        dst = pl.multiple_of(j * SUBROWS, SUBROWS)
        return pltpu.make_async_copy(os_hbm.at[pl.ds(src, SUBROWS), :], buf.at[pl.ds(dst, SUBROWS), :], sem_rows)

    _row_copies(pos_smem.shape[0], row_copy)
    gates = gtm_ref[...]
    y = _dot(_swiglu(h_ref[...], sg_ref[...], su_ref[...]).astype(BF16), sd_ref[...])
    for k in range(TOP_K):
        part = jnp.concatenate([buf[pl.ds(k * tok * SUBROWS + j, tok, stride=SUBROWS), :] for j in range(SUBROWS)],
                               axis=1)
        y = y + gates[:, k:k + 1] * part
    o_ref[...] = x_ref[...] + gt_ref[0] * y


def _combine(pos, os, gtm, h2, x, gt_tiles, sg, su, sd):
    t, d = h2.shape
    tok = ROW_TILE
    f = sg.shape[1]
    const = lambda i: (0, 0)
    return pl.pallas_call(
        _combine_kernel,
        out_shape=jax.ShapeDtypeStruct((t, d), F32),
        grid=(t // tok,),
        in_specs=[pl.BlockSpec(memory_space=pl.ANY), pl.BlockSpec(memory_space=pl.ANY),
                  pl.BlockSpec((tok, 128), lambda i: (i, 0)),
                  pl.BlockSpec((tok, d), lambda i: (i, 0)),
                  pl.BlockSpec((tok, d), lambda i: (i, 0)),
                  pl.BlockSpec((1, 1, d), lambda i: (i, 0, 0)),
                  pl.BlockSpec((d, f), const), pl.BlockSpec((d, f), const), pl.BlockSpec((f, d), const)],
        out_specs=pl.BlockSpec((tok, d), lambda i: (i, 0)),
        scratch_shapes=[pltpu.SMEM((pos.shape[1],), jnp.int32), pltpu.VMEM((TOP_K * tok * SUBROWS, 128), F32),
                        pltpu.SemaphoreType.DMA, pltpu.SemaphoreType.DMA],
        compiler_params=_cparams(("arbitrary",)),
        name="moe_combine",
    )(pos, os, gtm, h2, x, gt_tiles, sg, su, sd)


def _expert_layout(counts, n_tiles):
    padded = ((counts + ROW_TILE - 1) // ROW_TILE) * ROW_TILE
    ends = jnp.cumsum(padded)
    offsets = ends - padded
    tile_expert = jnp.minimum(
        jnp.sum((jnp.arange(n_tiles, dtype=jnp.int32)[:, None] * ROW_TILE >= ends[None, :]).astype(jnp.int32), axis=1),
        N_EXPERTS - 1).astype(jnp.int32)
    n_used = (ends[-1:] // ROW_TILE).astype(jnp.int32)
    return offsets, offsets + counts, padded - counts, tile_expert, n_used


def kernel(x, c, ctx, c_ctx, w_mod, b_mod, g_mix, g_ffn, w_in, w_out, a_gq, a_gk, a_lambda, a_gsub, n_gq, n_gk, n_rpb, s5_a_re, s5_a_im, s5_log_dt, s5_b_re, s5_b_im, s5_c_re, s5_c_im, s5_d, s5_w_glu, s5_b_glu, w_router, e_bias, w_gate, w_up, w_down, ws_gate, ws_up, ws_down):
    b, l, d = x.shape
    lc = ctx.shape[1]
    s = lc + l
    depth = w_mod.shape[0]
    n_img_rows = l // GRID_W

    xs = jnp.concatenate([ctx, x], axis=1).astype(F32)
    cond_rows = jnp.zeros((16, d), F32).at[:b].set(c.astype(F32)).at[b].set(c_ctx.astype(F32))
    cos, sa, sb = _rope_tables(s, lc)
    tables = (cos, sa, sb, _block_ones(A_QK_DIM), _block_ones(N_HEAD_DIM))

    for layer in range(depth):
        last = layer == depth - 1
        with_ctx = not last
        lam_init = 0.8 - 0.6 * math.exp(-0.3 * layer)

        mod = _modulation(cond_rows, w_mod[layer].astype(F32), b_mod[layer].astype(F32))
        mod_lat = mod[:b].reshape(b, 1, 6, d)
        mod_ctx = jnp.broadcast_to(mod[b].reshape(1, 1, 6, d), (b, 1, 6, d))
        modall = jnp.concatenate([mod_ctx, mod_lat], axis=1)

        gains = ((jnp.tile(a_gq[layer].astype(F32), HW // A_QK_DIM) * (A_QK_DIM ** -0.5 * LOG2E)).reshape(1, HW),
                 jnp.tile(a_gk[layer].astype(F32), HW // A_QK_DIM).reshape(1, HW),
                 (jnp.tile(n_gq[layer].astype(F32), N_HEADS) * (N_HEAD_DIM ** -0.5 * LOG2E)).reshape(1, HW),
                 jnp.tile(n_gk[layer].astype(F32), N_HEADS).reshape(1, HW))
        qa, kat, va, qn, knt, vn, u = _in_projection(xs, modall, g_mix[layer].astype(F32), w_in[layer].astype(BF16),
                                                     tables, gains, lc)

        lv = a_lambda[layer].astype(F32)
        lam = (jnp.exp(jnp.sum(lv[0] * lv[1])) - jnp.exp(jnp.sum(lv[2] * lv[3])) + lam_init).reshape(1, 1)
        gsub_t = (jnp.tile(a_gsub[layer].astype(F32), A_HEADS) * (1.0 - lam_init)).reshape(1, HW)
        a_bound = BOUND_SLACK * A_QK_DIM * jnp.max(jnp.abs(gains[0])) * jnp.max(jnp.abs(gains[1]))
        oa = lax.cond(a_bound <= SCORE_BOUND_LIMIT,
                      lambda *a: _diff_attention(*a, lc, with_ctx, False),
                      lambda *a: _diff_attention(*a, lc, with_ctx, True), qa, kat, va, lam, gsub_t)

        n_bound = (BOUND_SLACK * N_HEAD_DIM * jnp.max(jnp.abs(gains[2])) * jnp.max(jnp.abs(gains[3]))
                   + LOG2E * jnp.max(jnp.abs(n_rpb[layer].astype(F32))))
        on = lax.cond(n_bound <= SCORE_BOUND_LIMIT,
                      lambda *a: _na_attention(*a, lc, with_ctx, False),
                      lambda *a: _na_attention(*a, lc, with_ctx, True),
                      qn, knt, vn, _na_bias(n_rpb[layer], n_img_rows))

        u_tm = jnp.transpose(u, (1, 0, 2))
        s5_fwd, s5_bwd = _s5_params(s5_a_re[layer], s5_a_im[layer], s5_log_dt[layer], s5_b_re[layer],
                                    s5_b_im[layer], s5_c_re[layer], s5_c_im[layer], b)
        y_f, y_b = _s5_scan(u_tm, s5_fwd, s5_bwd, lc)
        y = jnp.transpose(y_f + y_b, (1, 0, 2))

        w_router_pad = jnp.zeros((128, d), F32).at[:N_EXPERTS].set(w_router[layer].astype(F32).T)
        x_new, h2, logits_t = _out_projection(oa, on, y, u, xs, modall, g_ffn[layer].astype(F32),
                                            w_out[layer].astype(BF16), s5_d[layer].astype(F32),
                                            s5_w_glu[layer].astype(BF16), s5_b_glu[layer].astype(F32),
                                            w_router_pad, lc, with_ctx)
        rows = x_new.shape[1]
        t = b * rows
        mask_t, gates_t, counts = _route(logits_t, e_bias[layer])
        n_tiles = (t * TOP_K) // ROW_TILE + N_EXPERTS
        offsets, pad_start, pad_cnt, tile_expert, n_used = _expert_layout(counts[:, 0].astype(jnp.int32), n_tiles)
        pos, gate_tm = _positions(mask_t, gates_t, offsets)

        gt2 = modall[:, :, 5, :]
        tiles_per_batch = rows // ROW_TILE
        if with_ctx:
            nctx = lc // ROW_TILE
            sel = (jnp.arange(tiles_per_batch) >= nctx).astype(jnp.int32)
        else:
            sel = jnp.ones((tiles_per_batch,), jnp.int32)
        gt_tiles = gt2[:, sel, :].reshape(t // ROW_TILE, 1, d)

        h2f = h2.reshape(t, d)
        hs = _dispatch(h2f, pos, pad_start.astype(jnp.int32), pad_cnt.astype(jnp.int32), n_tiles * ROW_TILE)
        os = _expert_ffn(hs, tile_expert, n_used, w_gate[layer].astype(BF16), w_up[layer].astype(BF16),
                         w_down[layer].astype(BF16))
        out = _combine(pos, os, gate_tm, h2f, x_new.reshape(t, d), gt_tiles,
                       ws_gate[layer].astype(BF16), ws_up[layer].astype(BF16), ws_down[layer].astype(BF16))
        xs = out.reshape(b, rows, d)

    return xs.astype(x.dtype)
```

```python
import functools
import math

import jax
import jax.numpy as jnp
from jax import lax
from jax.experimental import pallas as pl
from jax.experimental.pallas import tpu as pltpu

F32 = jnp.float32
BF16 = jnp.bfloat16

D_MODEL = 1024
GRID_W = 64
EPS = 1e-6
A_HEADS = 6
A_QK_DIM = 32
A_V_DIM = 64
ROPE_THETA = 10000.0
S5_GROUPS = 16
S5_GROUP_CH = 16
S5_STATE = 64
N_HEADS = 6
N_HEAD_DIM = 64
WIN_ROWS = 8
WIN_COLS = 16
HW = 384
B_WIDTH = 256
Q_COLS = 768
IN_COLS = 2560
N_EXPERTS = 64
TOP_K = 8
N_GROUPS = 8
TOPK_GROUPS = 4
EXPERT_DIM = 256
ROUTED_SCALE = 2.5

ROW_TILE = 256
NA_ROWS = 4
NA_KROWS = 12
S5_CHUNK = 128
MOE_TILE = 1024
MOE_EB = 4
NEG = -1e30
LOG2E = math.log2(math.e)
SCORE_BOUND_LIMIT = 50.0
BOUND_SLACK = 1.05
VMEM_LIMIT = 56 * 1024 * 1024


def _sigmoid(x):
    return 1.0 / (1.0 + jnp.exp(-x))


def _gelu_tanh(x):
    return 0.5 * x * (1.0 + jnp.tanh(math.sqrt(2.0 / math.pi) * (x + 0.044715 * (x * x * x))))


def _split_bf16(a):
    hi = a.astype(BF16)
    lo = (a - hi.astype(F32)).astype(BF16)
    return hi, lo


def _dot(a, b):
    return jnp.dot(a, b, preferred_element_type=F32)


def _dot3(a, b):
    ah, al = _split_bf16(a)
    bh, bl = _split_bf16(b)
    return _dot(ah, bh) + _dot(ah, bl) + _dot(al, bh)


def _cparams(sem):
    return pltpu.CompilerParams(dimension_semantics=sem, vmem_limit_bytes=VMEM_LIMIT)


def _mod_kernel(c_ref, w_ref, b_ref, o_ref):
    c = c_ref[...]
    cond = c * _sigmoid(c)
    o_ref[...] = _dot3(cond, w_ref[...]) + b_ref[...]


def _modulation(cond_rows, w_mod, b_mod):
    r, d = cond_rows.shape
    n = w_mod.shape[1]
    tn = 1536
    return pl.pallas_call(
        _mod_kernel,
        out_shape=jax.ShapeDtypeStruct((r, n), F32),
        grid=(n // tn,),
        in_specs=[pl.BlockSpec((r, d), lambda j: (0, 0)),
                  pl.BlockSpec((d, tn), lambda j: (0, j)),
                  pl.BlockSpec((1, tn), lambda j: (0, j))],
        out_specs=pl.BlockSpec((r, tn), lambda j: (0, j)),
        compiler_params=_cparams(("arbitrary",)),
        name="adaln_mod",
    )(cond_rows, w_mod, b_mod.reshape(1, n))


def _group_rms(t, ones_ref, gain_ref, group):
    ms = _dot((t * t).astype(BF16), ones_ref[...]) * (1.0 / group)
    return t * lax.rsqrt(ms + EPS) * gain_ref[...]


def _rope(t, cos_ref, sa_ref, sb_ref):
    up = pltpu.roll(t, HW - 8, 1)
    dn = pltpu.roll(t, 8, 1)
    return t * cos_ref[...] + up * sa_ref[...] + dn * sb_ref[...]


def _proj_kernel(x_ref, mod_ref, g_ref, w_ref, cos_ref, sa_ref, sb_ref, ones32_ref, ones64_ref,
                 gqa_ref, gka_ref, gqn_ref, gkn_ref,
                 qa_ref, kat_ref, va_ref, qn_ref, knt_ref, vn_ref, u_ref):
    x = x_ref[0]
    mod = mod_ref[0, 0]
    ms = jnp.mean(x * x, axis=-1, keepdims=True)
    h = x * lax.rsqrt(ms + EPS) * g_ref[...] * (1.0 + mod[1:2]) + mod[0:1]
    hb = h.astype(BF16)

    def sec(a, b):
        return _dot(hb, w_ref[:, a:b])

    qa = _rope(_group_rms(sec(0, 384), ones32_ref, gqa_ref, A_QK_DIM), cos_ref, sa_ref, sb_ref)
    qa_ref[0] = qa.astype(BF16)
    qn_ref[0] = _group_rms(sec(384, 768), ones64_ref, gqn_ref, N_HEAD_DIM).astype(BF16)
    ka = _rope(_group_rms(sec(768, 1152), ones32_ref, gka_ref, A_QK_DIM), cos_ref, sa_ref, sb_ref)
    kat_ref[0] = ka.T.astype(BF16)
    va_ref[0] = sec(1152, 1536).astype(BF16)
    kn = _group_rms(sec(1536, 1920), ones64_ref, gkn_ref, N_HEAD_DIM)
    knt_ref[0] = kn.T.astype(BF16)
    vn_ref[0] = sec(1920, 2304).astype(BF16)
    u_ref[0] = sec(2304, 2560)


def _in_projection(xs, modall, g_mix, w_in_bf, tables, gains, lc):
    b, s, d = xs.shape
    tm = ROW_TILE
    cos, sa, sb, ones32, ones64 = tables
    row = lambda i, bb: (bb, i, 0)
    tab = lambda i, bb: (i, 0)
    const2 = lambda i, bb: (0, 0)
    act = lambda w, dt: jax.ShapeDtypeStruct((b, s, w), dt)
    act_t = jax.ShapeDtypeStruct((b, HW, s), BF16)
    return pl.pallas_call(
        _proj_kernel,
        out_shape=(act(HW, BF16), act_t, act(HW, BF16), act(HW, BF16), act_t, act(HW, BF16), act(B_WIDTH, F32)),
        grid=(s // tm, b),
        in_specs=[pl.BlockSpec((1, tm, d), row),
                  pl.BlockSpec((1, 1, 6, d), lambda i, bb: (bb, jnp.minimum(i, 1), 0, 0)),
                  pl.BlockSpec((1, d), const2),
                  pl.BlockSpec((d, IN_COLS), const2),
                  pl.BlockSpec((tm, HW), tab), pl.BlockSpec((tm, HW), tab), pl.BlockSpec((tm, HW), tab),
                  pl.BlockSpec((HW, HW), const2), pl.BlockSpec((HW, HW), const2),
                  pl.BlockSpec((1, HW), const2), pl.BlockSpec((1, HW), const2),
                  pl.BlockSpec((1, HW), const2), pl.BlockSpec((1, HW), const2)],
        out_specs=(pl.BlockSpec((1, tm, HW), row),
                   pl.BlockSpec((1, HW, tm), lambda i, bb: (bb, 0, i)),
                   pl.BlockSpec((1, tm, HW), row),
                   pl.BlockSpec((1, tm, HW), row),
                   pl.BlockSpec((1, HW, tm), lambda i, bb: (bb, 0, i)),
                   pl.BlockSpec((1, tm, HW), row),
                   pl.BlockSpec((1, tm, B_WIDTH), row)),
        compiler_params=_cparams(("arbitrary", "arbitrary")),
        name="in_proj",
    )(xs, modall, g_mix.reshape(1, d), w_in_bf, cos, sa, sb, ones32, ones64, *gains)


def _rope_tables(s, lc):
    p = jnp.arange(s)
    pos = jnp.maximum(p - lc, 0)
    rows = (pos // GRID_W).astype(F32)
    cols = (pos % GRID_W).astype(F32)
    lane = jnp.arange(HW)
    j32 = lane % A_QK_DIM
    half = j32 // 16
    i16 = j32 % 16
    nf = 8
    inv = ROPE_THETA ** (-(i16 % nf).astype(F32) / nf)
    coord = jnp.where(half[None, :] == 0, rows[:, None], cols[:, None])
    ang = coord * inv[None, :]
    is_lat = (p >= lc)[:, None]
    second = (i16 >= nf)[None, :]
    cos = jnp.where(is_lat, jnp.cos(ang), 1.0)
    sin = jnp.where(is_lat, jnp.sin(ang), 0.0)
    sa = jnp.where(second, 0.0, -sin)
    sb = jnp.where(second, sin, 0.0)
    return cos.astype(F32), sa.astype(F32), sb.astype(F32)


def _block_ones(group):
    g = jnp.arange(HW) // group
    return (g[:, None] == g[None, :]).astype(BF16)


def _diff_attend(q_all, kt_ref, v_ref, lam, gsub_ref, sk, use_max):
    outs = []
    for h in range(A_HEADS):
        v = v_ref[0, 0:sk, h * A_V_DIM:(h + 1) * A_V_DIM]
        parts = []
        for sub in range(2):
            hs = 2 * h + sub
            off = hs * A_QK_DIM
            sc = _dot(q_all[:, off:off + A_QK_DIM], kt_ref[0, off:off + A_QK_DIM, 0:sk])
            e = jnp.exp2(sc - jnp.max(sc, axis=-1, keepdims=True)) if use_max else jnp.exp2(sc)
            parts.append(_dot(e.astype(BF16), v) * (1.0 / jnp.sum(e, axis=-1, keepdims=True)))
        o = parts[0] - lam * parts[1]
        outs.append(o * lax.rsqrt(jnp.mean(o * o, axis=-1, keepdims=True) + EPS))
    return (jnp.concatenate(outs, axis=-1) * gsub_ref[...]).astype(BF16)


def _diff_attn_kernel(q_ref, kt_ref, v_ref, lam_ref, gsub_ref, o_ref, *, lc, ctx_first, use_max):
    lam = lam_ref[...]
    s = kt_ref.shape[2]
    if ctx_first:
        i = pl.program_id(1)

        @pl.when(i == 0)
        def _():
            o_ref[0] = _diff_attend(q_ref[0], kt_ref, v_ref, lam, gsub_ref, lc, use_max)

        @pl.when(i > 0)
        def _():
            o_ref[0] = _diff_attend(q_ref[0], kt_ref, v_ref, lam, gsub_ref, s, use_max)
    else:
        o_ref[0] = _diff_attend(q_ref[0], kt_ref, v_ref, lam, gsub_ref, s, use_max)


def _diff_attention(qa, kat, va, lam, gsub_t, lc, with_ctx, use_max):
    b, s, _ = qa.shape
    tq = ROW_TILE
    off = 0 if with_ctx else lc // tq
    rows_out = s - off * tq
    return pl.pallas_call(
        functools.partial(_diff_attn_kernel, lc=lc, ctx_first=with_ctx, use_max=use_max),
        out_shape=jax.ShapeDtypeStruct((b, rows_out, HW), BF16),
        grid=(b, rows_out // tq),
        in_specs=[pl.BlockSpec((1, tq, HW), lambda bb, i: (bb, i + off, 0)),
                  pl.BlockSpec((1, HW, s), lambda bb, i: (bb, 0, 0)),
                  pl.BlockSpec((1, s, HW), lambda bb, i: (bb, 0, 0)),
                  pl.BlockSpec((1, 1), lambda bb, i: (0, 0)),
                  pl.BlockSpec((1, HW), lambda bb, i: (0, 0))],
        out_specs=pl.BlockSpec((1, tq, HW), lambda bb, i: (bb, i, 0)),
        compiler_params=_cparams(("arbitrary", "arbitrary")),
        name="diff_attn_rowmax" if use_max else "diff_attn",
    )(qa, kat, va, lam, gsub_t)


def _na_ctx_attend(q_all, kt_ref, v_ref, lc, use_max):
    outs = []
    for h in range(N_HEADS):
        hs = slice(h * N_HEAD_DIM, (h + 1) * N_HEAD_DIM)
        sc = _dot(q_all[:, hs], kt_ref[0, hs, 0:lc])
        e = jnp.exp2(sc - jnp.max(sc, axis=-1, keepdims=True)) if use_max else jnp.exp2(sc)
        o = _dot(e.astype(BF16), v_ref[0, 0:lc, hs])
        outs.append(o * (1.0 / jnp.sum(e, axis=-1, keepdims=True)))
    return jnp.concatenate(outs, axis=-1).astype(BF16)


def _na_attend(q_all, kt_ref, v_ref, bias_ref, koff, lc, use_max):
    nk = NA_KROWS * GRID_W
    outs = []
    for h in range(N_HEADS):
        hs = slice(h * N_HEAD_DIM, (h + 1) * N_HEAD_DIM)
        q = q_all[:, hs]
        s_loc = _dot(q, kt_ref[0, hs, pl.ds(koff, nk)]) + bias_ref[0, h]
        s_ctx = _dot(q, kt_ref[0, hs, 0:lc])
        if use_max:
            m = jnp.maximum(jnp.max(s_loc, axis=-1, keepdims=True), jnp.max(s_ctx, axis=-1, keepdims=True))
            s_loc, s_ctx = s_loc - m, s_ctx - m
        e_loc = jnp.exp2(s_loc)
        e_ctx = jnp.exp2(s_ctx)
        den = jnp.sum(e_loc, axis=-1, keepdims=True) + jnp.sum(e_ctx, axis=-1, keepdims=True)
        o = _dot(e_loc.astype(BF16), v_ref[0, pl.ds(koff, nk), hs]) + _dot(e_ctx.astype(BF16), v_ref[0, 0:lc, hs])
        outs.append(o * (1.0 / den))
    return jnp.concatenate(outs, axis=-1).astype(BF16)


def _na_kernel(q_ref, kt_ref, v_ref, bias_ref, o_ref, *, lc, n_img_rows, ctx_first, use_max):
    i = pl.program_id(1)
    blk = i - 1 if ctx_first else i
    start_row = jnp.clip(NA_ROWS * blk - WIN_ROWS // 2, 0, n_img_rows - NA_KROWS)
    koff = pl.multiple_of(lc + start_row * GRID_W, 128)
    if ctx_first:
        @pl.when(i == 0)
        def _():
            o_ref[0] = _na_ctx_attend(q_ref[0], kt_ref, v_ref, lc, use_max)

        @pl.when(i > 0)
        def _():
            o_ref[0] = _na_attend(q_ref[0], kt_ref, v_ref, bias_ref, koff, lc, use_max)
    else:
        o_ref[0] = _na_attend(q_ref[0], kt_ref, v_ref, bias_ref, koff, lc, use_max)


def _na_attention(qn, knt, vn, bias, lc, with_ctx, use_max):
    b, s, _ = qn.shape
    tq = NA_ROWS * GRID_W
    assert tq == ROW_TILE and lc % tq == 0
    n_img_rows = (s - lc) // GRID_W
    nblk = n_img_rows // NA_ROWS
    off = 0 if with_ctx else lc // tq
    rows_out = s - off * tq
    first = 1 if with_ctx else 0

    def variant(bb, i):
        blk = i - first
        return (jnp.where(blk <= 0, 0, jnp.where(blk == nblk - 1, 2, 1)), 0, 0, 0)

    return pl.pallas_call(
        functools.partial(_na_kernel, lc=lc, n_img_rows=n_img_rows, ctx_first=with_ctx, use_max=use_max),
        out_shape=jax.ShapeDtypeStruct((b, rows_out, HW), BF16),
        grid=(b, rows_out // tq),
        in_specs=[pl.BlockSpec((1, tq, HW), lambda bb, i: (bb, i + off, 0)),
                  pl.BlockSpec((1, HW, s), lambda bb, i: (bb, 0, 0)),
                  pl.BlockSpec((1, s, HW), lambda bb, i: (bb, 0, 0)),
                  pl.BlockSpec((1, N_HEADS, tq, NA_KROWS * GRID_W), variant)],
        out_specs=pl.BlockSpec((1, tq, HW), lambda bb, i: (bb, i, 0)),
        compiler_params=_cparams(("arbitrary", "arbitrary")),
        name="nbr_attn_rowmax" if use_max else "nbr_attn",
    )(qn, knt, vn, bias)


def _na_bias(rpb, n_img_rows):
    a = jnp.arange(NA_ROWS)[:, None, None, None]
    cq = jnp.arange(GRID_W)[None, :, None, None]
    j = jnp.arange(NA_KROWS)[None, None, :, None]
    ck = jnp.arange(GRID_W)[None, None, None, :]
    cstart = jnp.clip(cq - WIN_COLS // 2, 0, GRID_W - WIN_COLS)
    colmask = (ck >= cstart) & (ck < cstart + WIN_COLS)
    dc = jnp.clip(ck - cq, -(WIN_COLS - 1), WIN_COLS - 1) + (WIN_COLS - 1)
    by_col = jnp.take(rpb.astype(F32), dc.reshape(-1), axis=2).reshape(N_HEADS, 2 * WIN_ROWS - 1, GRID_W, GRID_W)
    out = []
    for r0_minus_k, wstart in ((0, 0 * a), (WIN_ROWS // 2, a), (NA_KROWS - NA_ROWS, NA_KROWS - WIN_ROWS + 0 * a)):
        inwin = (j >= wstart) & (j < wstart + WIN_ROWS)
        dr = jnp.clip(j - r0_minus_k - a + (WIN_ROWS - 1), 0, 2 * WIN_ROWS - 2)
        vals = jnp.take(by_col, dr.reshape(-1), axis=1).reshape(N_HEADS, NA_ROWS, NA_KROWS, GRID_W, GRID_W)
        vals = jnp.transpose(vals, (0, 1, 3, 2, 4))
        vals = jnp.where((inwin & colmask)[None], vals * LOG2E, NEG)
        out.append(vals.reshape(N_HEADS, NA_ROWS * GRID_W, NA_KROWS * GRID_W))
    return jnp.stack(out)


def _s5_kernel(uf_ref, ub_ref, bmf_ref, cmf_ref, arf_ref, aif_ref, bmb_ref, cmb_ref, arb_ref, aib_ref,
               yf_ref, yb_ref, xf_scr, xb_scr, st_scr, io_scr):
    nb, tc, w = uf_ref.shape
    ns = arf_ref.shape[1]

    @pl.when(pl.program_id(0) == 0)
    def _():
        st_scr[...] = jnp.zeros_like(st_scr)

    nl = w // 128

    def drive(u_ref, bm_ref, x_scr):
        for b in range(nb):
            for c in range(nl):
                io_scr[c, pl.ds(b, tc, stride=nb), :] = u_ref[b, :, c * 128:(c + 1) * 128]
        u_tm = jnp.concatenate([io_scr[c] for c in range(nl)], axis=1)
        x_scr[...] = _dot(u_tm.astype(BF16), bm_ref[...])

    drive(uf_ref, bmf_ref, xf_scr)
    drive(ub_ref, bmb_ref, xb_scr)

    def advance(x_scr, a_re, a_im, s_re, s_im, tt):
        rows = pl.ds(pl.multiple_of(tt * nb, nb), nb)
        n_re = a_re * s_re - a_im * s_im + x_scr[rows, 0:ns]
        n_im = a_re * s_im + a_im * s_re + x_scr[rows, ns:2 * ns]
        x_scr[rows, 0:ns] = n_re
        x_scr[rows, ns:2 * ns] = n_im
        return n_re, n_im

    def step(t, carry):
        f_re, f_im, b_re, b_im = carry
        f_re, f_im = advance(xf_scr, arf_ref[...], aif_ref[...], f_re, f_im, t)
        b_re, b_im = advance(xb_scr, arb_ref[...], aib_ref[...], b_re, b_im, tc - 1 - t)
        return f_re, f_im, b_re, b_im

    init = (st_scr[0, :, 0:ns], st_scr[0, :, ns:2 * ns], st_scr[1, :, 0:ns], st_scr[1, :, ns:2 * ns])
    f_re, f_im, b_re, b_im = lax.fori_loop(0, tc, step, init, unroll=2)
    st_scr[0, :, 0:ns] = f_re
    st_scr[0, :, ns:2 * ns] = f_im
    st_scr[1, :, 0:ns] = b_re
    st_scr[1, :, ns:2 * ns] = b_im

    def readout(x_scr, cm_ref, y_ref):
        y_tm = _dot(x_scr[...].astype(BF16), cm_ref[...])
        for c in range(nl):
            io_scr[c] = y_tm[:, c * 128:(c + 1) * 128]
        for b in range(nb):
            for c in range(nl):
                y_ref[b, :, c * 128:(c + 1) * 128] = io_scr[c, pl.ds(b, tc, stride=nb), :]

    readout(xf_scr, cmf_ref, yf_ref)
    readout(xb_scr, cmb_ref, yb_ref)


def _s5_scan(u, fwd, bwd, lc):
    nb, s, w = u.shape
    tc = S5_CHUNK
    nc, ncc = s // tc, lc // tc
    ns = fwd[2].shape[1]
    chunk_f = lambda j: (0, j, 0)
    chunk_b = lambda j: (0, jnp.where(j < ncc, ncc - 1 - j, nc - 1 - (j - ncc)), 0)
    const = lambda j: (0, 0)
    pspecs = [pl.BlockSpec((w, 2 * ns), const), pl.BlockSpec((2 * ns, w), const),
              pl.BlockSpec((nb, ns), const), pl.BlockSpec((nb, ns), const)]
    out = jax.ShapeDtypeStruct((nb, s, w), F32)
    return pl.pallas_call(
        _s5_kernel,
        out_shape=(out, out),
        grid=(nc,),
        in_specs=[pl.BlockSpec((nb, tc, w), chunk_f), pl.BlockSpec((nb, tc, w), chunk_b)] + pspecs + pspecs,
        out_specs=(pl.BlockSpec((nb, tc, w), chunk_f), pl.BlockSpec((nb, tc, w), chunk_b)),
        scratch_shapes=[pltpu.VMEM((tc * nb, 2 * ns), F32), pltpu.VMEM((tc * nb, 2 * ns), F32),
                        pltpu.VMEM((2, nb, 2 * ns), F32), pltpu.VMEM((w // 128, tc * nb, 128), F32)],
        compiler_params=_cparams(("arbitrary",)),
        name="s5_scan",
    )(u, u, *fwd, *bwd)


def _s5_params(a_re, a_im, log_dt, b_re, b_im, c_re, c_im, nb):
    g, n, p = S5_GROUPS, S5_STATE, S5_GROUP_CH
    lr, li = a_re.astype(F32), a_im.astype(F32)
    dt = jnp.exp(log_dt.astype(F32))[..., None]
    mag = jnp.exp(lr * dt)
    ab_r, ab_i = mag * jnp.cos(li * dt), mag * jnp.sin(li * dt)
    den = lr * lr + li * li
    cf_r = ((ab_r - 1.0) * lr + ab_i * li) / den
    cf_i = (ab_i * lr - (ab_r - 1.0) * li) / den
    br, bi = b_re.astype(F32), b_im.astype(F32)
    bb_r = cf_r[..., None] * br - cf_i[..., None] * bi
    bb_i = cf_r[..., None] * bi + cf_i[..., None] * br
    eye = jnp.eye(g, dtype=F32)
    out = []
    for k in range(2):
        b_r = jnp.einsum('gnp,gh->gphn', bb_r[k], eye).reshape(g * p, g * n)
        b_i = jnp.einsum('gnp,gh->gphn', bb_i[k], eye).reshape(g * p, g * n)
        bmat = jnp.concatenate([b_r, b_i], axis=1).astype(BF16)
        ct = jnp.transpose(c_re[k].astype(F32), (0, 2, 1))
        ci = jnp.transpose(c_im[k].astype(F32), (0, 2, 1))
        c_r = jnp.einsum('gnp,gh->gnhp', ct, eye).reshape(g * n, g * p)
        c_i = jnp.einsum('gnp,gh->gnhp', ci, eye).reshape(g * n, g * p)
        cmat = jnp.concatenate([c_r, -c_i], axis=0).astype(BF16)
        ar = jnp.broadcast_to(ab_r[k].reshape(1, g * n), (nb, g * n))
        ai = jnp.broadcast_to(ab_i[k].reshape(1, g * n), (nb, g * n))
        out.append((bmat, cmat, ar, ai))
    return out


def _out_kernel(oa_ref, on_ref, yf_ref, yb_ref, u_ref, x_ref, mod_ref, gffn_ref, wo_ref, dskip_ref, wglu_ref, bglu_ref,
                wr_ref, xo_ref, h2_ref, lg_ref):
    mod = mod_ref[0, 0]
    nt = lambda a, bb: lax.dot_general(a, bb, (((1,), (1,)), ((), ())), preferred_element_type=F32)
    wh, wl = _split_bf16(wr_ref[...])
    half = x_ref.shape[1] // 2
    for r in (slice(0, half), slice(half, 2 * half)):
        g = _gelu_tanh(yf_ref[0, r, :] + yb_ref[0, r, :] + dskip_ref[...] * u_ref[0, r, :])
        ob = g * _sigmoid(_dot(g.astype(BF16), wglu_ref[...]) + bglu_ref[...])
        mix = (_dot(oa_ref[0, r, :], wo_ref[0:HW, :]) + _dot(ob.astype(BF16), wo_ref[HW:HW + B_WIDTH, :])
               + _dot(on_ref[0, r, :], wo_ref[HW + B_WIDTH:, :]))
        x = x_ref[0, r, :] + mod[2:3] * mix
        xo_ref[0, r, :] = x
        ms = jnp.mean(x * x, axis=-1, keepdims=True)
        h2 = x * lax.rsqrt(ms + EPS) * gffn_ref[...] * (1.0 + mod[4:5]) + mod[3:4]
        h2_ref[0, r, :] = h2.astype(BF16)
        hh, hl = _split_bf16(h2)
        lg_ref[0, :, r] = nt(wh, hh) + nt(wh, hl) + nt(wl, hh)


def _out_projection(oa, on, y_f, y_b, u, xs, modall, g_ffn, w_out_bf, d_skip, w_glu_bf, b_glu, w_router_pad, lc,
                    with_ctx):
    b, s, d = xs.shape
    tm = ROW_TILE
    off = 0 if with_ctx else lc // tm
    rows_out = s - off * tm
    full = lambda bb, i: (bb, i + off, 0)
    outr = lambda bb, i: (bb, i, 0)
    const = lambda bb, i: (0, 0)
    ne = w_router_pad.shape[0]
    return pl.pallas_call(
        _out_kernel,
        out_shape=(jax.ShapeDtypeStruct((b, rows_out, d), F32),
                   jax.ShapeDtypeStruct((b, rows_out, d), BF16),
                   jax.ShapeDtypeStruct((b, ne, rows_out), F32)),
        grid=(b, rows_out // tm),
        in_specs=[pl.BlockSpec((1, tm, HW), outr),
                  pl.BlockSpec((1, tm, HW), outr),
                  pl.BlockSpec((1, tm, B_WIDTH), full),
                  pl.BlockSpec((1, tm, B_WIDTH), full),
                  pl.BlockSpec((1, tm, B_WIDTH), full),
                  pl.BlockSpec((1, tm, d), full),
                  pl.BlockSpec((1, 1, 6, d), lambda bb, i: (bb, jnp.minimum(i + off, 1), 0, 0)),
                  pl.BlockSpec((1, d), const),
                  pl.BlockSpec((d, d), const),
                  pl.BlockSpec((1, B_WIDTH), const),
                  pl.BlockSpec((B_WIDTH, B_WIDTH), const),
                  pl.BlockSpec((1, B_WIDTH), const),
                  pl.BlockSpec((ne, d), const)],
        out_specs=(pl.BlockSpec((1, tm, d), outr), pl.BlockSpec((1, tm, d), outr),
                   pl.BlockSpec((1, ne, tm), lambda bb, i: (bb, 0, i))),
        compiler_params=_cparams(("arbitrary", "arbitrary")),
        name="out_proj",
    )(oa, on, y_f, y_b, u, xs, modall, g_ffn.reshape(1, d), w_out_bf, d_skip.reshape(1, -1), w_glu_bf,
      b_glu.reshape(1, -1), w_router_pad)


def _swiglu(h, wg, wu):
    a = _dot(h, wg)
    return a * _sigmoid(a) * _dot(h, wu)


def _moe_kernel(h_ref, gates_ref, x_ref, gt_ref, wg_ref, wu_ref, wd_ref, sg_ref, su_ref, sd_ref, o_ref, acc_ref):
    e = pl.program_id(1)
    n_routed = pl.num_programs(1) - 1

    @pl.when(e == 0)
    def _():
        acc_ref[...] = jnp.zeros_like(acc_ref)

    @pl.when(e < n_routed)
    def _():
        h = h_ref[...]
        src = lax.broadcasted_iota(jnp.int32, (128, 128), 0)
        dst = lax.broadcasted_iota(jnp.int32, (128, 128), 1)
        sel = jnp.where(src == e * MOE_EB + dst, jnp.where(dst < MOE_EB, 1.0, 0.0), 0.0).astype(BF16)
        g_hi, g_lo = _split_bf16(gates_ref[...])
        gsel = _dot(g_hi, sel) + _dot(g_lo, sel)
        hid = [(_swiglu(h, wg_ref[j], wu_ref[j]) * gsel[:, j:j + 1]).astype(BF16) for j in range(MOE_EB)]
        acc_ref[...] += _dot(jnp.concatenate(hid, axis=-1), wd_ref[...])

    @pl.when(e == n_routed)
    def _():
        for j in range(h_ref.shape[0] // ROW_TILE):
            rows = slice(j * ROW_TILE, (j + 1) * ROW_TILE)
            hs = _swiglu(h_ref[rows, :], sg_ref[...], su_ref[...]).astype(BF16)
            y = acc_ref[rows, :] + _dot(hs, sd_ref[...])
            o_ref[rows, :] = x_ref[rows, :] + gt_ref[0, j:j + 1, :] * y


def _moe_tile(t):
    return max(m for m in range(ROW_TILE, MOE_TILE + 1, ROW_TILE) if t % m == 0)


def _moe(h2, gates, x, gt_rows, wg, wu, wd, sg, su, sd):
    t, d = h2.shape
    tm = _moe_tile(t)
    n_routed = wg.shape[0] // MOE_EB
    nsub = tm // ROW_TILE
    step = lambda e: jnp.minimum(e, n_routed - 1)
    const = lambda i, e: (0, 0)
    return pl.pallas_call(
        _moe_kernel,
        out_shape=jax.ShapeDtypeStruct((t, d), F32),
        grid=(t // tm, n_routed + 1),
        in_specs=[pl.BlockSpec((tm, d), lambda i, e: (i, 0)),
                  pl.BlockSpec((tm, 128), lambda i, e: (i, 0)),
                  pl.BlockSpec((tm, d), lambda i, e: (i, 0)),
                  pl.BlockSpec((1, nsub, d), lambda i, e: (i, 0, 0)),
                  pl.BlockSpec((MOE_EB, d, EXPERT_DIM), lambda i, e: (step(e), 0, 0)),
                  pl.BlockSpec((MOE_EB, d, EXPERT_DIM), lambda i, e: (step(e), 0, 0)),
                  pl.BlockSpec((MOE_EB * EXPERT_DIM, d), lambda i, e: (step(e), 0)),
                  pl.BlockSpec((d, EXPERT_DIM), const),
                  pl.BlockSpec((d, EXPERT_DIM), const),
                  pl.BlockSpec((EXPERT_DIM, d), const)],
        out_specs=pl.BlockSpec((tm, d), lambda i, e: (i, 0)),
        scratch_shapes=[pltpu.VMEM((tm, d), F32)],
        compiler_params=_cparams(("arbitrary", "arbitrary")),
        name="moe_ffn",
    )(h2, gates, x, gt_rows, wg, wu, wd, sg, su, sd)


def _route_kernel(lg_ref, bias_ref, o_ref):
    gsz = N_EXPERTS // N_GROUPS
    tn = lg_ref.shape[2]
    ninf = -jnp.inf
    jidx = lax.broadcasted_iota(jnp.int32, (gsz, tn), 0)
    scores, biased, gscore = [], [], []
    for g in range(N_GROUPS):
        rows = slice(g * gsz, (g + 1) * gsz)
        sc = _sigmoid(lg_ref[0, rows, :])
        bi = sc + bias_ref[rows, :]
        m1 = jnp.max(bi, axis=0, keepdims=True)
        first = jnp.min(jnp.where(bi == m1, jidx, gsz), axis=0, keepdims=True)
        m2 = jnp.max(jnp.where(jidx == first, ninf, bi), axis=0, keepdims=True)
        scores.append(sc)
        biased.append(bi)
        gscore.append(m1 + m2)
    masked = []
    for g in range(N_GROUPS):
        rank = jnp.zeros((1, tn), F32)
        for g2 in range(N_GROUPS):
            if g2 != g:
                ahead = (gscore[g2] >= gscore[g]) if g2 < g else (gscore[g2] > gscore[g])
                rank = rank + jnp.where(ahead, 1.0, 0.0)
        keep = jnp.broadcast_to(rank, (gsz, tn)) < TOPK_GROUPS
        masked.append(jnp.where(keep, biased[g], ninf))
    ranks = [jnp.zeros((gsz, tn), F32) for _ in range(N_GROUPS)]
    for g2 in range(N_GROUPS):
        for j2 in range(gsz):
            other = jnp.broadcast_to(masked[g2][j2:j2 + 1, :], (gsz, tn))
            for g in range(N_GROUPS):
                ge = jnp.where(other >= masked[g], 1.0, 0.0)
                gt = jnp.where(other > masked[g], 1.0, 0.0)
                if g2 < g:
                    ahead = ge
                elif g2 > g:
                    ahead = gt
                else:
                    ahead = jnp.where(jidx > j2, ge, gt)
                ranks[g] = ranks[g] + ahead
    picked = [jnp.where(ranks[g] < TOP_K, scores[g], 0.0) for g in range(N_GROUPS)]
    den = sum(jnp.sum(p, axis=0, keepdims=True) for p in picked)
    scale = ROUTED_SCALE / den
    shared_row = jnp.where(lax.broadcasted_iota(jnp.int32, (128 - N_EXPERTS, tn), 0) == 0, 1.0, 0.0)
    gates_t = jnp.concatenate([p * scale for p in picked] + [shared_row], axis=0)
    o_ref[...] = gates_t.T


def _route(logits_t, e_bias):
    b, ne, rows = logits_t.shape
    tn = ROW_TILE
    nt = rows // tn
    bias = jnp.broadcast_to(e_bias.astype(F32)[:, None], (N_EXPERTS, tn))
    return pl.pallas_call(
        _route_kernel,
        out_shape=jax.ShapeDtypeStruct((b * rows, ne), F32),
        grid=(b, nt),
        in_specs=[pl.BlockSpec((1, ne, tn), lambda bb, i: (bb, 0, i)),
                  pl.BlockSpec((N_EXPERTS, tn), lambda bb, i: (0, 0))],
        out_specs=pl.BlockSpec((tn, ne), lambda bb, i: (bb * nt + i, 0)),
        compiler_params=_cparams(("arbitrary", "arbitrary")),
        name="moe_route",
    )(logits_t, bias)


def kernel(x, c, ctx, c_ctx, w_mod, b_mod, g_mix, g_ffn, w_in, w_out, a_gq, a_gk, a_lambda, a_gsub, n_gq, n_gk, n_rpb, s5_a_re, s5_a_im, s5_log_dt, s5_b_re, s5_b_im, s5_c_re, s5_c_im, s5_d, s5_w_glu, s5_b_glu, w_router, e_bias, w_gate, w_up, w_down, ws_gate, ws_up, ws_down):
    b, l, d = x.shape
    lc = ctx.shape[1]
    s = lc + l
    depth = w_mod.shape[0]
    n_img_rows = l // GRID_W

    xs = jnp.concatenate([ctx, x], axis=1).astype(F32)
    cond_rows = jnp.zeros((16, d), F32).at[:b].set(c.astype(F32)).at[b].set(c_ctx.astype(F32))
    cos, sa, sb = _rope_tables(s, lc)
    tables = (cos, sa, sb, _block_ones(A_QK_DIM), _block_ones(N_HEAD_DIM))

    for layer in range(depth):
        last = layer == depth - 1
        with_ctx = not last
        lam_init = 0.8 - 0.6 * math.exp(-0.3 * layer)

        mod = _modulation(cond_rows, w_mod[layer].astype(F32), b_mod[layer].astype(F32))
        mod_lat = mod[:b].reshape(b, 1, 6, d)
        mod_ctx = jnp.broadcast_to(mod[b].reshape(1, 1, 6, d), (b, 1, 6, d))
        modall = jnp.concatenate([mod_ctx, mod_lat], axis=1)

        gains = ((jnp.tile(a_gq[layer].astype(F32), HW // A_QK_DIM) * (A_QK_DIM ** -0.5 * LOG2E)).reshape(1, HW),
                 jnp.tile(a_gk[layer].astype(F32), HW // A_QK_DIM).reshape(1, HW),
                 (jnp.tile(n_gq[layer].astype(F32), N_HEADS) * (N_HEAD_DIM ** -0.5 * LOG2E)).reshape(1, HW),
                 jnp.tile(n_gk[layer].astype(F32), N_HEADS).reshape(1, HW))
        qa, kat, va, qn, knt, vn, u = _in_projection(xs, modall, g_mix[layer].astype(F32), w_in[layer].astype(BF16),
                                                     tables, gains, lc)

        lv = a_lambda[layer].astype(F32)
        lam = (jnp.exp(jnp.sum(lv[0] * lv[1])) - jnp.exp(jnp.sum(lv[2] * lv[3])) + lam_init).reshape(1, 1)
        gsub_t = (jnp.tile(a_gsub[layer].astype(F32), A_HEADS) * (1.0 - lam_init)).reshape(1, HW)
        a_bound = BOUND_SLACK * A_QK_DIM * jnp.max(jnp.abs(gains[0])) * jnp.max(jnp.abs(gains[1]))
        oa = lax.cond(a_bound <= SCORE_BOUND_LIMIT,
                      lambda *a: _diff_attention(*a, lc, with_ctx, False),
                      lambda *a: _diff_attention(*a, lc, with_ctx, True), qa, kat, va, lam, gsub_t)

        n_bound = (BOUND_SLACK * N_HEAD_DIM * jnp.max(jnp.abs(gains[2])) * jnp.max(jnp.abs(gains[3]))
                   + LOG2E * jnp.max(jnp.abs(n_rpb[layer].astype(F32))))
        on = lax.cond(n_bound <= SCORE_BOUND_LIMIT,
                      lambda *a: _na_attention(*a, lc, with_ctx, False),
                      lambda *a: _na_attention(*a, lc, with_ctx, True),
                      qn, knt, vn, _na_bias(n_rpb[layer], n_img_rows))

        s5_fwd, s5_bwd = _s5_params(s5_a_re[layer], s5_a_im[layer], s5_log_dt[layer], s5_b_re[layer],
                                    s5_b_im[layer], s5_c_re[layer], s5_c_im[layer], b)
        y_f, y_b = _s5_scan(u, s5_fwd, s5_bwd, lc)

        w_router_pad = jnp.zeros((128, d), F32).at[:N_EXPERTS].set(w_router[layer].astype(F32).T)
        x_new, h2, logits_t = _out_projection(oa, on, y_f, y_b, u, xs, modall, g_ffn[layer].astype(F32),
                                              w_out[layer].astype(BF16), s5_d[layer].astype(F32),
                                              s5_w_glu[layer].astype(BF16), s5_b_glu[layer].astype(F32),
                                              w_router_pad, lc, with_ctx)
        rows = x_new.shape[1]
        t = b * rows
        gates = _route(logits_t, e_bias[layer])

        gt2 = modall[:, :, 5, :]
        tiles_per_batch = rows // ROW_TILE
        if with_ctx:
            nctx = lc // ROW_TILE
            sel = (jnp.arange(tiles_per_batch) >= nctx).astype(jnp.int32)
        else:
            sel = jnp.ones((tiles_per_batch,), jnp.int32)
        gt_rows = gt2[:, sel, :].reshape(t // _moe_tile(t), _moe_tile(t) // ROW_TILE, d)

        out = _moe(h2.reshape(t, d), gates, x_new.reshape(t, d), gt_rows,
                   w_gate[layer].astype(BF16), w_up[layer].astype(BF16), w_down[layer].astype(BF16).reshape(-1, d),
                   ws_gate[layer].astype(BF16), ws_up[layer].astype(BF16), ws_down[layer].astype(BF16))
        xs = out.reshape(b, rows, d)

    return xs.astype(x.dtype)
```

```python
import functools
import math

import jax
import jax.numpy as jnp
from jax import lax
from jax.experimental import pallas as pl
from jax.experimental.pallas import tpu as pltpu

F32 = jnp.float32
BF16 = jnp.bfloat16

D_MODEL = 1024
GRID_W = 64
EPS = 1e-6
A_HEADS = 6
A_QK_DIM = 32
A_V_DIM = 64
ROPE_THETA = 10000.0
S5_GROUPS = 16
S5_GROUP_CH = 16
S5_STATE = 64
N_HEADS = 6
N_HEAD_DIM = 64
WIN_ROWS = 8
WIN_COLS = 16
HW = 384
B_WIDTH = 256
Q_COLS = 768
IN_COLS = 2560
N_EXPERTS = 64
TOP_K = 8
N_GROUPS = 8
TOPK_GROUPS = 4
EXPERT_DIM = 256
ROUTED_SCALE = 2.5

ROW_TILE = 256
NA_ROWS = 4
NA_KROWS = 12
S5_CHUNK = 128
MOE_TILE = 1024
MOE_EB = 4
NEG = -1e30
LOG2E = math.log2(math.e)
SCORE_BOUND_LIMIT = 50.0
BOUND_SLACK = 1.05
VMEM_LIMIT = 56 * 1024 * 1024


def _sigmoid(x):
    return 1.0 / (1.0 + jnp.exp(-x))


def _gelu_tanh(x):
    return 0.5 * x * (1.0 + jnp.tanh(math.sqrt(2.0 / math.pi) * (x + 0.044715 * (x * x * x))))


def _split_bf16(a):
    hi = a.astype(BF16)
    lo = (a - hi.astype(F32)).astype(BF16)
    return hi, lo


def _dot(a, b):
    return jnp.dot(a, b, preferred_element_type=F32)


def _dot3(a, b):
    ah, al = _split_bf16(a)
    bh, bl = _split_bf16(b)
    return _dot(ah, bh) + _dot(ah, bl) + _dot(al, bh)


def _cparams(sem):
    return pltpu.CompilerParams(dimension_semantics=sem, vmem_limit_bytes=VMEM_LIMIT)


def _mod_kernel(c_ref, w_ref, b_ref, o_ref):
    c = c_ref[...]
    cond = c * _sigmoid(c)
    o_ref[...] = _dot3(cond, w_ref[...]) + b_ref[...]


def _modulation(cond_rows, w_mod, b_mod):
    r, d = cond_rows.shape
    n = w_mod.shape[1]
    tn = 1536
    return pl.pallas_call(
        _mod_kernel,
        out_shape=jax.ShapeDtypeStruct((r, n), F32),
        grid=(n // tn,),
        in_specs=[pl.BlockSpec((r, d), lambda j: (0, 0)),
                  pl.BlockSpec((d, tn), lambda j: (0, j)),
                  pl.BlockSpec((1, tn), lambda j: (0, j))],
        out_specs=pl.BlockSpec((r, tn), lambda j: (0, j)),
        compiler_params=_cparams(("arbitrary",)),
        name="adaln_mod",
    )(cond_rows, w_mod, b_mod.reshape(1, n))


def _group_rms(t, ones_ref, gain_ref, group):
    ms = _dot((t * t).astype(BF16), ones_ref[...]) * (1.0 / group)
    return t * lax.rsqrt(ms + EPS) * gain_ref[...]


def _rope(t, cos_ref, sa_ref, sb_ref):
    up = pltpu.roll(t, HW - 8, 1)
    dn = pltpu.roll(t, 8, 1)
    return t * cos_ref[...] + up * sa_ref[...] + dn * sb_ref[...]


def _proj_kernel(x_ref, mod_ref, g_ref, w_ref, cos_ref, sa_ref, sb_ref, ones32_ref, ones64_ref,
                 gqa_ref, gka_ref, gqn_ref, gkn_ref,
                 qa_ref, kat_ref, va_ref, qn_ref, knt_ref, vn_ref, u_ref):
    x = x_ref[0]
    mod = mod_ref[0, 0]
    ms = jnp.mean(x * x, axis=-1, keepdims=True)
    h = x * lax.rsqrt(ms + EPS) * g_ref[...] * (1.0 + mod[1:2]) + mod[0:1]
    hb = h.astype(BF16)

    def sec(a, b):
        return _dot(hb, w_ref[:, a:b])

    qa = _rope(_group_rms(sec(0, 384), ones32_ref, gqa_ref, A_QK_DIM), cos_ref, sa_ref, sb_ref)
    qa_ref[0] = qa.astype(BF16)
    qn_ref[0] = _group_rms(sec(384, 768), ones64_ref, gqn_ref, N_HEAD_DIM).astype(BF16)
    ka = _rope(_group_rms(sec(768, 1152), ones32_ref, gka_ref, A_QK_DIM), cos_ref, sa_ref, sb_ref)
    kat_ref[0] = ka.T.astype(BF16)
    va_ref[0] = sec(1152, 1536).astype(BF16)
    kn = _group_rms(sec(1536, 1920), ones64_ref, gkn_ref, N_HEAD_DIM)
    knt_ref[0] = kn.T.astype(BF16)
    vn_ref[0] = sec(1920, 2304).astype(BF16)
    u_ref[0] = sec(2304, 2560)


def _in_projection(xs, modall, g_mix, w_in_bf, tables, gains, lc):
    b, s, d = xs.shape
    tm = ROW_TILE
    cos, sa, sb, ones32, ones64 = tables
    row = lambda i, bb: (bb, i, 0)
    tab = lambda i, bb: (i, 0)
    const2 = lambda i, bb: (0, 0)
    act = lambda w, dt: jax.ShapeDtypeStruct((b, s, w), dt)
    act_t = jax.ShapeDtypeStruct((b, HW, s), BF16)
    return pl.pallas_call(
        _proj_kernel,
        out_shape=(act(HW, BF16), act_t, act(HW, BF16), act(HW, BF16), act_t, act(HW, BF16), act(B_WIDTH, F32)),
        grid=(s // tm, b),
        in_specs=[pl.BlockSpec((1, tm, d), row),
                  pl.BlockSpec((1, 1, 6, d), lambda i, bb: (bb, jnp.minimum(i, 1), 0, 0)),
                  pl.BlockSpec((1, d), const2),
                  pl.BlockSpec((d, IN_COLS), const2),
                  pl.BlockSpec((tm, HW), tab), pl.BlockSpec((tm, HW), tab), pl.BlockSpec((tm, HW), tab),
                  pl.BlockSpec((HW, HW), const2), pl.BlockSpec((HW, HW), const2),
                  pl.BlockSpec((1, HW), const2), pl.BlockSpec((1, HW), const2),
                  pl.BlockSpec((1, HW), const2), pl.BlockSpec((1, HW), const2)],
        out_specs=(pl.BlockSpec((1, tm, HW), row),
                   pl.BlockSpec((1, HW, tm), lambda i, bb: (bb, 0, i)),
                   pl.BlockSpec((1, tm, HW), row),
                   pl.BlockSpec((1, tm, HW), row),
                   pl.BlockSpec((1, HW, tm), lambda i, bb: (bb, 0, i)),
                   pl.BlockSpec((1, tm, HW), row),
                   pl.BlockSpec((1, tm, B_WIDTH), row)),
        compiler_params=_cparams(("arbitrary", "arbitrary")),
        name="in_proj",
    )(xs, modall, g_mix.reshape(1, d), w_in_bf, cos, sa, sb, ones32, ones64, *gains)


def _rope_tables(s, lc):
    p = jnp.arange(s)
    pos = jnp.maximum(p - lc, 0)
    rows = (pos // GRID_W).astype(F32)
    cols = (pos % GRID_W).astype(F32)
    lane = jnp.arange(HW)
    j32 = lane % A_QK_DIM
    half = j32 // 16
    i16 = j32 % 16
    nf = 8
    inv = ROPE_THETA ** (-(i16 % nf).astype(F32) / nf)
    coord = jnp.where(half[None, :] == 0, rows[:, None], cols[:, None])
    ang = coord * inv[None, :]
    is_lat = (p >= lc)[:, None]
    second = (i16 >= nf)[None, :]
    cos = jnp.where(is_lat, jnp.cos(ang), 1.0)
    sin = jnp.where(is_lat, jnp.sin(ang), 0.0)
    sa = jnp.where(second, 0.0, -sin)
    sb = jnp.where(second, sin, 0.0)
    return cos.astype(F32), sa.astype(F32), sb.astype(F32)


def _block_ones(group):
    g = jnp.arange(HW) // group
    return (g[:, None] == g[None, :]).astype(BF16)


def _diff_attend(q_all, kt_ref, v_ref, lam, gsub_ref, sk, use_max):
    outs = []
    for h in range(A_HEADS):
        v = v_ref[0, 0:sk, h * A_V_DIM:(h + 1) * A_V_DIM]
        parts = []
        for sub in range(2):
            hs = 2 * h + sub
            off = hs * A_QK_DIM
            sc = _dot(q_all[:, off:off + A_QK_DIM], kt_ref[0, off:off + A_QK_DIM, 0:sk])
            e = jnp.exp2(sc - jnp.max(sc, axis=-1, keepdims=True)) if use_max else jnp.exp2(sc)
            parts.append(_dot(e.astype(BF16), v) * (1.0 / jnp.sum(e, axis=-1, keepdims=True)))
        o = parts[0] - lam * parts[1]
        outs.append(o * lax.rsqrt(jnp.mean(o * o, axis=-1, keepdims=True) + EPS))
    return (jnp.concatenate(outs, axis=-1) * gsub_ref[...]).astype(BF16)


def _diff_attn_kernel(q_ref, kt_ref, v_ref, lam_ref, gsub_ref, o_ref, *, lc, ctx_first, use_max):
    lam = lam_ref[...]
    s = kt_ref.shape[2]
    if ctx_first:
        i = pl.program_id(1)

        @pl.when(i == 0)
        def _():
            o_ref[0] = _diff_attend(q_ref[0], kt_ref, v_ref, lam, gsub_ref, lc, use_max)

        @pl.when(i > 0)
        def _():
            o_ref[0] = _diff_attend(q_ref[0], kt_ref, v_ref, lam, gsub_ref, s, use_max)
    else:
        o_ref[0] = _diff_attend(q_ref[0], kt_ref, v_ref, lam, gsub_ref, s, use_max)


def _diff_attention(qa, kat, va, lam, gsub_t, lc, with_ctx, use_max):
    b, s, _ = qa.shape
    tq = ROW_TILE
    off = 0 if with_ctx else lc // tq
    rows_out = s - off * tq
    return pl.pallas_call(
        functools.partial(_diff_attn_kernel, lc=lc, ctx_first=with_ctx, use_max=use_max),
        out_shape=jax.ShapeDtypeStruct((b, rows_out, HW), BF16),
        grid=(b, rows_out // tq),
        in_specs=[pl.BlockSpec((1, tq, HW), lambda bb, i: (bb, i + off, 0)),
                  pl.BlockSpec((1, HW, s), lambda bb, i: (bb, 0, 0)),
                  pl.BlockSpec((1, s, HW), lambda bb, i: (bb, 0, 0)),
                  pl.BlockSpec((1, 1), lambda bb, i: (0, 0)),
                  pl.BlockSpec((1, HW), lambda bb, i: (0, 0))],
        out_specs=pl.BlockSpec((1, tq, HW), lambda bb, i: (bb, i, 0)),
        compiler_params=_cparams(("arbitrary", "arbitrary")),
        name="diff_attn_rowmax" if use_max else "diff_attn",
    )(qa, kat, va, lam, gsub_t)


def _na_ctx_attend(q_all, kt_ref, v_ref, lc, use_max):
    outs = []
    for h in range(N_HEADS):
        hs = slice(h * N_HEAD_DIM, (h + 1) * N_HEAD_DIM)
        sc = _dot(q_all[:, hs], kt_ref[0, hs, 0:lc])
        e = jnp.exp2(sc - jnp.max(sc, axis=-1, keepdims=True)) if use_max else jnp.exp2(sc)
        o = _dot(e.astype(BF16), v_ref[0, 0:lc, hs])
        outs.append(o * (1.0 / jnp.sum(e, axis=-1, keepdims=True)))
    return jnp.concatenate(outs, axis=-1).astype(BF16)


def _na_attend(q_all, kt_ref, v_ref, bias_ref, koff, lc, use_max):
    nk = NA_KROWS * GRID_W
    outs = []
    for h in range(N_HEADS):
        hs = slice(h * N_HEAD_DIM, (h + 1) * N_HEAD_DIM)
        q = q_all[:, hs]
        s_loc = _dot(q, kt_ref[0, hs, pl.ds(koff, nk)]) + bias_ref[0, h]
        s_ctx = _dot(q, kt_ref[0, hs, 0:lc])
        if use_max:
            m = jnp.maximum(jnp.max(s_loc, axis=-1, keepdims=True), jnp.max(s_ctx, axis=-1, keepdims=True))
            s_loc, s_ctx = s_loc - m, s_ctx - m
        e_loc = jnp.exp2(s_loc)
        e_ctx = jnp.exp2(s_ctx)
        den = jnp.sum(e_loc, axis=-1, keepdims=True) + jnp.sum(e_ctx, axis=-1, keepdims=True)
        o = _dot(e_loc.astype(BF16), v_ref[0, pl.ds(koff, nk), hs]) + _dot(e_ctx.astype(BF16), v_ref[0, 0:lc, hs])
        outs.append(o * (1.0 / den))
    return jnp.concatenate(outs, axis=-1).astype(BF16)


def _na_kernel(q_ref, kt_ref, v_ref, bias_ref, o_ref, *, lc, n_img_rows, ctx_first, use_max):
    i = pl.program_id(1)
    blk = i - 1 if ctx_first else i
    start_row = jnp.clip(NA_ROWS * blk - WIN_ROWS // 2, 0, n_img_rows - NA_KROWS)
    koff = pl.multiple_of(lc + start_row * GRID_W, 128)
    if ctx_first:
        @pl.when(i == 0)
        def _():
            o_ref[0] = _na_ctx_attend(q_ref[0], kt_ref, v_ref, lc, use_max)

        @pl.when(i > 0)
        def _():
            o_ref[0] = _na_attend(q_ref[0], kt_ref, v_ref, bias_ref, koff, lc, use_max)
    else:
        o_ref[0] = _na_attend(q_ref[0], kt_ref, v_ref, bias_ref, koff, lc, use_max)


def _na_attention(qn, knt, vn, bias, lc, with_ctx, use_max):
    b, s, _ = qn.shape
    tq = NA_ROWS * GRID_W
    assert tq == ROW_TILE and lc % tq == 0
    n_img_rows = (s - lc) // GRID_W
    nblk = n_img_rows // NA_ROWS
    off = 0 if with_ctx else lc // tq
    rows_out = s - off * tq
    first = 1 if with_ctx else 0

    def variant(bb, i):
        blk = i - first
        return (jnp.where(blk <= 0, 0, jnp.where(blk == nblk - 1, 2, 1)), 0, 0, 0)

    return pl.pallas_call(
        functools.partial(_na_kernel, lc=lc, n_img_rows=n_img_rows, ctx_first=with_ctx, use_max=use_max),
        out_shape=jax.ShapeDtypeStruct((b, rows_out, HW), BF16),
        grid=(b, rows_out // tq),
        in_specs=[pl.BlockSpec((1, tq, HW), lambda bb, i: (bb, i + off, 0)),
                  pl.BlockSpec((1, HW, s), lambda bb, i: (bb, 0, 0)),
                  pl.BlockSpec((1, s, HW), lambda bb, i: (bb, 0, 0)),
                  pl.BlockSpec((1, N_HEADS, tq, NA_KROWS * GRID_W), variant)],
        out_specs=pl.BlockSpec((1, tq, HW), lambda bb, i: (bb, i, 0)),
        compiler_params=_cparams(("arbitrary", "arbitrary")),
        name="nbr_attn_rowmax" if use_max else "nbr_attn",
    )(qn, knt, vn, bias)


def _na_bias(rpb, n_img_rows):
    a = jnp.arange(NA_ROWS)[:, None, None, None]
    cq = jnp.arange(GRID_W)[None, :, None, None]
    j = jnp.arange(NA_KROWS)[None, None, :, None]
    ck = jnp.arange(GRID_W)[None, None, None, :]
    cstart = jnp.clip(cq - WIN_COLS // 2, 0, GRID_W - WIN_COLS)
    colmask = (ck >= cstart) & (ck < cstart + WIN_COLS)
    dc = jnp.clip(ck - cq, -(WIN_COLS - 1), WIN_COLS - 1) + (WIN_COLS - 1)
    by_col = jnp.take(rpb.astype(F32), dc.reshape(-1), axis=2).reshape(N_HEADS, 2 * WIN_ROWS - 1, GRID_W, GRID_W)
    out = []
    for r0_minus_k, wstart in ((0, 0 * a), (WIN_ROWS // 2, a), (NA_KROWS - NA_ROWS, NA_KROWS - WIN_ROWS + 0 * a)):
        inwin = (j >= wstart) & (j < wstart + WIN_ROWS)
        dr = jnp.clip(j - r0_minus_k - a + (WIN_ROWS - 1), 0, 2 * WIN_ROWS - 2)
        vals = jnp.take(by_col, dr.reshape(-1), axis=1).reshape(N_HEADS, NA_ROWS, NA_KROWS, GRID_W, GRID_W)
        vals = jnp.transpose(vals, (0, 1, 3, 2, 4))
        vals = jnp.where((inwin & colmask)[None], vals * LOG2E, NEG)
        out.append(vals.reshape(N_HEADS, NA_ROWS * GRID_W, NA_KROWS * GRID_W))
    return jnp.stack(out)


def _s5_kernel(uf_ref, ub_ref, bmf_ref, cmf_ref, arf_ref, aif_ref, bmb_ref, cmb_ref, arb_ref, aib_ref,
               yf_ref, yb_ref, xf_scr, xb_scr, st_scr, io_scr):
    nb, tc, w = uf_ref.shape
    ns = arf_ref.shape[1]

    @pl.when(pl.program_id(0) == 0)
    def _():
        st_scr[...] = jnp.zeros_like(st_scr)

    nl = w // 128

    def drive(u_ref, bm_ref, x_scr):
        for b in range(nb):
            for c in range(nl):
                io_scr[c, pl.ds(b, tc, stride=nb), :] = u_ref[b, :, c * 128:(c + 1) * 128]
        u_tm = jnp.concatenate([io_scr[c] for c in range(nl)], axis=1)
        x_scr[...] = _dot(u_tm.astype(BF16), bm_ref[...])

    drive(uf_ref, bmf_ref, xf_scr)
    drive(ub_ref, bmb_ref, xb_scr)

    def advance(x_scr, a_re, a_im, s_re, s_im, tt):
        rows = pl.ds(pl.multiple_of(tt * nb, nb), nb)
        n_re = a_re * s_re - a_im * s_im + x_scr[rows, 0:ns]
        n_im = a_re * s_im + a_im * s_re + x_scr[rows, ns:2 * ns]
        x_scr[rows, 0:ns] = n_re
        x_scr[rows, ns:2 * ns] = n_im
        return n_re, n_im

    def step(t, carry):
        f_re, f_im, b_re, b_im = carry
        f_re, f_im = advance(xf_scr, arf_ref[...], aif_ref[...], f_re, f_im, t)
        b_re, b_im = advance(xb_scr, arb_ref[...], aib_ref[...], b_re, b_im, tc - 1 - t)
        return f_re, f_im, b_re, b_im

    init = (st_scr[0, :, 0:ns], st_scr[0, :, ns:2 * ns], st_scr[1, :, 0:ns], st_scr[1, :, ns:2 * ns])
    f_re, f_im, b_re, b_im = lax.fori_loop(0, tc, step, init, unroll=2)
    st_scr[0, :, 0:ns] = f_re
    st_scr[0, :, ns:2 * ns] = f_im
    st_scr[1, :, 0:ns] = b_re
    st_scr[1, :, ns:2 * ns] = b_im

    def readout(x_scr, cm_ref, y_ref):
        y_tm = _dot(x_scr[...].astype(BF16), cm_ref[...])
        for c in range(nl):
            io_scr[c] = y_tm[:, c * 128:(c + 1) * 128]
        for b in range(nb):
            for c in range(nl):
                y_ref[b, :, c * 128:(c + 1) * 128] = io_scr[c, pl.ds(b, tc, stride=nb), :]

    readout(xf_scr, cmf_ref, yf_ref)
    readout(xb_scr, cmb_ref, yb_ref)


def _s5_scan(u, fwd, bwd, lc):
    nb, s, w = u.shape
    tc = S5_CHUNK
    nc, ncc = s // tc, lc // tc
    ns = fwd[2].shape[1]
    chunk_f = lambda j: (0, j, 0)
    chunk_b = lambda j: (0, jnp.where(j < ncc, ncc - 1 - j, nc - 1 - (j - ncc)), 0)
    const = lambda j: (0, 0)
    pspecs = [pl.BlockSpec((w, 2 * ns), const), pl.BlockSpec((2 * ns, w), const),
              pl.BlockSpec((nb, ns), const), pl.BlockSpec((nb, ns), const)]
    out = jax.ShapeDtypeStruct((nb, s, w), F32)
    return pl.pallas_call(
        _s5_kernel,
        out_shape=(out, out),
        grid=(nc,),
        in_specs=[pl.BlockSpec((nb, tc, w), chunk_f), pl.BlockSpec((nb, tc, w), chunk_b)] + pspecs + pspecs,
        out_specs=(pl.BlockSpec((nb, tc, w), chunk_f), pl.BlockSpec((nb, tc, w), chunk_b)),
        scratch_shapes=[pltpu.VMEM((tc * nb, 2 * ns), F32), pltpu.VMEM((tc * nb, 2 * ns), F32),
                        pltpu.VMEM((2, nb, 2 * ns), F32), pltpu.VMEM((w // 128, tc * nb, 128), F32)],
        compiler_params=_cparams(("arbitrary",)),
        name="s5_scan",
    )(u, u, *fwd, *bwd)


def _s5_params(a_re, a_im, log_dt, b_re, b_im, c_re, c_im, nb):
    g, n, p = S5_GROUPS, S5_STATE, S5_GROUP_CH
    lr, li = a_re.astype(F32), a_im.astype(F32)
    dt = jnp.exp(log_dt.astype(F32))[..., None]
    mag = jnp.exp(lr * dt)
    ab_r, ab_i = mag * jnp.cos(li * dt), mag * jnp.sin(li * dt)
    den = lr * lr + li * li
    cf_r = ((ab_r - 1.0) * lr + ab_i * li) / den
    cf_i = (ab_i * lr - (ab_r - 1.0) * li) / den
    br, bi = b_re.astype(F32), b_im.astype(F32)
    bb_r = cf_r[..., None] * br - cf_i[..., None] * bi
    bb_i = cf_r[..., None] * bi + cf_i[..., None] * br
    eye = jnp.eye(g, dtype=F32)
    out = []
    for k in range(2):
        b_r = jnp.einsum('gnp,gh->gphn', bb_r[k], eye).reshape(g * p, g * n)
        b_i = jnp.einsum('gnp,gh->gphn', bb_i[k], eye).reshape(g * p, g * n)
        bmat = jnp.concatenate([b_r, b_i], axis=1).astype(BF16)
        ct = jnp.transpose(c_re[k].astype(F32), (0, 2, 1))
        ci = jnp.transpose(c_im[k].astype(F32), (0, 2, 1))
        c_r = jnp.einsum('gnp,gh->gnhp', ct, eye).reshape(g * n, g * p)
        c_i = jnp.einsum('gnp,gh->gnhp', ci, eye).reshape(g * n, g * p)
        cmat = jnp.concatenate([c_r, -c_i], axis=0).astype(BF16)
        ar = jnp.broadcast_to(ab_r[k].reshape(1, g * n), (nb, g * n))
        ai = jnp.broadcast_to(ab_i[k].reshape(1, g * n), (nb, g * n))
        out.append((bmat, cmat, ar, ai))
    return out


def _out_kernel(oa_ref, on_ref, yf_ref, yb_ref, u_ref, x_ref, mod_ref, gffn_ref, wo_ref, dskip_ref, wglu_ref, bglu_ref,
                wr_ref, xo_ref, h2_ref, lg_ref):
    mod = mod_ref[0, 0]
    nt = lambda a, bb: lax.dot_general(a, bb, (((1,), (1,)), ((), ())), preferred_element_type=F32)
    wh, wl = _split_bf16(wr_ref[...])
    half = x_ref.shape[1] // 2
    for r in (slice(0, half), slice(half, 2 * half)):
        g = _gelu_tanh(yf_ref[0, r, :] + yb_ref[0, r, :] + dskip_ref[...] * u_ref[0, r, :])
        ob = g * _sigmoid(_dot(g.astype(BF16), wglu_ref[...]) + bglu_ref[...])
        mix = (_dot(oa_ref[0, r, :], wo_ref[0:HW, :]) + _dot(ob.astype(BF16), wo_ref[HW:HW + B_WIDTH, :])
               + _dot(on_ref[0, r, :], wo_ref[HW + B_WIDTH:, :]))
        x = x_ref[0, r, :] + mod[2:3] * mix
        xo_ref[0, r, :] = x
        ms = jnp.mean(x * x, axis=-1, keepdims=True)
        h2 = x * lax.rsqrt(ms + EPS) * gffn_ref[...] * (1.0 + mod[4:5]) + mod[3:4]
        h2_ref[0, r, :] = h2.astype(BF16)
        hh, hl = _split_bf16(h2)
        lg_ref[0, :, r] = nt(wh, hh) + nt(wh, hl) + nt(wl, hh)


def _out_projection(oa, on, y_f, y_b, u, xs, modall, g_ffn, w_out_bf, d_skip, w_glu_bf, b_glu, w_router_pad, lc,
                    with_ctx):
    b, s, d = xs.shape
    tm = ROW_TILE
    off = 0 if with_ctx else lc // tm
    rows_out = s - off * tm
    full = lambda bb, i: (bb, i + off, 0)
    outr = lambda bb, i: (bb, i, 0)
    const = lambda bb, i: (0, 0)
    ne = w_router_pad.shape[0]
    return pl.pallas_call(
        _out_kernel,
        out_shape=(jax.ShapeDtypeStruct((b, rows_out, d), F32),
                   jax.ShapeDtypeStruct((b, rows_out, d), BF16),
                   jax.ShapeDtypeStruct((b, ne, rows_out), F32)),
        grid=(b, rows_out // tm),
        in_specs=[pl.BlockSpec((1, tm, HW), outr),
                  pl.BlockSpec((1, tm, HW), outr),
                  pl.BlockSpec((1, tm, B_WIDTH), full),
                  pl.BlockSpec((1, tm, B_WIDTH), full),
                  pl.BlockSpec((1, tm, B_WIDTH), full),
                  pl.BlockSpec((1, tm, d), full),
                  pl.BlockSpec((1, 1, 6, d), lambda bb, i: (bb, jnp.minimum(i + off, 1), 0, 0)),
                  pl.BlockSpec((1, d), const),
                  pl.BlockSpec((d, d), const),
                  pl.BlockSpec((1, B_WIDTH), const),
                  pl.BlockSpec((B_WIDTH, B_WIDTH), const),
                  pl.BlockSpec((1, B_WIDTH), const),
                  pl.BlockSpec((ne, d), const)],
        out_specs=(pl.BlockSpec((1, tm, d), outr), pl.BlockSpec((1, tm, d), outr),
                   pl.BlockSpec((1, ne, tm), lambda bb, i: (bb, 0, i))),
        compiler_params=_cparams(("arbitrary", "arbitrary")),
        name="out_proj",
    )(oa, on, y_f, y_b, u, xs, modall, g_ffn.reshape(1, d), w_out_bf, d_skip.reshape(1, -1), w_glu_bf,
      b_glu.reshape(1, -1), w_router_pad)


def _swiglu(h, wg, wu):
    a = _dot(h, wg)
    return a * _sigmoid(a) * _dot(h, wu)


def _moe_kernel(h_ref, gates_ref, x_ref, gt_ref, wg_ref, wu_ref, wd_ref, sg_ref, su_ref, sd_ref, o_ref, acc_ref):
    e = pl.program_id(1)
    n_routed = pl.num_programs(1) - 1

    @pl.when(e == 0)
    def _():
        acc_ref[...] = jnp.zeros_like(acc_ref)

    @pl.when(e < n_routed)
    def _():
        h = h_ref[...]
        gsel = pltpu.roll(gates_ref[...], lax.rem(128 - e * MOE_EB, 128), 1)
        hid = [(_swiglu(h, wg_ref[j], wu_ref[j]) * gsel[:, j:j + 1]).astype(BF16) for j in range(MOE_EB)]
        acc_ref[...] += _dot(jnp.concatenate(hid, axis=-1), wd_ref[...])

    @pl.when(e == n_routed)
    def _():
        for j in range(h_ref.shape[0] // ROW_TILE):
            rows = slice(j * ROW_TILE, (j + 1) * ROW_TILE)
            hs = _swiglu(h_ref[rows, :], sg_ref[...], su_ref[...]).astype(BF16)
            y = acc_ref[rows, :] + _dot(hs, sd_ref[...])
            o_ref[rows, :] = x_ref[rows, :] + gt_ref[0, j:j + 1, :] * y


def _moe_tile(t):
    return max(m for m in range(ROW_TILE, MOE_TILE + 1, ROW_TILE) if t % m == 0)


def _moe(h2, gates, x, gt_rows, wg, wu, wd, sg, su, sd):
    t, d = h2.shape
    tm = _moe_tile(t)
    n_routed = wg.shape[0] // MOE_EB
    nsub = tm // ROW_TILE
    step = lambda e: jnp.minimum(e, n_routed - 1)
    const = lambda i, e: (0, 0)
    return pl.pallas_call(
        _moe_kernel,
        out_shape=jax.ShapeDtypeStruct((t, d), F32),
        grid=(t // tm, n_routed + 1),
        in_specs=[pl.BlockSpec((tm, d), lambda i, e: (i, 0)),
                  pl.BlockSpec((tm, 128), lambda i, e: (i, 0)),
                  pl.BlockSpec((tm, d), lambda i, e: (i, 0)),
                  pl.BlockSpec((1, nsub, d), lambda i, e: (i, 0, 0)),
                  pl.BlockSpec((MOE_EB, d, EXPERT_DIM), lambda i, e: (step(e), 0, 0)),
                  pl.BlockSpec((MOE_EB, d, EXPERT_DIM), lambda i, e: (step(e), 0, 0)),
                  pl.BlockSpec((MOE_EB * EXPERT_DIM, d), lambda i, e: (step(e), 0)),
                  pl.BlockSpec((d, EXPERT_DIM), const),
                  pl.BlockSpec((d, EXPERT_DIM), const),
                  pl.BlockSpec((EXPERT_DIM, d), const)],
        out_specs=pl.BlockSpec((tm, d), lambda i, e: (i, 0)),
        scratch_shapes=[pltpu.VMEM((tm, d), F32)],
        compiler_params=_cparams(("arbitrary", "arbitrary")),
        name="moe_ffn",
    )(h2, gates, x, gt_rows, wg, wu, wd, sg, su, sd)


def _route_kernel(lg_ref, bias_ref, o_ref):
    gsz = N_EXPERTS // N_GROUPS
    tn = lg_ref.shape[2]
    ninf = -jnp.inf
    jidx = lax.broadcasted_iota(jnp.int32, (gsz, tn), 0)
    scores, biased, gscore = [], [], []
    for g in range(N_GROUPS):
        rows = slice(g * gsz, (g + 1) * gsz)
        sc = _sigmoid(lg_ref[0, rows, :])
        bi = sc + bias_ref[rows, :]
        m1 = jnp.max(bi, axis=0, keepdims=True)
        first = jnp.min(jnp.where(bi == m1, jidx, gsz), axis=0, keepdims=True)
        m2 = jnp.max(jnp.where(jidx == first, ninf, bi), axis=0, keepdims=True)
        scores.append(sc)
        biased.append(bi)
        gscore.append(m1 + m2)
    masked = []
    for g in range(N_GROUPS):
        rank = jnp.zeros((1, tn), F32)
        for g2 in range(N_GROUPS):
            if g2 != g:
                ahead = (gscore[g2] >= gscore[g]) if g2 < g else (gscore[g2] > gscore[g])
                rank = rank + jnp.where(ahead, 1.0, 0.0)
        keep = jnp.broadcast_to(rank, (gsz, tn)) < TOPK_GROUPS
        masked.append(jnp.where(keep, biased[g], ninf))
    ranks = [jnp.zeros((gsz, tn), F32) for _ in range(N_GROUPS)]
    for g2 in range(N_GROUPS):
        for j2 in range(gsz):
            other = jnp.broadcast_to(masked[g2][j2:j2 + 1, :], (gsz, tn))
            for g in range(N_GROUPS):
                ge = jnp.where(other >= masked[g], 1.0, 0.0)
                gt = jnp.where(other > masked[g], 1.0, 0.0)
                if g2 < g:
                    ahead = ge
                elif g2 > g:
                    ahead = gt
                else:
                    ahead = jnp.where(jidx > j2, ge, gt)
                ranks[g] = ranks[g] + ahead
    picked = [jnp.where(ranks[g] < TOP_K, scores[g], 0.0) for g in range(N_GROUPS)]
    den = sum(jnp.sum(p, axis=0, keepdims=True) for p in picked)
    scale = ROUTED_SCALE / den
    shared_row = jnp.where(lax.broadcasted_iota(jnp.int32, (128 - N_EXPERTS, tn), 0) == 0, 1.0, 0.0)
    gates_t = jnp.concatenate([p * scale for p in picked] + [shared_row], axis=0)
    o_ref[...] = gates_t.T


def _route(logits_t, e_bias):
    b, ne, rows = logits_t.shape
    tn = ROW_TILE
    nt = rows // tn
    bias = jnp.broadcast_to(e_bias.astype(F32)[:, None], (N_EXPERTS, tn))
    return pl.pallas_call(
        _route_kernel,
        out_shape=jax.ShapeDtypeStruct((b * rows, ne), F32),
        grid=(b, nt),
        in_specs=[pl.BlockSpec((1, ne, tn), lambda bb, i: (bb, 0, i)),
                  pl.BlockSpec((N_EXPERTS, tn), lambda bb, i: (0, 0))],
        out_specs=pl.BlockSpec((tn, ne), lambda bb, i: (bb * nt + i, 0)),
        compiler_params=_cparams(("arbitrary", "arbitrary")),
        name="moe_route",
    )(logits_t, bias)


def kernel(x, c, ctx, c_ctx, w_mod, b_mod, g_mix, g_ffn, w_in, w_out, a_gq, a_gk, a_lambda, a_gsub, n_gq, n_gk, n_rpb, s5_a_re, s5_a_im, s5_log_dt, s5_b_re, s5_b_im, s5_c_re, s5_c_im, s5_d, s5_w_glu, s5_b_glu, w_router, e_bias, w_gate, w_up, w_down, ws_gate, ws_up, ws_down):
    b, l, d = x.shape
    lc = ctx.shape[1]
    s = lc + l
    depth = w_mod.shape[0]
    n_img_rows = l // GRID_W

    xs = jnp.concatenate([ctx, x], axis=1).astype(F32)
    cond_rows = jnp.zeros((16, d), F32).at[:b].set(c.astype(F32)).at[b].set(c_ctx.astype(F32))
    cos, sa, sb = _rope_tables(s, lc)
    tables = (cos, sa, sb, _block_ones(A_QK_DIM), _block_ones(N_HEAD_DIM))

    for layer in range(depth):
        last = layer == depth - 1
        with_ctx = not last
        lam_init = 0.8 - 0.6 * math.exp(-0.3 * layer)

        mod = _modulation(cond_rows, w_mod[layer].astype(F32), b_mod[layer].astype(F32))
        mod_lat = mod[:b].reshape(b, 1, 6, d)
        mod_ctx = jnp.broadcast_to(mod[b].reshape(1, 1, 6, d), (b, 1, 6, d))
        modall = jnp.concatenate([mod_ctx, mod_lat], axis=1)

        gains = ((jnp.tile(a_gq[layer].astype(F32), HW // A_QK_DIM) * (A_QK_DIM ** -0.5 * LOG2E)).reshape(1, HW),
                 jnp.tile(a_gk[layer].astype(F32), HW // A_QK_DIM).reshape(1, HW),
                 (jnp.tile(n_gq[layer].astype(F32), N_HEADS) * (N_HEAD_DIM ** -0.5 * LOG2E)).reshape(1, HW),
                 jnp.tile(n_gk[layer].astype(F32), N_HEADS).reshape(1, HW))
        qa, kat, va, qn, knt, vn, u = _in_projection(xs, modall, g_mix[layer].astype(F32), w_in[layer].astype(BF16),
                                                     tables, gains, lc)

        lv = a_lambda[layer].astype(F32)
        lam = (jnp.exp(jnp.sum(lv[0] * lv[1])) - jnp.exp(jnp.sum(lv[2] * lv[3])) + lam_init).reshape(1, 1)
        gsub_t = (jnp.tile(a_gsub[layer].astype(F32), A_HEADS) * (1.0 - lam_init)).reshape(1, HW)
        a_bound = BOUND_SLACK * A_QK_DIM * jnp.max(jnp.abs(gains[0])) * jnp.max(jnp.abs(gains[1]))
        oa = lax.cond(a_bound <= SCORE_BOUND_LIMIT,
                      lambda *a: _diff_attention(*a, lc, with_ctx, False),
                      lambda *a: _diff_attention(*a, lc, with_ctx, True), qa, kat, va, lam, gsub_t)

        n_bound = (BOUND_SLACK * N_HEAD_DIM * jnp.max(jnp.abs(gains[2])) * jnp.max(jnp.abs(gains[3]))
                   + LOG2E * jnp.max(jnp.abs(n_rpb[layer].astype(F32))))
        on = lax.cond(n_bound <= SCORE_BOUND_LIMIT,
                      lambda *a: _na_attention(*a, lc, with_ctx, False),
                      lambda *a: _na_attention(*a, lc, with_ctx, True),
                      qn, knt, vn, _na_bias(n_rpb[layer], n_img_rows))

        s5_fwd, s5_bwd = _s5_params(s5_a_re[layer], s5_a_im[layer], s5_log_dt[layer], s5_b_re[layer],
                                    s5_b_im[layer], s5_c_re[layer], s5_c_im[layer], b)
        y_f, y_b = _s5_scan(u, s5_fwd, s5_bwd, lc)

        w_router_pad = jnp.zeros((128, d), F32).at[:N_EXPERTS].set(w_router[layer].astype(F32).T)
        x_new, h2, logits_t = _out_projection(oa, on, y_f, y_b, u, xs, modall, g_ffn[layer].astype(F32),
                                              w_out[layer].astype(BF16), s5_d[layer].astype(F32),
                                              s5_w_glu[layer].astype(BF16), s5_b_glu[layer].astype(F32),
                                              w_router_pad, lc, with_ctx)
        rows = x_new.shape[1]
        t = b * rows
        gates = _route(logits_t, e_bias[layer])

        gt2 = modall[:, :, 5, :]
        tiles_per_batch = rows // ROW_TILE
        if with_ctx:
            nctx = lc // ROW_TILE
            sel = (jnp.arange(tiles_per_batch) >= nctx).astype(jnp.int32)
        else:
            sel = jnp.ones((tiles_per_batch,), jnp.int32)
        gt_rows = gt2[:, sel, :].reshape(t // _moe_tile(t), _moe_tile(t) // ROW_TILE, d)

        out = _moe(h2.reshape(t, d), gates, x_new.reshape(t, d), gt_rows,
                   w_gate[layer].astype(BF16), w_up[layer].astype(BF16), w_down[layer].astype(BF16).reshape(-1, d),
                   ws_gate[layer].astype(BF16), ws_up[layer].astype(BF16), ws_down[layer].astype(BF16))
        xs = out.reshape(b, rows, d)

    return xs.astype(x.dtype)
```

```python
import functools
import math

import jax
import jax.numpy as jnp
from jax import lax
from jax.experimental import pallas as pl
from jax.experimental.pallas import tpu as pltpu

F32 = jnp.float32
BF16 = jnp.bfloat16

D_MODEL = 1024
GRID_W = 64
EPS = 1e-6
A_HEADS = 6
A_QK_DIM = 32
A_V_DIM = 64
ROPE_THETA = 10000.0
S5_GROUPS = 16
S5_GROUP_CH = 16
S5_STATE = 64
N_HEADS = 6
N_HEAD_DIM = 64
WIN_ROWS = 8
WIN_COLS = 16
HW = 384
B_WIDTH = 256
Q_COLS = 768
IN_COLS = 2560
N_EXPERTS = 64
TOP_K = 8
N_GROUPS = 8
TOPK_GROUPS = 4
EXPERT_DIM = 256
ROUTED_SCALE = 2.5

ROW_TILE = 256
NA_ROWS = 4
NA_KROWS = 12
S5_CHUNK = 128
MOE_TILE = 1024
MOE_EB = 4
N_CLASSES = 1 << N_GROUPS
SUBROWS = 8
SLOT_ROWS = 16
NEG = -1e30
LOG2E = math.log2(math.e)
SCORE_BOUND_LIMIT = 50.0
BOUND_SLACK = 1.05
VMEM_LIMIT = 56 * 1024 * 1024


def _sigmoid(x):
    return 1.0 / (1.0 + jnp.exp(-x))


def _gelu_tanh(x):
    return 0.5 * x * (1.0 + jnp.tanh(math.sqrt(2.0 / math.pi) * (x + 0.044715 * (x * x * x))))


def _split_bf16(a):
    hi = a.astype(BF16)
    lo = (a - hi.astype(F32)).astype(BF16)
    return hi, lo


def _dot(a, b):
    return jnp.dot(a, b, preferred_element_type=F32)


def _dot3(a, b):
    ah, al = _split_bf16(a)
    bh, bl = _split_bf16(b)
    return _dot(ah, bh) + _dot(ah, bl) + _dot(al, bh)


def _cparams(sem):
    return pltpu.CompilerParams(dimension_semantics=sem, vmem_limit_bytes=VMEM_LIMIT)


def _mod_kernel(c_ref, w_ref, b_ref, o_ref):
    c = c_ref[...]
    cond = c * _sigmoid(c)
    o_ref[...] = _dot3(cond, w_ref[...]) + b_ref[...]


def _modulation(cond_rows, w_mod, b_mod):
    r, d = cond_rows.shape
    n = w_mod.shape[1]
    tn = 1536
    return pl.pallas_call(
        _mod_kernel,
        out_shape=jax.ShapeDtypeStruct((r, n), F32),
        grid=(n // tn,),
        in_specs=[pl.BlockSpec((r, d), lambda j: (0, 0)),
                  pl.BlockSpec((d, tn), lambda j: (0, j)),
                  pl.BlockSpec((1, tn), lambda j: (0, j))],
        out_specs=pl.BlockSpec((r, tn), lambda j: (0, j)),
        compiler_params=_cparams(("arbitrary",)),
        name="adaln_mod",
    )(cond_rows, w_mod, b_mod.reshape(1, n))


def _group_rms(t, ones_ref, gain_ref, group):
    ms = _dot((t * t).astype(BF16), ones_ref[...]) * (1.0 / group)
    return t * lax.rsqrt(ms + EPS) * gain_ref[...]


def _rope(t, cos_ref, sa_ref, sb_ref):
    up = pltpu.roll(t, HW - 8, 1)
    dn = pltpu.roll(t, 8, 1)
    return t * cos_ref[...] + up * sa_ref[...] + dn * sb_ref[...]


def _proj_kernel(x_ref, mod_ref, g_ref, w_ref, cos_ref, sa_ref, sb_ref, ones32_ref, ones64_ref,
                 gqa_ref, gka_ref, gqn_ref, gkn_ref,
                 qa_ref, kat_ref, va_ref, qn_ref, knt_ref, vn_ref, u_ref):
    x = x_ref[0]
    mod = mod_ref[0, 0]
    ms = jnp.mean(x * x, axis=-1, keepdims=True)
    h = x * lax.rsqrt(ms + EPS) * g_ref[...] * (1.0 + mod[1:2]) + mod[0:1]
    hb = h.astype(BF16)

    def sec(a, b):
        return _dot(hb, w_ref[:, a:b])

    qa = _rope(_group_rms(sec(0, 384), ones32_ref, gqa_ref, A_QK_DIM), cos_ref, sa_ref, sb_ref)
    qa_ref[0] = qa.astype(BF16)
    qn_ref[0] = _group_rms(sec(384, 768), ones64_ref, gqn_ref, N_HEAD_DIM).astype(BF16)
    ka = _rope(_group_rms(sec(768, 1152), ones32_ref, gka_ref, A_QK_DIM), cos_ref, sa_ref, sb_ref)
    kat_ref[0] = ka.T.astype(BF16)
    va_ref[0] = sec(1152, 1536).astype(BF16)
    kn = _group_rms(sec(1536, 1920), ones64_ref, gkn_ref, N_HEAD_DIM)
    knt_ref[0] = kn.T.astype(BF16)
    vn_ref[0] = sec(1920, 2304).astype(BF16)
    u_ref[0] = sec(2304, 2560)


def _in_projection(xs, modall, g_mix, w_in_bf, tables, gains, lc):
    b, s, d = xs.shape
    tm = ROW_TILE
    cos, sa, sb, ones32, ones64 = tables
    row = lambda i, bb: (bb, i, 0)
    tab = lambda i, bb: (i, 0)
    const2 = lambda i, bb: (0, 0)
    act = lambda w, dt: jax.ShapeDtypeStruct((b, s, w), dt)
    act_t = jax.ShapeDtypeStruct((b, HW, s), BF16)
    return pl.pallas_call(
        _proj_kernel,
        out_shape=(act(HW, BF16), act_t, act(HW, BF16), act(HW, BF16), act_t, act(HW, BF16), act(B_WIDTH, F32)),
        grid=(s // tm, b),
        in_specs=[pl.BlockSpec((1, tm, d), row),
                  pl.BlockSpec((1, 1, 6, d), lambda i, bb: (bb, jnp.minimum(i, 1), 0, 0)),
                  pl.BlockSpec((1, d), const2),
                  pl.BlockSpec((d, IN_COLS), const2),
                  pl.BlockSpec((tm, HW), tab), pl.BlockSpec((tm, HW), tab), pl.BlockSpec((tm, HW), tab),
                  pl.BlockSpec((HW, HW), const2), pl.BlockSpec((HW, HW), const2),
                  pl.BlockSpec((1, HW), const2), pl.BlockSpec((1, HW), const2),
                  pl.BlockSpec((1, HW), const2), pl.BlockSpec((1, HW), const2)],
        out_specs=(pl.BlockSpec((1, tm, HW), row),
                   pl.BlockSpec((1, HW, tm), lambda i, bb: (bb, 0, i)),
                   pl.BlockSpec((1, tm, HW), row),
                   pl.BlockSpec((1, tm, HW), row),
                   pl.BlockSpec((1, HW, tm), lambda i, bb: (bb, 0, i)),
                   pl.BlockSpec((1, tm, HW), row),
                   pl.BlockSpec((1, tm, B_WIDTH), row)),
        compiler_params=_cparams(("arbitrary", "arbitrary")),
        name="in_proj",
    )(xs, modall, g_mix.reshape(1, d), w_in_bf, cos, sa, sb, ones32, ones64, *gains)


def _rope_tables(s, lc):
    p = jnp.arange(s)
    pos = jnp.maximum(p - lc, 0)
    rows = (pos // GRID_W).astype(F32)
    cols = (pos % GRID_W).astype(F32)
    lane = jnp.arange(HW)
    j32 = lane % A_QK_DIM
    half = j32 // 16
    i16 = j32 % 16
    nf = 8
    inv = ROPE_THETA ** (-(i16 % nf).astype(F32) / nf)
    coord = jnp.where(half[None, :] == 0, rows[:, None], cols[:, None])
    ang = coord * inv[None, :]
    is_lat = (p >= lc)[:, None]
    second = (i16 >= nf)[None, :]
    cos = jnp.where(is_lat, jnp.cos(ang), 1.0)
    sin = jnp.where(is_lat, jnp.sin(ang), 0.0)
    sa = jnp.where(second, 0.0, -sin)
    sb = jnp.where(second, sin, 0.0)
    return cos.astype(F32), sa.astype(F32), sb.astype(F32)


def _block_ones(group):
    g = jnp.arange(HW) // group
    return (g[:, None] == g[None, :]).astype(BF16)


def _diff_attend(q_all, kt_ref, v_ref, lam, gsub_ref, sk, use_max):
    outs = []
    for h in range(A_HEADS):
        v = v_ref[0, 0:sk, h * A_V_DIM:(h + 1) * A_V_DIM]
        parts = []
        for sub in range(2):
            hs = 2 * h + sub
            off = hs * A_QK_DIM
            sc = _dot(q_all[:, off:off + A_QK_DIM], kt_ref[0, off:off + A_QK_DIM, 0:sk])
            e = jnp.exp2(sc - jnp.max(sc, axis=-1, keepdims=True)) if use_max else jnp.exp2(sc)
            parts.append(_dot(e.astype(BF16), v) * (1.0 / jnp.sum(e, axis=-1, keepdims=True)))
        o = parts[0] - lam * parts[1]
        outs.append(o * lax.rsqrt(jnp.mean(o * o, axis=-1, keepdims=True) + EPS))
    return (jnp.concatenate(outs, axis=-1) * gsub_ref[...]).astype(BF16)


def _diff_attn_kernel(q_ref, kt_ref, v_ref, lam_ref, gsub_ref, o_ref, *, lc, ctx_first, use_max):
    lam = lam_ref[...]
    s = kt_ref.shape[2]
    if ctx_first:
        i = pl.program_id(1)

        @pl.when(i == 0)
        def _():
            o_ref[0] = _diff_attend(q_ref[0], kt_ref, v_ref, lam, gsub_ref, lc, use_max)

        @pl.when(i > 0)
        def _():
            o_ref[0] = _diff_attend(q_ref[0], kt_ref, v_ref, lam, gsub_ref, s, use_max)
    else:
        o_ref[0] = _diff_attend(q_ref[0], kt_ref, v_ref, lam, gsub_ref, s, use_max)


def _diff_attention(qa, kat, va, lam, gsub_t, lc, with_ctx, use_max):
    b, s, _ = qa.shape
    tq = ROW_TILE
    off = 0 if with_ctx else lc // tq
    rows_out = s - off * tq
    return pl.pallas_call(
        functools.partial(_diff_attn_kernel, lc=lc, ctx_first=with_ctx, use_max=use_max),
        out_shape=jax.ShapeDtypeStruct((b, rows_out, HW), BF16),
        grid=(b, rows_out // tq),
        in_specs=[pl.BlockSpec((1, tq, HW), lambda bb, i: (bb, i + off, 0)),
                  pl.BlockSpec((1, HW, s), lambda bb, i: (bb, 0, 0)),
                  pl.BlockSpec((1, s, HW), lambda bb, i: (bb, 0, 0)),
                  pl.BlockSpec((1, 1), lambda bb, i: (0, 0)),
                  pl.BlockSpec((1, HW), lambda bb, i: (0, 0))],
        out_specs=pl.BlockSpec((1, tq, HW), lambda bb, i: (bb, i, 0)),
        compiler_params=_cparams(("arbitrary", "arbitrary")),
        name="diff_attn_rowmax" if use_max else "diff_attn",
    )(qa, kat, va, lam, gsub_t)


def _na_ctx_attend(q_all, kt_ref, v_ref, lc, use_max):
    outs = []
    for h in range(N_HEADS):
        hs = slice(h * N_HEAD_DIM, (h + 1) * N_HEAD_DIM)
        sc = _dot(q_all[:, hs], kt_ref[0, hs, 0:lc])
        e = jnp.exp2(sc - jnp.max(sc, axis=-1, keepdims=True)) if use_max else jnp.exp2(sc)
        o = _dot(e.astype(BF16), v_ref[0, 0:lc, hs])
        outs.append(o * (1.0 / jnp.sum(e, axis=-1, keepdims=True)))
    return jnp.concatenate(outs, axis=-1).astype(BF16)


def _na_attend(q_all, kt_ref, v_ref, bias_ref, koff, lc, use_max):
    nk = NA_KROWS * GRID_W
    outs = []
    for h in range(N_HEADS):
        hs = slice(h * N_HEAD_DIM, (h + 1) * N_HEAD_DIM)
        q = q_all[:, hs]
        s_loc = _dot(q, kt_ref[0, hs, pl.ds(koff, nk)]) + bias_ref[0, h]
        s_ctx = _dot(q, kt_ref[0, hs, 0:lc])
        if use_max:
            m = jnp.maximum(jnp.max(s_loc, axis=-1, keepdims=True), jnp.max(s_ctx, axis=-1, keepdims=True))
            s_loc, s_ctx = s_loc - m, s_ctx - m
        e_loc = jnp.exp2(s_loc)
        e_ctx = jnp.exp2(s_ctx)
        den = jnp.sum(e_loc, axis=-1, keepdims=True) + jnp.sum(e_ctx, axis=-1, keepdims=True)
        o = _dot(e_loc.astype(BF16), v_ref[0, pl.ds(koff, nk), hs]) + _dot(e_ctx.astype(BF16), v_ref[0, 0:lc, hs])
        outs.append(o * (1.0 / den))
    return jnp.concatenate(outs, axis=-1).astype(BF16)


def _na_kernel(q_ref, kt_ref, v_ref, bias_ref, o_ref, *, lc, n_img_rows, ctx_first, use_max):
    i = pl.program_id(1)
    blk = i - 1 if ctx_first else i
    start_row = jnp.clip(NA_ROWS * blk - WIN_ROWS // 2, 0, n_img_rows - NA_KROWS)
    koff = pl.multiple_of(lc + start_row * GRID_W, 128)
    if ctx_first:
        @pl.when(i == 0)
        def _():
            o_ref[0] = _na_ctx_attend(q_ref[0], kt_ref, v_ref, lc, use_max)

        @pl.when(i > 0)
        def _():
            o_ref[0] = _na_attend(q_ref[0], kt_ref, v_ref, bias_ref, koff, lc, use_max)
    else:
        o_ref[0] = _na_attend(q_ref[0], kt_ref, v_ref, bias_ref, koff, lc, use_max)


def _na_attention(qn, knt, vn, bias, lc, with_ctx, use_max):
    b, s, _ = qn.shape
    tq = NA_ROWS * GRID_W
    assert tq == ROW_TILE and lc % tq == 0
    n_img_rows = (s - lc) // GRID_W
    nblk = n_img_rows // NA_ROWS
    off = 0 if with_ctx else lc // tq
    rows_out = s - off * tq
    first = 1 if with_ctx else 0

    def variant(bb, i):
        blk = i - first
        return (jnp.where(blk <= 0, 0, jnp.where(blk == nblk - 1, 2, 1)), 0, 0, 0)

    return pl.pallas_call(
        functools.partial(_na_kernel, lc=lc, n_img_rows=n_img_rows, ctx_first=with_ctx, use_max=use_max),
        out_shape=jax.ShapeDtypeStruct((b, rows_out, HW), BF16),
        grid=(b, rows_out // tq),
        in_specs=[pl.BlockSpec((1, tq, HW), lambda bb, i: (bb, i + off, 0)),
                  pl.BlockSpec((1, HW, s), lambda bb, i: (bb, 0, 0)),
                  pl.BlockSpec((1, s, HW), lambda bb, i: (bb, 0, 0)),
                  pl.BlockSpec((1, N_HEADS, tq, NA_KROWS * GRID_W), variant)],
        out_specs=pl.BlockSpec((1, tq, HW), lambda bb, i: (bb, i, 0)),
        compiler_params=_cparams(("arbitrary", "arbitrary")),
        name="nbr_attn_rowmax" if use_max else "nbr_attn",
    )(qn, knt, vn, bias)


def _na_bias(rpb, n_img_rows):
    a = jnp.arange(NA_ROWS)[:, None, None, None]
    cq = jnp.arange(GRID_W)[None, :, None, None]
    j = jnp.arange(NA_KROWS)[None, None, :, None]
    ck = jnp.arange(GRID_W)[None, None, None, :]
    cstart = jnp.clip(cq - WIN_COLS // 2, 0, GRID_W - WIN_COLS)
    colmask = (ck >= cstart) & (ck < cstart + WIN_COLS)
    dc = jnp.clip(ck - cq, -(WIN_COLS - 1), WIN_COLS - 1) + (WIN_COLS - 1)
    by_col = jnp.take(rpb.astype(F32), dc.reshape(-1), axis=2).reshape(N_HEADS, 2 * WIN_ROWS - 1, GRID_W, GRID_W)
    out = []
    for r0_minus_k, wstart in ((0, 0 * a), (WIN_ROWS // 2, a), (NA_KROWS - NA_ROWS, NA_KROWS - WIN_ROWS + 0 * a)):
        inwin = (j >= wstart) & (j < wstart + WIN_ROWS)
        dr = jnp.clip(j - r0_minus_k - a + (WIN_ROWS - 1), 0, 2 * WIN_ROWS - 2)
        vals = jnp.take(by_col, dr.reshape(-1), axis=1).reshape(N_HEADS, NA_ROWS, NA_KROWS, GRID_W, GRID_W)
        vals = jnp.transpose(vals, (0, 1, 3, 2, 4))
        vals = jnp.where((inwin & colmask)[None], vals * LOG2E, NEG)
        out.append(vals.reshape(N_HEADS, NA_ROWS * GRID_W, NA_KROWS * GRID_W))
    return jnp.stack(out)


def _s5_kernel(uf_ref, ub_ref, bmf_ref, cmf_ref, arf_ref, aif_ref, bmb_ref, cmb_ref, arb_ref, aib_ref,
               yf_ref, yb_ref, xf_scr, xb_scr, st_scr, io_scr):
    nb, tc, w = uf_ref.shape
    ns = arf_ref.shape[1]

    @pl.when(pl.program_id(0) == 0)
    def _():
        st_scr[...] = jnp.zeros_like(st_scr)

    nl = w // 128

    def drive(u_ref, bm_ref, x_scr):
        for b in range(nb):
            for c in range(nl):
                io_scr[c, pl.ds(b, tc, stride=nb), :] = u_ref[b, :, c * 128:(c + 1) * 128]
        u_tm = jnp.concatenate([io_scr[c] for c in range(nl)], axis=1)
        x_scr[...] = _dot(u_tm.astype(BF16), bm_ref[...])

    drive(uf_ref, bmf_ref, xf_scr)
    drive(ub_ref, bmb_ref, xb_scr)

    def advance(x_scr, a_re, a_im, s_re, s_im, tt):
        rows = pl.ds(pl.multiple_of(tt * nb, nb), nb)
        n_re = a_re * s_re - a_im * s_im + x_scr[rows, 0:ns]
        n_im = a_re * s_im + a_im * s_re + x_scr[rows, ns:2 * ns]
        x_scr[rows, 0:ns] = n_re
        x_scr[rows, ns:2 * ns] = n_im
        return n_re, n_im

    def step(t, carry):
        f_re, f_im, b_re, b_im = carry
        f_re, f_im = advance(xf_scr, arf_ref[...], aif_ref[...], f_re, f_im, t)
        b_re, b_im = advance(xb_scr, arb_ref[...], aib_ref[...], b_re, b_im, tc - 1 - t)
        return f_re, f_im, b_re, b_im

    init = (st_scr[0, :, 0:ns], st_scr[0, :, ns:2 * ns], st_scr[1, :, 0:ns], st_scr[1, :, ns:2 * ns])
    f_re, f_im, b_re, b_im = lax.fori_loop(0, tc, step, init, unroll=2)
    st_scr[0, :, 0:ns] = f_re
    st_scr[0, :, ns:2 * ns] = f_im
    st_scr[1, :, 0:ns] = b_re
    st_scr[1, :, ns:2 * ns] = b_im

    def readout(x_scr, cm_ref, y_ref):
        y_tm = _dot(x_scr[...].astype(BF16), cm_ref[...])
        for c in range(nl):
            io_scr[c] = y_tm[:, c * 128:(c + 1) * 128]
        for b in range(nb):
            for c in range(nl):
                y_ref[b, :, c * 128:(c + 1) * 128] = io_scr[c, pl.ds(b, tc, stride=nb), :]

    readout(xf_scr, cmf_ref, yf_ref)
    readout(xb_scr, cmb_ref, yb_ref)


def _s5_scan(u, fwd, bwd, lc):
    nb, s, w = u.shape
    tc = S5_CHUNK
    nc, ncc = s // tc, lc // tc
    ns = fwd[2].shape[1]
    chunk_f = lambda j: (0, j, 0)
    chunk_b = lambda j: (0, jnp.where(j < ncc, ncc - 1 - j, nc - 1 - (j - ncc)), 0)
    const = lambda j: (0, 0)
    pspecs = [pl.BlockSpec((w, 2 * ns), const), pl.BlockSpec((2 * ns, w), const),
              pl.BlockSpec((nb, ns), const), pl.BlockSpec((nb, ns), const)]
    out = jax.ShapeDtypeStruct((nb, s, w), F32)
    return pl.pallas_call(
        _s5_kernel,
        out_shape=(out, out),
        grid=(nc,),
        in_specs=[pl.BlockSpec((nb, tc, w), chunk_f), pl.BlockSpec((nb, tc, w), chunk_b)] + pspecs + pspecs,
        out_specs=(pl.BlockSpec((nb, tc, w), chunk_f), pl.BlockSpec((nb, tc, w), chunk_b)),
        scratch_shapes=[pltpu.VMEM((tc * nb, 2 * ns), F32), pltpu.VMEM((tc * nb, 2 * ns), F32),
                        pltpu.VMEM((2, nb, 2 * ns), F32), pltpu.VMEM((w // 128, tc * nb, 128), F32)],
        compiler_params=_cparams(("arbitrary",)),
        name="s5_scan",
    )(u, u, *fwd, *bwd)


def _s5_params(a_re, a_im, log_dt, b_re, b_im, c_re, c_im, nb):
    g, n, p = S5_GROUPS, S5_STATE, S5_GROUP_CH
    lr, li = a_re.astype(F32), a_im.astype(F32)
    dt = jnp.exp(log_dt.astype(F32))[..., None]
    mag = jnp.exp(lr * dt)
    ab_r, ab_i = mag * jnp.cos(li * dt), mag * jnp.sin(li * dt)
    den = lr * lr + li * li
    cf_r = ((ab_r - 1.0) * lr + ab_i * li) / den
    cf_i = (ab_i * lr - (ab_r - 1.0) * li) / den
    br, bi = b_re.astype(F32), b_im.astype(F32)
    bb_r = cf_r[..., None] * br - cf_i[..., None] * bi
    bb_i = cf_r[..., None] * bi + cf_i[..., None] * br
    eye = jnp.eye(g, dtype=F32)
    out = []
    for k in range(2):
        b_r = jnp.einsum('gnp,gh->gphn', bb_r[k], eye).reshape(g * p, g * n)
        b_i = jnp.einsum('gnp,gh->gphn', bb_i[k], eye).reshape(g * p, g * n)
        bmat = jnp.concatenate([b_r, b_i], axis=1).astype(BF16)
        ct = jnp.transpose(c_re[k].astype(F32), (0, 2, 1))
        ci = jnp.transpose(c_im[k].astype(F32), (0, 2, 1))
        c_r = jnp.einsum('gnp,gh->gnhp', ct, eye).reshape(g * n, g * p)
        c_i = jnp.einsum('gnp,gh->gnhp', ci, eye).reshape(g * n, g * p)
        cmat = jnp.concatenate([c_r, -c_i], axis=0).astype(BF16)
        ar = jnp.broadcast_to(ab_r[k].reshape(1, g * n), (nb, g * n))
        ai = jnp.broadcast_to(ab_i[k].reshape(1, g * n), (nb, g * n))
        out.append((bmat, cmat, ar, ai))
    return out


def _out_kernel(oa_ref, on_ref, yf_ref, yb_ref, u_ref, x_ref, mod_ref, gffn_ref, wo_ref, dskip_ref, wglu_ref, bglu_ref,
                wr_ref, xo_ref, h2_ref, lg_ref):
    mod = mod_ref[0, 0]
    nt = lambda a, bb: lax.dot_general(a, bb, (((1,), (1,)), ((), ())), preferred_element_type=F32)
    wh, wl = _split_bf16(wr_ref[...])
    half = x_ref.shape[1] // 2
    for r in (slice(0, half), slice(half, 2 * half)):
        g = _gelu_tanh(yf_ref[0, r, :] + yb_ref[0, r, :] + dskip_ref[...] * u_ref[0, r, :])
        ob = g * _sigmoid(_dot(g.astype(BF16), wglu_ref[...]) + bglu_ref[...])
        mix = (_dot(oa_ref[0, r, :], wo_ref[0:HW, :]) + _dot(ob.astype(BF16), wo_ref[HW:HW + B_WIDTH, :])
               + _dot(on_ref[0, r, :], wo_ref[HW + B_WIDTH:, :]))
        x = x_ref[0, r, :] + mod[2:3] * mix
        xo_ref[0, r, :] = x
        ms = jnp.mean(x * x, axis=-1, keepdims=True)
        h2 = x * lax.rsqrt(ms + EPS) * gffn_ref[...] * (1.0 + mod[4:5]) + mod[3:4]
        h2_ref[0, r, :] = h2.astype(BF16)
        hh, hl = _split_bf16(h2)
        lg_ref[0, :, r] = nt(wh, hh) + nt(wh, hl) + nt(wl, hh)


def _out_projection(oa, on, y_f, y_b, u, xs, modall, g_ffn, w_out_bf, d_skip, w_glu_bf, b_glu, w_router_pad, lc,
                    with_ctx):
    b, s, d = xs.shape
    tm = ROW_TILE
    off = 0 if with_ctx else lc // tm
    rows_out = s - off * tm
    full = lambda bb, i: (bb, i + off, 0)
    outr = lambda bb, i: (bb, i, 0)
    const = lambda bb, i: (0, 0)
    ne = w_router_pad.shape[0]
    return pl.pallas_call(
        _out_kernel,
        out_shape=(jax.ShapeDtypeStruct((b, rows_out, d), F32),
                   jax.ShapeDtypeStruct((b, rows_out, d), BF16),
                   jax.ShapeDtypeStruct((b, ne, rows_out), F32)),
        grid=(b, rows_out // tm),
        in_specs=[pl.BlockSpec((1, tm, HW), outr),
                  pl.BlockSpec((1, tm, HW), outr),
                  pl.BlockSpec((1, tm, B_WIDTH), full),
                  pl.BlockSpec((1, tm, B_WIDTH), full),
                  pl.BlockSpec((1, tm, B_WIDTH), full),
                  pl.BlockSpec((1, tm, d), full),
                  pl.BlockSpec((1, 1, 6, d), lambda bb, i: (bb, jnp.minimum(i + off, 1), 0, 0)),
                  pl.BlockSpec((1, d), const),
                  pl.BlockSpec((d, d), const),
                  pl.BlockSpec((1, B_WIDTH), const),
                  pl.BlockSpec((B_WIDTH, B_WIDTH), const),
                  pl.BlockSpec((1, B_WIDTH), const),
                  pl.BlockSpec((ne, d), const)],
        out_specs=(pl.BlockSpec((1, tm, d), outr), pl.BlockSpec((1, tm, d), outr),
                   pl.BlockSpec((1, ne, tm), lambda bb, i: (bb, 0, i))),
        compiler_params=_cparams(("arbitrary", "arbitrary")),
        name="out_proj",
    )(oa, on, y_f, y_b, u, xs, modall, g_ffn.reshape(1, d), w_out_bf, d_skip.reshape(1, -1), w_glu_bf,
      b_glu.reshape(1, -1), w_router_pad)


def _swiglu(h, wg, wu):
    a = _dot(h, wg)
    return a * _sigmoid(a) * _dot(h, wu)


def _moe_kernel(need_ref, slots_ref, wg_ref, wu_ref, wd_ref, sg_ref, su_ref, sd_ref, y_ref, h_scr, g_scr, acc_ref,
                *, n_routed):
    i = pl.program_id(0)
    e = pl.program_id(1)
    tm = h_scr.shape[0]
    nsub = tm // ROW_TILE

    @pl.when(e == 0)
    def _():
        for j in range(SUBROWS):
            h_scr[:, j * 128:(j + 1) * 128] = slots_ref[pl.ds(j, tm, stride=SLOT_ROWS), :].astype(BF16)
        g_scr[...] = slots_ref[pl.ds(SUBROWS, tm, stride=SLOT_ROWS), :]
        acc_ref[...] = jnp.zeros_like(acc_ref)

    @pl.when(e < n_routed)
    def _():
        shift = lax.rem(128 - e * MOE_EB, 128)
        for s in range(nsub):
            rows = slice(s * ROW_TILE, (s + 1) * ROW_TILE)

            @pl.when(need_ref[(i * nsub + s) * n_routed + e] != 0)
            def _():
                h = h_scr[rows, :]
                gsel = pltpu.roll(g_scr[rows, :], shift, 1)
                hid = [(_swiglu(h, wg_ref[j], wu_ref[j]) * gsel[:, j:j + 1]).astype(BF16) for j in range(MOE_EB)]
                acc_ref[rows, :] += _dot(jnp.concatenate(hid, axis=-1), wd_ref[...])

    @pl.when(e == n_routed)
    def _():
        for s in range(nsub):
            rows = slice(s * ROW_TILE, (s + 1) * ROW_TILE)
            hs = _swiglu(h_scr[rows, :], sg_ref[...], su_ref[...]).astype(BF16)
            y = acc_ref[rows, :] + _dot(hs, sd_ref[...])
            for j in range(SUBROWS):
                y_ref[pl.ds(s * ROW_TILE * SUBROWS + j, ROW_TILE, stride=SUBROWS), :] = y[:, j * 128:(j + 1) * 128]


def _moe_tile(t):
    return max(m for m in range(ROW_TILE, MOE_TILE + 1, ROW_TILE) if t % m == 0)


def _moe(slots, need, wg, wu, wd, sg, su, sd):
    t = slots.shape[0] // SLOT_ROWS
    d = wg.shape[1]
    tm = _moe_tile(t)
    n_routed = wg.shape[0] // MOE_EB
    step = lambda e: jnp.minimum(e, n_routed - 1)
    const = lambda i, e, nd: (0, 0)
    return pl.pallas_call(
        functools.partial(_moe_kernel, n_routed=n_routed),
        out_shape=jax.ShapeDtypeStruct((t * SUBROWS, 128), F32),
        grid_spec=pltpu.PrefetchScalarGridSpec(
            num_scalar_prefetch=1,
            grid=(t // tm, n_routed + 1),
            in_specs=[pl.BlockSpec((tm * SLOT_ROWS, 128), lambda i, e, nd: (i, 0)),
                      pl.BlockSpec((MOE_EB, d, EXPERT_DIM), lambda i, e, nd: (step(e), 0, 0)),
                      pl.BlockSpec((MOE_EB, d, EXPERT_DIM), lambda i, e, nd: (step(e), 0, 0)),
                      pl.BlockSpec((MOE_EB * EXPERT_DIM, d), lambda i, e, nd: (step(e), 0)),
                      pl.BlockSpec((d, EXPERT_DIM), const),
                      pl.BlockSpec((d, EXPERT_DIM), const),
                      pl.BlockSpec((EXPERT_DIM, d), const)],
            out_specs=pl.BlockSpec((tm * SUBROWS, 128), lambda i, e, nd: (i, 0)),
            scratch_shapes=[pltpu.VMEM((tm, d), BF16), pltpu.VMEM((tm, 128), F32), pltpu.VMEM((tm, d), F32)]),
        compiler_params=_cparams(("arbitrary", "arbitrary")),
        name="moe_ffn",
    )(need, slots, wg, wu, wd, sg, su, sd)


def _route_kernel(lg_ref, bias_ref, o_ref, cls_ref, cnt_ref):
    gsz = N_EXPERTS // N_GROUPS
    tn = lg_ref.shape[2]
    ninf = -jnp.inf
    jidx = lax.broadcasted_iota(jnp.int32, (gsz, tn), 0)
    scores, biased, gscore = [], [], []
    for g in range(N_GROUPS):
        rows = slice(g * gsz, (g + 1) * gsz)
        sc = _sigmoid(lg_ref[0, rows, :])
        bi = sc + bias_ref[rows, :]
        m1 = jnp.max(bi, axis=0, keepdims=True)
        first = jnp.min(jnp.where(bi == m1, jidx, gsz), axis=0, keepdims=True)
        m2 = jnp.max(jnp.where(jidx == first, ninf, bi), axis=0, keepdims=True)
        scores.append(sc)
        biased.append(bi)
        gscore.append(m1 + m2)
    masked = []
    cls = jnp.zeros((1, tn), F32)
    for g in range(N_GROUPS):
        rank = jnp.zeros((1, tn), F32)
        for g2 in range(N_GROUPS):
            if g2 != g:
                ahead = (gscore[g2] >= gscore[g]) if g2 < g else (gscore[g2] > gscore[g])
                rank = rank + jnp.where(ahead, 1.0, 0.0)
        cls = cls + jnp.where(rank < TOPK_GROUPS, float(1 << g), 0.0)
        keep = jnp.broadcast_to(rank, (gsz, tn)) < TOPK_GROUPS
        masked.append(jnp.where(keep, biased[g], ninf))
    cls_ref[...] = cls
    onehot = jnp.where(lax.broadcasted_iota(jnp.int32, (N_CLASSES, tn), 0).astype(F32)
                       == jnp.broadcast_to(cls, (N_CLASSES, tn)), 1.0, 0.0)

    @pl.when((pl.program_id(0) == 0) & (pl.program_id(1) == 0))
    def _():
        cnt_ref[...] = jnp.zeros_like(cnt_ref)

    cnt_ref[...] += jnp.broadcast_to(jnp.sum(onehot, axis=1, keepdims=True), cnt_ref.shape)
    ranks = [jnp.zeros((gsz, tn), F32) for _ in range(N_GROUPS)]
    for g2 in range(N_GROUPS):
        for j2 in range(gsz):
            other = jnp.broadcast_to(masked[g2][j2:j2 + 1, :], (gsz, tn))
            for g in range(N_GROUPS):
                ge = jnp.where(other >= masked[g], 1.0, 0.0)
                gt = jnp.where(other > masked[g], 1.0, 0.0)
                if g2 < g:
                    ahead = ge
                elif g2 > g:
                    ahead = gt
                else:
                    ahead = jnp.where(jidx > j2, ge, gt)
                ranks[g] = ranks[g] + ahead
    picked = [jnp.where(ranks[g] < TOP_K, scores[g], 0.0) for g in range(N_GROUPS)]
    den = sum(jnp.sum(p, axis=0, keepdims=True) for p in picked)
    scale = ROUTED_SCALE / den
    shared_row = jnp.where(lax.broadcasted_iota(jnp.int32, (128 - N_EXPERTS, tn), 0) == 0, 1.0, 0.0)
    gates_t = jnp.concatenate([p * scale for p in picked] + [shared_row], axis=0)
    o_ref[...] = gates_t.T


def _route(logits_t, e_bias):
    b, ne, rows = logits_t.shape
    tn = ROW_TILE
    nt = rows // tn
    bias = jnp.broadcast_to(e_bias.astype(F32)[:, None], (N_EXPERTS, tn))
    return pl.pallas_call(
        _route_kernel,
        out_shape=(jax.ShapeDtypeStruct((b * rows, ne), F32), jax.ShapeDtypeStruct((1, b * rows), F32),
                   jax.ShapeDtypeStruct((N_CLASSES, 128), F32)),
        grid=(b, nt),
        in_specs=[pl.BlockSpec((1, ne, tn), lambda bb, i: (bb, 0, i)),
                  pl.BlockSpec((N_EXPERTS, tn), lambda bb, i: (0, 0))],
        out_specs=(pl.BlockSpec((tn, ne), lambda bb, i: (bb * nt + i, 0)),
                   pl.BlockSpec((1, tn), lambda bb, i: (0, bb * nt + i)),
                   pl.BlockSpec((N_CLASSES, 128), lambda bb, i: (0, 0))),
        compiler_params=_cparams(("arbitrary", "arbitrary")),
        name="moe_route",
    )(logits_t, bias)


def _sort_pos_kernel(cls_ref, off_ref, pos_ref, base_scr):
    tn = cls_ref.shape[1]

    @pl.when(pl.program_id(0) == 0)
    def _():
        base_scr[...] = jnp.zeros_like(base_scr)

    onehot = jnp.where(lax.broadcasted_iota(jnp.int32, (N_CLASSES, tn), 0).astype(F32)
                       == jnp.broadcast_to(cls_ref[...], (N_CLASSES, tn)), 1.0, 0.0)
    upper = jnp.where(lax.broadcasted_iota(jnp.int32, (tn, tn), 0) <= lax.broadcasted_iota(jnp.int32, (tn, tn), 1),
                      1.0, 0.0).astype(BF16)
    incl = _dot(onehot.astype(BF16), upper)
    posf = off_ref[:, 0:1] + base_scr[:, 0:1] + incl - 1.0
    base_scr[...] += jnp.broadcast_to(incl[:, tn - 1:tn], base_scr.shape)
    pos_ref[0] = jnp.sum(posf * onehot, axis=0, keepdims=True).astype(jnp.int32)


def _sort_positions(cls, offsets):
    t = cls.shape[1]
    tn = ROW_TILE
    nt = t // tn
    off = jnp.broadcast_to(offsets.astype(F32)[:, None], (N_CLASSES, 128))
    pos = pl.pallas_call(
        _sort_pos_kernel,
        out_shape=jax.ShapeDtypeStruct((nt, 1, tn), jnp.int32),
        grid=(nt,),
        in_specs=[pl.BlockSpec((1, tn), lambda i: (0, i)), pl.BlockSpec((N_CLASSES, 128), lambda i: (0, 0))],
        out_specs=pl.BlockSpec((1, 1, tn), lambda i: (i, 0, 0)),
        scratch_shapes=[pltpu.VMEM((N_CLASSES, 128), F32)],
        compiler_params=_cparams(("arbitrary",)),
        name="moe_sort_positions",
    )(cls, off)
    return pos.reshape(nt, tn)


def _row_copies(n, make):
    def issue(jj, c):
        for p in range(2):
            make(jj * 2 + p).start(priority=p)
        return c

    lax.fori_loop(0, n // 2, issue, 0, unroll=4)

    def drain(j, c):
        make(0).wait()
        return c

    lax.fori_loop(0, n, drain, 0, unroll=8)


def _permute_kernel(pos_hbm, h_ref, g_ref, out_hbm, pos_smem, rows_scr, sem_idx, sem_rows):
    i = pl.program_id(0)
    tok = h_ref.shape[0]
    idx_copy = pltpu.make_async_copy(pos_hbm.at[i], pos_smem, sem_idx)
    idx_copy.start()

    @pl.when(i == 0)
    def _():
        rows_scr[...] = jnp.zeros_like(rows_scr)

    for j in range(SUBROWS):
        rows_scr[pl.ds(j, tok, stride=SLOT_ROWS), :] = h_ref[:, j * 128:(j + 1) * 128].astype(F32)
    rows_scr[pl.ds(SUBROWS, tok, stride=SLOT_ROWS), :] = g_ref[...]
    idx_copy.wait()

    def row_copy(j):
        src = pl.multiple_of(j * SLOT_ROWS, SLOT_ROWS)
        dst = pl.multiple_of(pos_smem[j] * SLOT_ROWS, SLOT_ROWS)
        return pltpu.make_async_copy(rows_scr.at[pl.ds(src, SLOT_ROWS), :], out_hbm.at[pl.ds(dst, SLOT_ROWS), :],
                                     sem_rows)

    _row_copies(tok, row_copy)


def _permute(h2, gates, pos):
    t, d = h2.shape
    tok = ROW_TILE
    assert d == SUBROWS * 128
    return pl.pallas_call(
        _permute_kernel,
        out_shape=jax.ShapeDtypeStruct((t * SLOT_ROWS, 128), F32),
        grid=(t // tok,),
        in_specs=[pl.BlockSpec(memory_space=pl.ANY),
                  pl.BlockSpec((tok, d), lambda i: (i, 0)),
                  pl.BlockSpec((tok, 128), lambda i: (i, 0))],
        out_specs=pl.BlockSpec(memory_space=pl.ANY),
        scratch_shapes=[pltpu.SMEM((tok,), jnp.int32), pltpu.VMEM((tok * SLOT_ROWS, 128), F32),
                        pltpu.SemaphoreType.DMA, pltpu.SemaphoreType.DMA],
        compiler_params=_cparams(("arbitrary",)),
        name="moe_permute",
    )(pos, h2, gates)


def _unpermute_kernel(pos_hbm, y_hbm, x_ref, gt_ref, o_ref, pos_smem, buf, sem_idx, sem_rows):
    i = pl.program_id(0)
    tok = x_ref.shape[0]
    idx_copy = pltpu.make_async_copy(pos_hbm.at[i], pos_smem, sem_idx)
    idx_copy.start()
    idx_copy.wait()

    def row_copy(j):
        src = pl.multiple_of(pos_smem[j] * SUBROWS, SUBROWS)
        dst = pl.multiple_of(j * SUBROWS, SUBROWS)
        return pltpu.make_async_copy(y_hbm.at[pl.ds(src, SUBROWS), :], buf.at[pl.ds(dst, SUBROWS), :], sem_rows)

    _row_copies(tok, row_copy)
    y = jnp.concatenate([buf[pl.ds(j, tok, stride=SUBROWS), :] for j in range(SUBROWS)], axis=1)
    o_ref[...] = x_ref[...] + gt_ref[0] * y


def _unpermute(pos, y_rows, x, gt_tiles):
    t, d = x.shape
    tok = ROW_TILE
    return pl.pallas_call(
        _unpermute_kernel,
        out_shape=jax.ShapeDtypeStruct((t, d), F32),
        grid=(t // tok,),
        in_specs=[pl.BlockSpec(memory_space=pl.ANY), pl.BlockSpec(memory_space=pl.ANY),
                  pl.BlockSpec((tok, d), lambda i: (i, 0)),
                  pl.BlockSpec((1, 1, d), lambda i: (i, 0, 0))],
        out_specs=pl.BlockSpec((tok, d), lambda i: (i, 0)),
        scratch_shapes=[pltpu.SMEM((tok,), jnp.int32), pltpu.VMEM((tok * SUBROWS, 128), F32),
                        pltpu.SemaphoreType.DMA, pltpu.SemaphoreType.DMA],
        compiler_params=_cparams(("arbitrary",)),
        name="moe_unpermute",
    )(pos, y_rows, x, gt_tiles)


def _step_needed(class_counts, n_sub, n_steps):
    ends = jnp.cumsum(class_counts)
    starts = ends - class_counts
    lo = jnp.arange(n_sub, dtype=jnp.int32)[:, None] * ROW_TILE
    present = (class_counts[None, :] > 0) & (starts[None, :] < lo + ROW_TILE) & (ends[None, :] > lo)
    bits = ((jnp.arange(N_CLASSES)[:, None] >> jnp.arange(N_GROUPS)[None, :]) & 1).astype(bool)
    group_needed = jnp.any(present[:, :, None] & bits[None, :, :], axis=1)
    steps_per_group = n_steps // N_GROUPS
    return jnp.repeat(group_needed, steps_per_group, axis=1).astype(jnp.int32).reshape(-1), starts


def kernel(x, c, ctx, c_ctx, w_mod, b_mod, g_mix, g_ffn, w_in, w_out, a_gq, a_gk, a_lambda, a_gsub, n_gq, n_gk, n_rpb, s5_a_re, s5_a_im, s5_log_dt, s5_b_re, s5_b_im, s5_c_re, s5_c_im, s5_d, s5_w_glu, s5_b_glu, w_router, e_bias, w_gate, w_up, w_down, ws_gate, ws_up, ws_down):
    b, l, d = x.shape
    lc = ctx.shape[1]
    s = lc + l
    depth = w_mod.shape[0]
    n_img_rows = l // GRID_W

    xs = jnp.concatenate([ctx, x], axis=1).astype(F32)
    cond_rows = jnp.zeros((16, d), F32).at[:b].set(c.astype(F32)).at[b].set(c_ctx.astype(F32))
    cos, sa, sb = _rope_tables(s, lc)
    tables = (cos, sa, sb, _block_ones(A_QK_DIM), _block_ones(N_HEAD_DIM))

    for layer in range(depth):
        last = layer == depth - 1
        with_ctx = not last
        lam_init = 0.8 - 0.6 * math.exp(-0.3 * layer)

        mod = _modulation(cond_rows, w_mod[layer].astype(F32), b_mod[layer].astype(F32))
        mod_lat = mod[:b].reshape(b, 1, 6, d)
        mod_ctx = jnp.broadcast_to(mod[b].reshape(1, 1, 6, d), (b, 1, 6, d))
        modall = jnp.concatenate([mod_ctx, mod_lat], axis=1)

        gains = ((jnp.tile(a_gq[layer].astype(F32), HW // A_QK_DIM) * (A_QK_DIM ** -0.5 * LOG2E)).reshape(1, HW),
                 jnp.tile(a_gk[layer].astype(F32), HW // A_QK_DIM).reshape(1, HW),
                 (jnp.tile(n_gq[layer].astype(F32), N_HEADS) * (N_HEAD_DIM ** -0.5 * LOG2E)).reshape(1, HW),
                 jnp.tile(n_gk[layer].astype(F32), N_HEADS).reshape(1, HW))
        qa, kat, va, qn, knt, vn, u = _in_projection(xs, modall, g_mix[layer].astype(F32), w_in[layer].astype(BF16),
                                                     tables, gains, lc)

        lv = a_lambda[layer].astype(F32)
        lam = (jnp.exp(jnp.sum(lv[0] * lv[1])) - jnp.exp(jnp.sum(lv[2] * lv[3])) + lam_init).reshape(1, 1)
        gsub_t = (jnp.tile(a_gsub[layer].astype(F32), A_HEADS) * (1.0 - lam_init)).reshape(1, HW)
        a_bound = BOUND_SLACK * A_QK_DIM * jnp.max(jnp.abs(gains[0])) * jnp.max(jnp.abs(gains[1]))
        oa = lax.cond(a_bound <= SCORE_BOUND_LIMIT,
                      lambda *a: _diff_attention(*a, lc, with_ctx, False),
                      lambda *a: _diff_attention(*a, lc, with_ctx, True), qa, kat, va, lam, gsub_t)

        n_bound = (BOUND_SLACK * N_HEAD_DIM * jnp.max(jnp.abs(gains[2])) * jnp.max(jnp.abs(gains[3]))
                   + LOG2E * jnp.max(jnp.abs(n_rpb[layer].astype(F32))))
        on = lax.cond(n_bound <= SCORE_BOUND_LIMIT,
                      lambda *a: _na_attention(*a, lc, with_ctx, False),
                      lambda *a: _na_attention(*a, lc, with_ctx, True),
                      qn, knt, vn, _na_bias(n_rpb[layer], n_img_rows))

        s5_fwd, s5_bwd = _s5_params(s5_a_re[layer], s5_a_im[layer], s5_log_dt[layer], s5_b_re[layer],
                                    s5_b_im[layer], s5_c_re[layer], s5_c_im[layer], b)
        y_f, y_b = _s5_scan(u, s5_fwd, s5_bwd, lc)

        w_router_pad = jnp.zeros((128, d), F32).at[:N_EXPERTS].set(w_router[layer].astype(F32).T)
        x_new, h2, logits_t = _out_projection(oa, on, y_f, y_b, u, xs, modall, g_ffn[layer].astype(F32),
                                              w_out[layer].astype(BF16), s5_d[layer].astype(F32),
                                              s5_w_glu[layer].astype(BF16), s5_b_glu[layer].astype(F32),
                                              w_router_pad, lc, with_ctx)
        rows = x_new.shape[1]
        t = b * rows
        gates, cls, class_cnt = _route(logits_t, e_bias[layer])
        need, class_start = _step_needed(class_cnt[:, 0].astype(jnp.int32), t // ROW_TILE, N_EXPERTS // MOE_EB)
        pos = _sort_positions(cls, class_start)
        slots = _permute(h2.reshape(t, d), gates, pos)
        y_rows = _moe(slots, need, w_gate[layer].astype(BF16), w_up[layer].astype(BF16),
                      w_down[layer].astype(BF16).reshape(-1, d),
                      ws_gate[layer].astype(BF16), ws_up[layer].astype(BF16), ws_down[layer].astype(BF16))

        gt2 = modall[:, :, 5, :]
        tiles_per_batch = rows // ROW_TILE
        if with_ctx:
            nctx = lc // ROW_TILE
            sel = (jnp.arange(tiles_per_batch) >= nctx).astype(jnp.int32)
        else:
            sel = jnp.ones((tiles_per_batch,), jnp.int32)
        gt_tiles = gt2[:, sel, :].reshape(t // ROW_TILE, 1, d)
        out = _unpermute(pos, y_rows, x_new.reshape(t, d), gt_tiles)
        xs = out.reshape(b, rows, d)

    return xs.astype(x.dtype)
```

```python
import functools
import math

import jax
import jax.numpy as jnp
from jax import lax
from jax.experimental import pallas as pl
from jax.experimental.pallas import tpu as pltpu

F32 = jnp.float32
BF16 = jnp.bfloat16

D_MODEL = 1024
GRID_W = 64
EPS = 1e-6
A_HEADS = 6
A_QK_DIM = 32
A_V_DIM = 64
ROPE_THETA = 10000.0
S5_GROUPS = 16
S5_GROUP_CH = 16
S5_STATE = 64
N_HEADS = 6
N_HEAD_DIM = 64
WIN_ROWS = 8
WIN_COLS = 16
HW = 384
B_WIDTH = 256
Q_COLS = 768
IN_COLS = 2560
N_EXPERTS = 64
TOP_K = 8
N_GROUPS = 8
TOPK_GROUPS = 4
EXPERT_DIM = 256
ROUTED_SCALE = 2.5

ROW_TILE = 256
NA_ROWS = 4
NA_KROWS = 12
S5_CHUNK = 128
MOE_TILE = 1024
MOE_EB = 4
N_CLASSES = 1 << N_GROUPS
SUBROWS = 8
SLOT_ROWS = 16
NEG = -1e30
LOG2E = math.log2(math.e)
SCORE_BOUND_LIMIT = 50.0
BOUND_SLACK = 1.05
VMEM_LIMIT = 56 * 1024 * 1024


def _sigmoid(x):
    return 1.0 / (1.0 + jnp.exp(-x))


def _gelu_tanh(x):
    return 0.5 * x * (1.0 + jnp.tanh(math.sqrt(2.0 / math.pi) * (x + 0.044715 * (x * x * x))))


def _split_bf16(a):
    hi = a.astype(BF16)
    lo = (a - hi.astype(F32)).astype(BF16)
    return hi, lo


def _dot(a, b):
    return jnp.dot(a, b, preferred_element_type=F32)


def _dot3(a, b):
    ah, al = _split_bf16(a)
    bh, bl = _split_bf16(b)
    return _dot(ah, bh) + _dot(ah, bl) + _dot(al, bh)


def _cparams(sem):
    return pltpu.CompilerParams(dimension_semantics=sem, vmem_limit_bytes=VMEM_LIMIT)


def _mod_kernel(c_ref, w_ref, b_ref, o_ref):
    c = c_ref[...]
    cond = c * _sigmoid(c)
    o_ref[...] = _dot3(cond, w_ref[...]) + b_ref[...]


def _modulation(cond_rows, w_mod, b_mod):
    r, d = cond_rows.shape
    n = w_mod.shape[1]
    tn = 1536
    return pl.pallas_call(
        _mod_kernel,
        out_shape=jax.ShapeDtypeStruct((r, n), F32),
        grid=(n // tn,),
        in_specs=[pl.BlockSpec((r, d), lambda j: (0, 0)),
                  pl.BlockSpec((d, tn), lambda j: (0, j)),
                  pl.BlockSpec((1, tn), lambda j: (0, j))],
        out_specs=pl.BlockSpec((r, tn), lambda j: (0, j)),
        compiler_params=_cparams(("arbitrary",)),
        name="adaln_mod",
    )(cond_rows, w_mod, b_mod.reshape(1, n))


def _group_rms(t, ones_ref, gain_ref, group):
    ms = _dot((t * t).astype(BF16), ones_ref[...]) * (1.0 / group)
    return t * lax.rsqrt(ms + EPS) * gain_ref[...]


def _rope(t, cos_ref, sa_ref, sb_ref):
    up = pltpu.roll(t, HW - 8, 1)
    dn = pltpu.roll(t, 8, 1)
    return t * cos_ref[...] + up * sa_ref[...] + dn * sb_ref[...]


def _proj_kernel(x_ref, mod_ref, g_ref, w_ref, cos_ref, sa_ref, sb_ref, ones32_ref, ones64_ref,
                 gqa_ref, gka_ref, gqn_ref, gkn_ref,
                 qa_ref, kat_ref, va_ref, qn_ref, knt_ref, vn_ref, u_ref):
    x = x_ref[0]
    mod = mod_ref[0, 0]
    ms = jnp.mean(x * x, axis=-1, keepdims=True)
    h = x * lax.rsqrt(ms + EPS) * g_ref[...] * (1.0 + mod[1:2]) + mod[0:1]
    hb = h.astype(BF16)

    def sec(a, b):
        return _dot(hb, w_ref[:, a:b])

    qa = _rope(_group_rms(sec(0, 384), ones32_ref, gqa_ref, A_QK_DIM), cos_ref, sa_ref, sb_ref)
    qa_ref[0] = qa.astype(BF16)
    qn_ref[0] = _group_rms(sec(384, 768), ones64_ref, gqn_ref, N_HEAD_DIM).astype(BF16)
    ka = _rope(_group_rms(sec(768, 1152), ones32_ref, gka_ref, A_QK_DIM), cos_ref, sa_ref, sb_ref)
    kat_ref[0] = ka.T.astype(BF16)
    va_ref[0] = sec(1152, 1536).astype(BF16)
    kn = _group_rms(sec(1536, 1920), ones64_ref, gkn_ref, N_HEAD_DIM)
    knt_ref[0] = kn.T.astype(BF16)
    vn_ref[0] = sec(1920, 2304).astype(BF16)
    u_ref[0] = sec(2304, 2560)


def _in_projection(xs, modall, g_mix, w_in_bf, tables, gains, lc):
    b, s, d = xs.shape
    tm = ROW_TILE
    cos, sa, sb, ones32, ones64 = tables
    row = lambda i, bb: (bb, i, 0)
    tab = lambda i, bb: (i, 0)
    const2 = lambda i, bb: (0, 0)
    act = lambda w, dt: jax.ShapeDtypeStruct((b, s, w), dt)
    act_t = jax.ShapeDtypeStruct((b, HW, s), BF16)
    return pl.pallas_call(
        _proj_kernel,
        out_shape=(act(HW, BF16), act_t, act(HW, BF16), act(HW, BF16), act_t, act(HW, BF16), act(B_WIDTH, F32)),
        grid=(s // tm, b),
        in_specs=[pl.BlockSpec((1, tm, d), row),
                  pl.BlockSpec((1, 1, 6, d), lambda i, bb: (bb, jnp.minimum(i, 1), 0, 0)),
                  pl.BlockSpec((1, d), const2),
                  pl.BlockSpec((d, IN_COLS), const2),
                  pl.BlockSpec((tm, HW), tab), pl.BlockSpec((tm, HW), tab), pl.BlockSpec((tm, HW), tab),
                  pl.BlockSpec((HW, HW), const2), pl.BlockSpec((HW, HW), const2),
                  pl.BlockSpec((1, HW), const2), pl.BlockSpec((1, HW), const2),
                  pl.BlockSpec((1, HW), const2), pl.BlockSpec((1, HW), const2)],
        out_specs=(pl.BlockSpec((1, tm, HW), row),
                   pl.BlockSpec((1, HW, tm), lambda i, bb: (bb, 0, i)),
                   pl.BlockSpec((1, tm, HW), row),
                   pl.BlockSpec((1, tm, HW), row),
                   pl.BlockSpec((1, HW, tm), lambda i, bb: (bb, 0, i)),
                   pl.BlockSpec((1, tm, HW), row),
                   pl.BlockSpec((1, tm, B_WIDTH), row)),
        compiler_params=_cparams(("arbitrary", "arbitrary")),
        name="in_proj",
    )(xs, modall, g_mix.reshape(1, d), w_in_bf, cos, sa, sb, ones32, ones64, *gains)


def _rope_tables(s, lc):
    p = jnp.arange(s)
    pos = jnp.maximum(p - lc, 0)
    rows = (pos // GRID_W).astype(F32)
    cols = (pos % GRID_W).astype(F32)
    lane = jnp.arange(HW)
    j32 = lane % A_QK_DIM
    half = j32 // 16
    i16 = j32 % 16
    nf = 8
    inv = ROPE_THETA ** (-(i16 % nf).astype(F32) / nf)
    coord = jnp.where(half[None, :] == 0, rows[:, None], cols[:, None])
    ang = coord * inv[None, :]
    is_lat = (p >= lc)[:, None]
    second = (i16 >= nf)[None, :]
    cos = jnp.where(is_lat, jnp.cos(ang), 1.0)
    sin = jnp.where(is_lat, jnp.sin(ang), 0.0)
    sa = jnp.where(second, 0.0, -sin)
    sb = jnp.where(second, sin, 0.0)
    return cos.astype(F32), sa.astype(F32), sb.astype(F32)


def _block_ones(group):
    g = jnp.arange(HW) // group
    return (g[:, None] == g[None, :]).astype(BF16)


def _diff_attend(q_all, kt_ref, v_ref, lam, gsub_ref, sk, use_max):
    outs = []
    for h in range(A_HEADS):
        v = v_ref[0, 0:sk, h * A_V_DIM:(h + 1) * A_V_DIM]
        parts = []
        for sub in range(2):
            hs = 2 * h + sub
            off = hs * A_QK_DIM
            sc = _dot(q_all[:, off:off + A_QK_DIM], kt_ref[0, off:off + A_QK_DIM, 0:sk])
            e = jnp.exp2(sc - jnp.max(sc, axis=-1, keepdims=True)) if use_max else jnp.exp2(sc)
            parts.append(_dot(e.astype(BF16), v) * (1.0 / jnp.sum(e, axis=-1, keepdims=True)))
        o = parts[0] - lam * parts[1]
        outs.append(o * lax.rsqrt(jnp.mean(o * o, axis=-1, keepdims=True) + EPS))
    return (jnp.concatenate(outs, axis=-1) * gsub_ref[...]).astype(BF16)


def _diff_attn_kernel(q_ref, kt_ref, v_ref, lam_ref, gsub_ref, o_ref, *, lc, ctx_first, use_max):
    lam = lam_ref[...]
    s = kt_ref.shape[2]
    if ctx_first:
        i = pl.program_id(1)

        @pl.when(i == 0)
        def _():
            o_ref[0] = _diff_attend(q_ref[0], kt_ref, v_ref, lam, gsub_ref, lc, use_max)

        @pl.when(i > 0)
        def _():
            o_ref[0] = _diff_attend(q_ref[0], kt_ref, v_ref, lam, gsub_ref, s, use_max)
    else:
        o_ref[0] = _diff_attend(q_ref[0], kt_ref, v_ref, lam, gsub_ref, s, use_max)


def _diff_attention(qa, kat, va, lam, gsub_t, lc, with_ctx, use_max):
    b, s, _ = qa.shape
    tq = ROW_TILE
    off = 0 if with_ctx else lc // tq
    rows_out = s - off * tq
    return pl.pallas_call(
        functools.partial(_diff_attn_kernel, lc=lc, ctx_first=with_ctx, use_max=use_max),
        out_shape=jax.ShapeDtypeStruct((b, rows_out, HW), BF16),
        grid=(b, rows_out // tq),
        in_specs=[pl.BlockSpec((1, tq, HW), lambda bb, i: (bb, i + off, 0)),
                  pl.BlockSpec((1, HW, s), lambda bb, i: (bb, 0, 0)),
                  pl.BlockSpec((1, s, HW), lambda bb, i: (bb, 0, 0)),
                  pl.BlockSpec((1, 1), lambda bb, i: (0, 0)),
                  pl.BlockSpec((1, HW), lambda bb, i: (0, 0))],
        out_specs=pl.BlockSpec((1, tq, HW), lambda bb, i: (bb, i, 0)),
        compiler_params=_cparams(("arbitrary", "arbitrary")),
        name="diff_attn_rowmax" if use_max else "diff_attn",
    )(qa, kat, va, lam, gsub_t)


def _na_ctx_attend(q_all, kt_ref, v_ref, lc, use_max):
    outs = []
    for h in range(N_HEADS):
        hs = slice(h * N_HEAD_DIM, (h + 1) * N_HEAD_DIM)
        sc = _dot(q_all[:, hs], kt_ref[0, hs, 0:lc])
        e = jnp.exp2(sc - jnp.max(sc, axis=-1, keepdims=True)) if use_max else jnp.exp2(sc)
        o = _dot(e.astype(BF16), v_ref[0, 0:lc, hs])
        outs.append(o * (1.0 / jnp.sum(e, axis=-1, keepdims=True)))
    return jnp.concatenate(outs, axis=-1).astype(BF16)


def _na_attend(q_all, kt_ref, v_ref, bias_ref, koff, lc, use_max):
    nk = NA_KROWS * GRID_W
    outs = []
    for h in range(N_HEADS):
        hs = slice(h * N_HEAD_DIM, (h + 1) * N_HEAD_DIM)
        q = q_all[:, hs]
        s_loc = _dot(q, kt_ref[0, hs, pl.ds(koff, nk)]) + bias_ref[0, h]
        s_ctx = _dot(q, kt_ref[0, hs, 0:lc])
        if use_max:
            m = jnp.maximum(jnp.max(s_loc, axis=-1, keepdims=True), jnp.max(s_ctx, axis=-1, keepdims=True))
            s_loc, s_ctx = s_loc - m, s_ctx - m
        e_loc = jnp.exp2(s_loc)
        e_ctx = jnp.exp2(s_ctx)
        den = jnp.sum(e_loc, axis=-1, keepdims=True) + jnp.sum(e_ctx, axis=-1, keepdims=True)
        o = _dot(e_loc.astype(BF16), v_ref[0, pl.ds(koff, nk), hs]) + _dot(e_ctx.astype(BF16), v_ref[0, 0:lc, hs])
        outs.append(o * (1.0 / den))
    return jnp.concatenate(outs, axis=-1).astype(BF16)


def _na_kernel(q_ref, kt_ref, v_ref, bias_ref, o_ref, *, lc, n_img_rows, ctx_first, use_max):
    i = pl.program_id(1)
    blk = i - 1 if ctx_first else i
    start_row = jnp.clip(NA_ROWS * blk - WIN_ROWS // 2, 0, n_img_rows - NA_KROWS)
    koff = pl.multiple_of(lc + start_row * GRID_W, 128)
    if ctx_first:
        @pl.when(i == 0)
        def _():
            o_ref[0] = _na_ctx_attend(q_ref[0], kt_ref, v_ref, lc, use_max)

        @pl.when(i > 0)
        def _():
            o_ref[0] = _na_attend(q_ref[0], kt_ref, v_ref, bias_ref, koff, lc, use_max)
    else:
        o_ref[0] = _na_attend(q_ref[0], kt_ref, v_ref, bias_ref, koff, lc, use_max)


def _na_attention(qn, knt, vn, bias, lc, with_ctx, use_max):
    b, s, _ = qn.shape
    tq = NA_ROWS * GRID_W
    assert tq == ROW_TILE and lc % tq == 0
    n_img_rows = (s - lc) // GRID_W
    nblk = n_img_rows // NA_ROWS
    off = 0 if with_ctx else lc // tq
    rows_out = s - off * tq
    first = 1 if with_ctx else 0

    def variant(bb, i):
        blk = i - first
        return (jnp.where(blk <= 0, 0, jnp.where(blk == nblk - 1, 2, 1)), 0, 0, 0)

    return pl.pallas_call(
        functools.partial(_na_kernel, lc=lc, n_img_rows=n_img_rows, ctx_first=with_ctx, use_max=use_max),
        out_shape=jax.ShapeDtypeStruct((b, rows_out, HW), BF16),
        grid=(b, rows_out // tq),
        in_specs=[pl.BlockSpec((1, tq, HW), lambda bb, i: (bb, i + off, 0)),
                  pl.BlockSpec((1, HW, s), lambda bb, i: (bb, 0, 0)),
                  pl.BlockSpec((1, s, HW), lambda bb, i: (bb, 0, 0)),
                  pl.BlockSpec((1, N_HEADS, tq, NA_KROWS * GRID_W), variant)],
        out_specs=pl.BlockSpec((1, tq, HW), lambda bb, i: (bb, i, 0)),
        compiler_params=_cparams(("arbitrary", "arbitrary")),
        name="nbr_attn_rowmax" if use_max else "nbr_attn",
    )(qn, knt, vn, bias)


def _na_bias(rpb, n_img_rows):
    a = jnp.arange(NA_ROWS)[:, None, None, None]
    cq = jnp.arange(GRID_W)[None, :, None, None]
    j = jnp.arange(NA_KROWS)[None, None, :, None]
    ck = jnp.arange(GRID_W)[None, None, None, :]
    cstart = jnp.clip(cq - WIN_COLS // 2, 0, GRID_W - WIN_COLS)
    colmask = (ck >= cstart) & (ck < cstart + WIN_COLS)
    dc = jnp.clip(ck - cq, -(WIN_COLS - 1), WIN_COLS - 1) + (WIN_COLS - 1)
    by_col = jnp.take(rpb.astype(F32), dc.reshape(-1), axis=2).reshape(N_HEADS, 2 * WIN_ROWS - 1, GRID_W, GRID_W)
    out = []
    for r0_minus_k, wstart in ((0, 0 * a), (WIN_ROWS // 2, a), (NA_KROWS - NA_ROWS, NA_KROWS - WIN_ROWS + 0 * a)):
        inwin = (j >= wstart) & (j < wstart + WIN_ROWS)
        dr = jnp.clip(j - r0_minus_k - a + (WIN_ROWS - 1), 0, 2 * WIN_ROWS - 2)
        vals = jnp.take(by_col, dr.reshape(-1), axis=1).reshape(N_HEADS, NA_ROWS, NA_KROWS, GRID_W, GRID_W)
        vals = jnp.transpose(vals, (0, 1, 3, 2, 4))
        vals = jnp.where((inwin & colmask)[None], vals * LOG2E, NEG)
        out.append(vals.reshape(N_HEADS, NA_ROWS * GRID_W, NA_KROWS * GRID_W))
    return jnp.stack(out)


def _s5_kernel(uf_ref, ub_ref, bmf_ref, cmf_ref, arf_ref, aif_ref, bmb_ref, cmb_ref, arb_ref, aib_ref,
               yf_ref, yb_ref, xf_scr, xb_scr, st_scr, io_scr):
    nb, tc, w = uf_ref.shape
    ns = arf_ref.shape[1]

    @pl.when(pl.program_id(0) == 0)
    def _():
        st_scr[...] = jnp.zeros_like(st_scr)

    nl = w // 128

    def drive(u_ref, bm_ref, x_scr):
        for b in range(nb):
            for c in range(nl):
                io_scr[c, pl.ds(b, tc, stride=nb), :] = u_ref[b, :, c * 128:(c + 1) * 128]
        u_tm = jnp.concatenate([io_scr[c] for c in range(nl)], axis=1)
        x_scr[...] = _dot(u_tm.astype(BF16), bm_ref[...])

    drive(uf_ref, bmf_ref, xf_scr)
    drive(ub_ref, bmb_ref, xb_scr)

    def advance(x_scr, a_re, a_im, s_re, s_im, tt):
        rows = pl.ds(pl.multiple_of(tt * nb, nb), nb)
        n_re = a_re * s_re - a_im * s_im + x_scr[rows, 0:ns]
        n_im = a_re * s_im + a_im * s_re + x_scr[rows, ns:2 * ns]
        x_scr[rows, 0:ns] = n_re
        x_scr[rows, ns:2 * ns] = n_im
        return n_re, n_im

    def step(t, carry):
        f_re, f_im, b_re, b_im = carry
        f_re, f_im = advance(xf_scr, arf_ref[...], aif_ref[...], f_re, f_im, t)
        b_re, b_im = advance(xb_scr, arb_ref[...], aib_ref[...], b_re, b_im, tc - 1 - t)
        return f_re, f_im, b_re, b_im

    init = (st_scr[0, :, 0:ns], st_scr[0, :, ns:2 * ns], st_scr[1, :, 0:ns], st_scr[1, :, ns:2 * ns])
    f_re, f_im, b_re, b_im = lax.fori_loop(0, tc, step, init, unroll=2)
    st_scr[0, :, 0:ns] = f_re
    st_scr[0, :, ns:2 * ns] = f_im
    st_scr[1, :, 0:ns] = b_re
    st_scr[1, :, ns:2 * ns] = b_im

    def readout(x_scr, cm_ref, y_ref):
        y_tm = _dot(x_scr[...].astype(BF16), cm_ref[...])
        for c in range(nl):
            io_scr[c] = y_tm[:, c * 128:(c + 1) * 128]
        for b in range(nb):
            for c in range(nl):
                y_ref[b, :, c * 128:(c + 1) * 128] = io_scr[c, pl.ds(b, tc, stride=nb), :]

    readout(xf_scr, cmf_ref, yf_ref)
    readout(xb_scr, cmb_ref, yb_ref)


def _s5_scan(u, fwd, bwd, lc):
    nb, s, w = u.shape
    tc = S5_CHUNK
    nc, ncc = s // tc, lc // tc
    ns = fwd[2].shape[1]
    chunk_f = lambda j: (0, j, 0)
    chunk_b = lambda j: (0, jnp.where(j < ncc, ncc - 1 - j, nc - 1 - (j - ncc)), 0)
    const = lambda j: (0, 0)
    pspecs = [pl.BlockSpec((w, 2 * ns), const), pl.BlockSpec((2 * ns, w), const),
              pl.BlockSpec((nb, ns), const), pl.BlockSpec((nb, ns), const)]
    out = jax.ShapeDtypeStruct((nb, s, w), F32)
    return pl.pallas_call(
        _s5_kernel,
        out_shape=(out, out),
        grid=(nc,),
        in_specs=[pl.BlockSpec((nb, tc, w), chunk_f), pl.BlockSpec((nb, tc, w), chunk_b)] + pspecs + pspecs,
        out_specs=(pl.BlockSpec((nb, tc, w), chunk_f), pl.BlockSpec((nb, tc, w), chunk_b)),
        scratch_shapes=[pltpu.VMEM((tc * nb, 2 * ns), F32), pltpu.VMEM((tc * nb, 2 * ns), F32),
                        pltpu.VMEM((2, nb, 2 * ns), F32), pltpu.VMEM((w // 128, tc * nb, 128), F32)],
        compiler_params=_cparams(("arbitrary",)),
        name="s5_scan",
    )(u, u, *fwd, *bwd)


def _s5_params(a_re, a_im, log_dt, b_re, b_im, c_re, c_im, nb):
    g, n, p = S5_GROUPS, S5_STATE, S5_GROUP_CH
    lr, li = a_re.astype(F32), a_im.astype(F32)
    dt = jnp.exp(log_dt.astype(F32))[..., None]
    mag = jnp.exp(lr * dt)
    ab_r, ab_i = mag * jnp.cos(li * dt), mag * jnp.sin(li * dt)
    den = lr * lr + li * li
    cf_r = ((ab_r - 1.0) * lr + ab_i * li) / den
    cf_i = (ab_i * lr - (ab_r - 1.0) * li) / den
    br, bi = b_re.astype(F32), b_im.astype(F32)
    bb_r = cf_r[..., None] * br - cf_i[..., None] * bi
    bb_i = cf_r[..., None] * bi + cf_i[..., None] * br
    eye = jnp.eye(g, dtype=F32)
    out = []
    for k in range(2):
        b_r = jnp.einsum('gnp,gh->gphn', bb_r[k], eye).reshape(g * p, g * n)
        b_i = jnp.einsum('gnp,gh->gphn', bb_i[k], eye).reshape(g * p, g * n)
        bmat = jnp.concatenate([b_r, b_i], axis=1).astype(BF16)
        ct = jnp.transpose(c_re[k].astype(F32), (0, 2, 1))
        ci = jnp.transpose(c_im[k].astype(F32), (0, 2, 1))
        c_r = jnp.einsum('gnp,gh->gnhp', ct, eye).reshape(g * n, g * p)
        c_i = jnp.einsum('gnp,gh->gnhp', ci, eye).reshape(g * n, g * p)
        cmat = jnp.concatenate([c_r, -c_i], axis=0).astype(BF16)
        ar = jnp.broadcast_to(ab_r[k].reshape(1, g * n), (nb, g * n))
        ai = jnp.broadcast_to(ab_i[k].reshape(1, g * n), (nb, g * n))
        out.append((bmat, cmat, ar, ai))
    return out


def _out_kernel(oa_ref, on_ref, yf_ref, yb_ref, u_ref, x_ref, mod_ref, gffn_ref, wo_ref, dskip_ref, wglu_ref, bglu_ref,
                wr_ref, xo_ref, h2_ref, lg_ref):
    mod = mod_ref[0, 0]
    nt = lambda a, bb: lax.dot_general(a, bb, (((1,), (1,)), ((), ())), preferred_element_type=F32)
    wh, wl = _split_bf16(wr_ref[...])
    half = x_ref.shape[1] // 2
    for r in (slice(0, half), slice(half, 2 * half)):
        g = _gelu_tanh(yf_ref[0, r, :] + yb_ref[0, r, :] + dskip_ref[...] * u_ref[0, r, :])
        ob = g * _sigmoid(_dot(g.astype(BF16), wglu_ref[...]) + bglu_ref[...])
        mix = (_dot(oa_ref[0, r, :], wo_ref[0:HW, :]) + _dot(ob.astype(BF16), wo_ref[HW:HW + B_WIDTH, :])
               + _dot(on_ref[0, r, :], wo_ref[HW + B_WIDTH:, :]))
        x = x_ref[0, r, :] + mod[2:3] * mix
        xo_ref[0, r, :] = x
        ms = jnp.mean(x * x, axis=-1, keepdims=True)
        h2 = x * lax.rsqrt(ms + EPS) * gffn_ref[...] * (1.0 + mod[4:5]) + mod[3:4]
        h2_ref[0, r, :] = h2.astype(BF16)
        hh, hl = _split_bf16(h2)
        lg_ref[0, :, r] = nt(wh, hh) + nt(wh, hl) + nt(wl, hh)


def _out_projection(oa, on, y_f, y_b, u, xs, modall, g_ffn, w_out_bf, d_skip, w_glu_bf, b_glu, w_router_pad, lc,
                    with_ctx):
    b, s, d = xs.shape
    tm = ROW_TILE
    off = 0 if with_ctx else lc // tm
    rows_out = s - off * tm
    full = lambda bb, i: (bb, i + off, 0)
    outr = lambda bb, i: (bb, i, 0)
    const = lambda bb, i: (0, 0)
    ne = w_router_pad.shape[0]
    return pl.pallas_call(
        _out_kernel,
        out_shape=(jax.ShapeDtypeStruct((b, rows_out, d), F32),
                   jax.ShapeDtypeStruct((b, rows_out, d), BF16),
                   jax.ShapeDtypeStruct((b, ne, rows_out), F32)),
        grid=(b, rows_out // tm),
        in_specs=[pl.BlockSpec((1, tm, HW), outr),
                  pl.BlockSpec((1, tm, HW), outr),
                  pl.BlockSpec((1, tm, B_WIDTH), full),
                  pl.BlockSpec((1, tm, B_WIDTH), full),
                  pl.BlockSpec((1, tm, B_WIDTH), full),
                  pl.BlockSpec((1, tm, d), full),
                  pl.BlockSpec((1, 1, 6, d), lambda bb, i: (bb, jnp.minimum(i + off, 1), 0, 0)),
                  pl.BlockSpec((1, d), const),
                  pl.BlockSpec((d, d), const),
                  pl.BlockSpec((1, B_WIDTH), const),
                  pl.BlockSpec((B_WIDTH, B_WIDTH), const),
                  pl.BlockSpec((1, B_WIDTH), const),
                  pl.BlockSpec((ne, d), const)],
        out_specs=(pl.BlockSpec((1, tm, d), outr), pl.BlockSpec((1, tm, d), outr),
                   pl.BlockSpec((1, ne, tm), lambda bb, i: (bb, 0, i))),
        compiler_params=_cparams(("arbitrary", "arbitrary")),
        name="out_proj",
    )(oa, on, y_f, y_b, u, xs, modall, g_ffn.reshape(1, d), w_out_bf, d_skip.reshape(1, -1), w_glu_bf,
      b_glu.reshape(1, -1), w_router_pad)


def _swiglu(h, wg, wu):
    a = _dot(h, wg)
    return a * _sigmoid(a) * _dot(h, wu)


def _moe_kernel(cnt_ref, sub_ref, slots_ref, wg_ref, wu_ref, wd_ref, sg_ref, su_ref, sd_ref, y_ref,
                h_scr, g_scr, acc_ref, *, n_routed):
    i = pl.program_id(0)
    e = pl.program_id(1)
    tm = h_scr.shape[0]
    nsub = tm // ROW_TILE

    @pl.when(e == 0)
    def _():
        for j in range(SUBROWS):
            h_scr[:, j * 128:(j + 1) * 128] = slots_ref[pl.ds(j, tm, stride=SLOT_ROWS), :].astype(BF16)
        g_scr[...] = slots_ref[pl.ds(SUBROWS, tm, stride=SLOT_ROWS), :]
        acc_ref[...] = jnp.zeros_like(acc_ref)

    @pl.when(e < n_routed)
    def _():
        shift = lax.rem(128 - e * MOE_EB, 128)
        entry = i * n_routed + e
        count = cnt_ref[entry]
        for k in range(1, nsub + 1):
            @pl.when(count == k)
            def _():
                rows = [pl.ds(pl.multiple_of(sub_ref[entry * nsub + q] * ROW_TILE, ROW_TILE), ROW_TILE)
                        for q in range(k)]
                h = jnp.concatenate([h_scr[r, :] for r in rows], axis=0)
                gsel = pltpu.roll(jnp.concatenate([g_scr[r, :] for r in rows], axis=0), shift, 1)
                hid = [(_swiglu(h, wg_ref[j], wu_ref[j]) * gsel[:, j:j + 1]).astype(BF16) for j in range(MOE_EB)]
                out = _dot(jnp.concatenate(hid, axis=-1), wd_ref[...])
                for q, r in enumerate(rows):
                    acc_ref[r, :] += out[q * ROW_TILE:(q + 1) * ROW_TILE, :]

    @pl.when(e == n_routed)
    def _():
        for s in range(nsub):
            rows = slice(s * ROW_TILE, (s + 1) * ROW_TILE)
            hs = _swiglu(h_scr[rows, :], sg_ref[...], su_ref[...]).astype(BF16)
            y = acc_ref[rows, :] + _dot(hs, sd_ref[...])
            for j in range(SUBROWS):
                y_ref[pl.ds(s * ROW_TILE * SUBROWS + j, ROW_TILE, stride=SUBROWS), :] = y[:, j * 128:(j + 1) * 128]


def _moe_tile(t):
    return max(m for m in range(ROW_TILE, MOE_TILE + 1, ROW_TILE) if t % m == 0)


def _moe(slots, need, wg, wu, wd, sg, su, sd):
    t = slots.shape[0] // SLOT_ROWS
    d = wg.shape[1]
    tm = _moe_tile(t)
    nsub = tm // ROW_TILE
    n_routed = wg.shape[0] // MOE_EB
    flags = jnp.transpose(need.reshape(t // tm, nsub, n_routed), (0, 2, 1))
    count = jnp.sum(flags, axis=-1).astype(jnp.int32).reshape(-1)
    listed = jnp.argsort(jnp.logical_not(flags), axis=-1, stable=True).astype(jnp.int32).reshape(-1)
    step = lambda e: jnp.minimum(e, n_routed - 1)
    const = lambda i, e, c, s: (0, 0)
    return pl.pallas_call(
        functools.partial(_moe_kernel, n_routed=n_routed),
        out_shape=jax.ShapeDtypeStruct((t * SUBROWS, 128), F32),
        grid_spec=pltpu.PrefetchScalarGridSpec(
            num_scalar_prefetch=2,
            grid=(t // tm, n_routed + 1),
            in_specs=[pl.BlockSpec((tm * SLOT_ROWS, 128), lambda i, e, c, s: (i, 0)),
                      pl.BlockSpec((MOE_EB, d, EXPERT_DIM), lambda i, e, c, s: (step(e), 0, 0)),
                      pl.BlockSpec((MOE_EB, d, EXPERT_DIM), lambda i, e, c, s: (step(e), 0, 0)),
                      pl.BlockSpec((MOE_EB * EXPERT_DIM, d), lambda i, e, c, s: (step(e), 0)),
                      pl.BlockSpec((d, EXPERT_DIM), const),
                      pl.BlockSpec((d, EXPERT_DIM), const),
                      pl.BlockSpec((EXPERT_DIM, d), const)],
            out_specs=pl.BlockSpec((tm * SUBROWS, 128), lambda i, e, c, s: (i, 0)),
            scratch_shapes=[pltpu.VMEM((tm, d), BF16), pltpu.VMEM((tm, 128), F32), pltpu.VMEM((tm, d), F32)]),
        compiler_params=_cparams(("arbitrary", "arbitrary")),
        name="moe_ffn",
    )(count, listed, slots, wg, wu, wd, sg, su, sd)


def _route_kernel(lg_ref, bias_ref, o_ref, cls_ref, cnt_ref):
    gsz = N_EXPERTS // N_GROUPS
    tn = lg_ref.shape[2]
    ninf = -jnp.inf
    jidx = lax.broadcasted_iota(jnp.int32, (gsz, tn), 0)
    scores, biased, gscore = [], [], []
    for g in range(N_GROUPS):
        rows = slice(g * gsz, (g + 1) * gsz)
        sc = _sigmoid(lg_ref[0, rows, :])
        bi = sc + bias_ref[rows, :]
        m1 = jnp.max(bi, axis=0, keepdims=True)
        first = jnp.min(jnp.where(bi == m1, jidx, gsz), axis=0, keepdims=True)
        m2 = jnp.max(jnp.where(jidx == first, ninf, bi), axis=0, keepdims=True)
        scores.append(sc)
        biased.append(bi)
        gscore.append(m1 + m2)
    masked = []
    cls = jnp.zeros((1, tn), F32)
    for g in range(N_GROUPS):
        rank = jnp.zeros((1, tn), F32)
        for g2 in range(N_GROUPS):
            if g2 != g:
                ahead = (gscore[g2] >= gscore[g]) if g2 < g else (gscore[g2] > gscore[g])
                rank = rank + jnp.where(ahead, 1.0, 0.0)
        cls = cls + jnp.where(rank < TOPK_GROUPS, float(1 << g), 0.0)
        keep = jnp.broadcast_to(rank, (gsz, tn)) < TOPK_GROUPS
        masked.append(jnp.where(keep, biased[g], ninf))
    cls_ref[...] = cls
    onehot = jnp.where(lax.broadcasted_iota(jnp.int32, (N_CLASSES, tn), 0).astype(F32)
                       == jnp.broadcast_to(cls, (N_CLASSES, tn)), 1.0, 0.0)

    @pl.when((pl.program_id(0) == 0) & (pl.program_id(1) == 0))
    def _():
        cnt_ref[...] = jnp.zeros_like(cnt_ref)

    cnt_ref[...] += jnp.broadcast_to(jnp.sum(onehot, axis=1, keepdims=True), cnt_ref.shape)
    ranks = [jnp.zeros((gsz, tn), F32) for _ in range(N_GROUPS)]
    for g2 in range(N_GROUPS):
        for j2 in range(gsz):
            other = jnp.broadcast_to(masked[g2][j2:j2 + 1, :], (gsz, tn))
            for g in range(N_GROUPS):
                ge = jnp.where(other >= masked[g], 1.0, 0.0)
                gt = jnp.where(other > masked[g], 1.0, 0.0)
                if g2 < g:
                    ahead = ge
                elif g2 > g:
                    ahead = gt
                else:
                    ahead = jnp.where(jidx > j2, ge, gt)
                ranks[g] = ranks[g] + ahead
    picked = [jnp.where(ranks[g] < TOP_K, scores[g], 0.0) for g in range(N_GROUPS)]
    den = sum(jnp.sum(p, axis=0, keepdims=True) for p in picked)
    scale = ROUTED_SCALE / den
    shared_row = jnp.where(lax.broadcasted_iota(jnp.int32, (128 - N_EXPERTS, tn), 0) == 0, 1.0, 0.0)
    gates_t = jnp.concatenate([p * scale for p in picked] + [shared_row], axis=0)
    o_ref[...] = gates_t.T


def _route(logits_t, e_bias):
    b, ne, rows = logits_t.shape
    tn = ROW_TILE
    nt = rows // tn
    bias = jnp.broadcast_to(e_bias.astype(F32)[:, None], (N_EXPERTS, tn))
    return pl.pallas_call(
        _route_kernel,
        out_shape=(jax.ShapeDtypeStruct((b * rows, ne), F32), jax.ShapeDtypeStruct((1, b * rows), F32),
                   jax.ShapeDtypeStruct((N_CLASSES, 128), F32)),
        grid=(b, nt),
        in_specs=[pl.BlockSpec((1, ne, tn), lambda bb, i: (bb, 0, i)),
                  pl.BlockSpec((N_EXPERTS, tn), lambda bb, i: (0, 0))],
        out_specs=(pl.BlockSpec((tn, ne), lambda bb, i: (bb * nt + i, 0)),
                   pl.BlockSpec((1, tn), lambda bb, i: (0, bb * nt + i)),
                   pl.BlockSpec((N_CLASSES, 128), lambda bb, i: (0, 0))),
        compiler_params=_cparams(("arbitrary", "arbitrary")),
        name="moe_route",
    )(logits_t, bias)


def _sort_pos_kernel(cls_ref, off_ref, pos_ref, base_scr):
    tn = cls_ref.shape[1]

    @pl.when(pl.program_id(0) == 0)
    def _():
        base_scr[...] = jnp.zeros_like(base_scr)

    onehot = jnp.where(lax.broadcasted_iota(jnp.int32, (N_CLASSES, tn), 0).astype(F32)
                       == jnp.broadcast_to(cls_ref[...], (N_CLASSES, tn)), 1.0, 0.0)
    upper = jnp.where(lax.broadcasted_iota(jnp.int32, (tn, tn), 0) <= lax.broadcasted_iota(jnp.int32, (tn, tn), 1),
                      1.0, 0.0).astype(BF16)
    incl = _dot(onehot.astype(BF16), upper)
    posf = off_ref[:, 0:1] + base_scr[:, 0:1] + incl - 1.0
    base_scr[...] += jnp.broadcast_to(incl[:, tn - 1:tn], base_scr.shape)
    pos_ref[0] = jnp.sum(posf * onehot, axis=0, keepdims=True).astype(jnp.int32)


def _sort_positions(cls, offsets):
    t = cls.shape[1]
    tn = ROW_TILE
    nt = t // tn
    off = jnp.broadcast_to(offsets.astype(F32)[:, None], (N_CLASSES, 128))
    pos = pl.pallas_call(
        _sort_pos_kernel,
        out_shape=jax.ShapeDtypeStruct((nt, 1, tn), jnp.int32),
        grid=(nt,),
        in_specs=[pl.BlockSpec((1, tn), lambda i: (0, i)), pl.BlockSpec((N_CLASSES, 128), lambda i: (0, 0))],
        out_specs=pl.BlockSpec((1, 1, tn), lambda i: (i, 0, 0)),
        scratch_shapes=[pltpu.VMEM((N_CLASSES, 128), F32)],
        compiler_params=_cparams(("arbitrary",)),
        name="moe_sort_positions",
    )(cls, off)
    return pos.reshape(nt, tn)


def _row_copies(n, make):
    def issue(jj, c):
        for p in range(2):
            make(jj * 2 + p).start(priority=p)
        return c

    lax.fori_loop(0, n // 2, issue, 0, unroll=4)

    def drain(j, c):
        make(0).wait()
        return c

    lax.fori_loop(0, n, drain, 0, unroll=8)


def _permute_kernel(pos_hbm, h_ref, g_ref, out_hbm, pos_smem, rows_scr, sem_idx, sem_rows):
    i = pl.program_id(0)
    tok = h_ref.shape[0]
    idx_copy = pltpu.make_async_copy(pos_hbm.at[i], pos_smem, sem_idx)
    idx_copy.start()

    @pl.when(i == 0)
    def _():
        rows_scr[...] = jnp.zeros_like(rows_scr)

    for j in range(SUBROWS):
        rows_scr[pl.ds(j, tok, stride=SLOT_ROWS), :] = h_ref[:, j * 128:(j + 1) * 128].astype(F32)
    rows_scr[pl.ds(SUBROWS, tok, stride=SLOT_ROWS), :] = g_ref[...]
    idx_copy.wait()

    def row_copy(j):
        src = pl.multiple_of(j * SLOT_ROWS, SLOT_ROWS)
        dst = pl.multiple_of(pos_smem[j] * SLOT_ROWS, SLOT_ROWS)
        return pltpu.make_async_copy(rows_scr.at[pl.ds(src, SLOT_ROWS), :], out_hbm.at[pl.ds(dst, SLOT_ROWS), :],
                                     sem_rows)

    _row_copies(tok, row_copy)


def _permute(h2, gates, pos):
    t, d = h2.shape
    tok = _moe_tile(t)
    pos = pos.reshape(t // tok, tok)
    assert d == SUBROWS * 128
    return pl.pallas_call(
        _permute_kernel,
        out_shape=jax.ShapeDtypeStruct((t * SLOT_ROWS, 128), F32),
        grid=(t // tok,),
        in_specs=[pl.BlockSpec(memory_space=pl.ANY),
                  pl.BlockSpec((tok, d), lambda i: (i, 0)),
                  pl.BlockSpec((tok, 128), lambda i: (i, 0))],
        out_specs=pl.BlockSpec(memory_space=pl.ANY),
        scratch_shapes=[pltpu.SMEM((tok,), jnp.int32), pltpu.VMEM((tok * SLOT_ROWS, 128), F32),
                        pltpu.SemaphoreType.DMA, pltpu.SemaphoreType.DMA],
        compiler_params=_cparams(("arbitrary",)),
        name="moe_permute",
    )(pos, h2, gates)


def _unpermute_kernel(pos_hbm, y_hbm, x_ref, gt_ref, o_ref, pos_smem, buf, sem_idx, sem_rows):
    i = pl.program_id(0)
    tok = x_ref.shape[0]
    idx_copy = pltpu.make_async_copy(pos_hbm.at[i], pos_smem, sem_idx)
    idx_copy.start()
    idx_copy.wait()

    def row_copy(j):
        src = pl.multiple_of(pos_smem[j] * SUBROWS, SUBROWS)
        dst = pl.multiple_of(j * SUBROWS, SUBROWS)
        return pltpu.make_async_copy(y_hbm.at[pl.ds(src, SUBROWS), :], buf.at[pl.ds(dst, SUBROWS), :], sem_rows)

    _row_copies(tok, row_copy)
    for s in range(tok // ROW_TILE):
        rows = slice(s * ROW_TILE, (s + 1) * ROW_TILE)
        y = jnp.concatenate([buf[pl.ds(s * ROW_TILE * SUBROWS + j, ROW_TILE, stride=SUBROWS), :]
                             for j in range(SUBROWS)], axis=1)
        o_ref[rows, :] = x_ref[rows, :] + gt_ref[s] * y


def _unpermute(pos, y_rows, x, gt_tiles):
    t, d = x.shape
    tok = _moe_tile(t)
    pos = pos.reshape(t // tok, tok)
    nsub = tok // ROW_TILE
    return pl.pallas_call(
        _unpermute_kernel,
        out_shape=jax.ShapeDtypeStruct((t, d), F32),
        grid=(t // tok,),
        in_specs=[pl.BlockSpec(memory_space=pl.ANY), pl.BlockSpec(memory_space=pl.ANY),
                  pl.BlockSpec((tok, d), lambda i: (i, 0)),
                  pl.BlockSpec((nsub, 1, d), lambda i: (i, 0, 0))],
        out_specs=pl.BlockSpec((tok, d), lambda i: (i, 0)),
        scratch_shapes=[pltpu.SMEM((tok,), jnp.int32), pltpu.VMEM((tok * SUBROWS, 128), F32),
                        pltpu.SemaphoreType.DMA, pltpu.SemaphoreType.DMA],
        compiler_params=_cparams(("arbitrary",)),
        name="moe_unpermute",
    )(pos, y_rows, x, gt_tiles)


def _step_needed(class_counts, n_sub, n_steps):
    ends = jnp.cumsum(class_counts)
    starts = ends - class_counts
    lo = jnp.arange(n_sub, dtype=jnp.int32)[:, None] * ROW_TILE
    present = (class_counts[None, :] > 0) & (starts[None, :] < lo + ROW_TILE) & (ends[None, :] > lo)
    bits = ((jnp.arange(N_CLASSES)[:, None] >> jnp.arange(N_GROUPS)[None, :]) & 1).astype(bool)
    group_needed = jnp.any(present[:, :, None] & bits[None, :, :], axis=1)
    steps_per_group = n_steps // N_GROUPS
    return jnp.repeat(group_needed, steps_per_group, axis=1), starts


def kernel(x, c, ctx, c_ctx, w_mod, b_mod, g_mix, g_ffn, w_in, w_out, a_gq, a_gk, a_lambda, a_gsub, n_gq, n_gk, n_rpb, s5_a_re, s5_a_im, s5_log_dt, s5_b_re, s5_b_im, s5_c_re, s5_c_im, s5_d, s5_w_glu, s5_b_glu, w_router, e_bias, w_gate, w_up, w_down, ws_gate, ws_up, ws_down):
    b, l, d = x.shape
    lc = ctx.shape[1]
    s = lc + l
    depth = w_mod.shape[0]
    n_img_rows = l // GRID_W

    xs = jnp.concatenate([ctx, x], axis=1).astype(F32)
    cond_rows = jnp.zeros((16, d), F32).at[:b].set(c.astype(F32)).at[b].set(c_ctx.astype(F32))
    cos, sa, sb = _rope_tables(s, lc)
    tables = (cos, sa, sb, _block_ones(A_QK_DIM), _block_ones(N_HEAD_DIM))

    for layer in range(depth):
        last = layer == depth - 1
        with_ctx = not last
        lam_init = 0.8 - 0.6 * math.exp(-0.3 * layer)

        mod = _modulation(cond_rows, w_mod[layer].astype(F32), b_mod[layer].astype(F32))
        mod_lat = mod[:b].reshape(b, 1, 6, d)
        mod_ctx = jnp.broadcast_to(mod[b].reshape(1, 1, 6, d), (b, 1, 6, d))
        modall = jnp.concatenate([mod_ctx, mod_lat], axis=1)

        gains = ((jnp.tile(a_gq[layer].astype(F32), HW // A_QK_DIM) * (A_QK_DIM ** -0.5 * LOG2E)).reshape(1, HW),
                 jnp.tile(a_gk[layer].astype(F32), HW // A_QK_DIM).reshape(1, HW),
                 (jnp.tile(n_gq[layer].astype(F32), N_HEADS) * (N_HEAD_DIM ** -0.5 * LOG2E)).reshape(1, HW),
                 jnp.tile(n_gk[layer].astype(F32), N_HEADS).reshape(1, HW))
        qa, kat, va, qn, knt, vn, u = _in_projection(xs, modall, g_mix[layer].astype(F32), w_in[layer].astype(BF16),
                                                     tables, gains, lc)

        lv = a_lambda[layer].astype(F32)
        lam = (jnp.exp(jnp.sum(lv[0] * lv[1])) - jnp.exp(jnp.sum(lv[2] * lv[3])) + lam_init).reshape(1, 1)
        gsub_t = (jnp.tile(a_gsub[layer].astype(F32), A_HEADS) * (1.0 - lam_init)).reshape(1, HW)
        a_bound = BOUND_SLACK * A_QK_DIM * jnp.max(jnp.abs(gains[0])) * jnp.max(jnp.abs(gains[1]))
        oa = lax.cond(a_bound <= SCORE_BOUND_LIMIT,
                      lambda *a: _diff_attention(*a, lc, with_ctx, False),
                      lambda *a: _diff_attention(*a, lc, with_ctx, True), qa, kat, va, lam, gsub_t)

        n_bound = (BOUND_SLACK * N_HEAD_DIM * jnp.max(jnp.abs(gains[2])) * jnp.max(jnp.abs(gains[3]))
                   + LOG2E * jnp.max(jnp.abs(n_rpb[layer].astype(F32))))
        on = lax.cond(n_bound <= SCORE_BOUND_LIMIT,
                      lambda *a: _na_attention(*a, lc, with_ctx, False),
                      lambda *a: _na_attention(*a, lc, with_ctx, True),
                      qn, knt, vn, _na_bias(n_rpb[layer], n_img_rows))

        s5_fwd, s5_bwd = _s5_params(s5_a_re[layer], s5_a_im[layer], s5_log_dt[layer], s5_b_re[layer],
                                    s5_b_im[layer], s5_c_re[layer], s5_c_im[layer], b)
        y_f, y_b = _s5_scan(u, s5_fwd, s5_bwd, lc)

        w_router_pad = jnp.zeros((128, d), F32).at[:N_EXPERTS].set(w_router[layer].astype(F32).T)
        x_new, h2, logits_t = _out_projection(oa, on, y_f, y_b, u, xs, modall, g_ffn[layer].astype(F32),
                                              w_out[layer].astype(BF16), s5_d[layer].astype(F32),
                                              s5_w_glu[layer].astype(BF16), s5_b_glu[layer].astype(F32),
                                              w_router_pad, lc, with_ctx)
        rows = x_new.shape[1]
        t = b * rows
        gates, cls, class_cnt = _route(logits_t, e_bias[layer])
        need, class_start = _step_needed(class_cnt[:, 0].astype(jnp.int32), t // ROW_TILE, N_EXPERTS // MOE_EB)
        pos = _sort_positions(cls, class_start)
        slots = _permute(h2.reshape(t, d), gates, pos)
        y_rows = _moe(slots, need, w_gate[layer].astype(BF16), w_up[layer].astype(BF16),
                      w_down[layer].astype(BF16).reshape(-1, d),
                      ws_gate[layer].astype(BF16), ws_up[layer].astype(BF16), ws_down[layer].astype(BF16))

        gt2 = modall[:, :, 5, :]
        tiles_per_batch = rows // ROW_TILE
        if with_ctx:
            nctx = lc // ROW_TILE
            sel = (jnp.arange(tiles_per_batch) >= nctx).astype(jnp.int32)
        else:
            sel = jnp.ones((tiles_per_batch,), jnp.int32)
        gt_tiles = gt2[:, sel, :].reshape(t // ROW_TILE, 1, d)
        out = _unpermute(pos, y_rows, x_new.reshape(t, d), gt_tiles)
        xs = out.reshape(b, rows, d)

    return xs.astype(x.dtype)
```

```python
import functools
import math

import jax
import jax.numpy as jnp
from jax import lax
from jax.experimental import pallas as pl
from jax.experimental.pallas import tpu as pltpu

F32 = jnp.float32
BF16 = jnp.bfloat16

D_MODEL = 1024
GRID_W = 64
EPS = 1e-6
A_HEADS = 6
A_QK_DIM = 32
A_V_DIM = 64
ROPE_THETA = 10000.0
S5_GROUPS = 16
S5_GROUP_CH = 16
S5_STATE = 64
N_HEADS = 6
N_HEAD_DIM = 64
WIN_ROWS = 8
WIN_COLS = 16
HW = 384
B_WIDTH = 256
Q_COLS = 768
IN_COLS = 2560
N_EXPERTS = 64
TOP_K = 8
N_GROUPS = 8
TOPK_GROUPS = 4
EXPERT_DIM = 256
ROUTED_SCALE = 2.5

ROW_TILE = 256
NA_ROWS = 4
NA_KROWS = 12
S5_CHUNK = 128
MOE_TILE = 1024
MOE_EB = 4
N_CLASSES = 1 << N_GROUPS
SUBROWS = 8
SLOT_ROWS = 16
NEG = -1e30
LOG2E = math.log2(math.e)
SCORE_BOUND_LIMIT = 50.0
BOUND_SLACK = 1.05
VMEM_LIMIT = 56 * 1024 * 1024


def _sigmoid(x):
    return 1.0 / (1.0 + jnp.exp(-x))


def _gelu_tanh(x):
    return 0.5 * x * (1.0 + jnp.tanh(math.sqrt(2.0 / math.pi) * (x + 0.044715 * (x * x * x))))


def _split_bf16(a):
    hi = a.astype(BF16)
    lo = (a - hi.astype(F32)).astype(BF16)
    return hi, lo


def _dot(a, b):
    return jnp.dot(a, b, preferred_element_type=F32)


def _dot3(a, b):
    ah, al = _split_bf16(a)
    bh, bl = _split_bf16(b)
    return _dot(ah, bh) + _dot(ah, bl) + _dot(al, bh)


def _cparams(sem):
    return pltpu.CompilerParams(dimension_semantics=sem, vmem_limit_bytes=VMEM_LIMIT)


def _mod_kernel(c_ref, w_ref, b_ref, o_ref):
    c = c_ref[...]
    cond = c * _sigmoid(c)
    o_ref[...] = _dot3(cond, w_ref[...]) + b_ref[...]


def _modulation(cond_rows, w_mod, b_mod):
    r, d = cond_rows.shape
    n = w_mod.shape[1]
    tn = 1536
    return pl.pallas_call(
        _mod_kernel,
        out_shape=jax.ShapeDtypeStruct((r, n), F32),
        grid=(n // tn,),
        in_specs=[pl.BlockSpec((r, d), lambda j: (0, 0)),
                  pl.BlockSpec((d, tn), lambda j: (0, j)),
                  pl.BlockSpec((1, tn), lambda j: (0, j))],
        out_specs=pl.BlockSpec((r, tn), lambda j: (0, j)),
        compiler_params=_cparams(("arbitrary",)),
        name="adaln_mod",
    )(cond_rows, w_mod, b_mod.reshape(1, n))


def _group_rms(t, ones_ref, gain_ref, group):
    ms = _dot((t * t).astype(BF16), ones_ref[...]) * (1.0 / group)
    return t * lax.rsqrt(ms + EPS) * gain_ref[...]


def _rope(t, cos_ref, sa_ref, sb_ref):
    up = pltpu.roll(t, HW - 8, 1)
    dn = pltpu.roll(t, 8, 1)
    return t * cos_ref[...] + up * sa_ref[...] + dn * sb_ref[...]


def _proj_kernel(x_ref, mod_ref, g_ref, w_ref, cos_ref, sa_ref, sb_ref, ones32_ref, ones64_ref,
                 gqa_ref, gka_ref, gqn_ref, gkn_ref,
                 qa_ref, kat_ref, va_ref, qn_ref, knt_ref, vn_ref, u_ref):
    x = x_ref[0]
    mod = mod_ref[0, 0]
    ms = jnp.mean(x * x, axis=-1, keepdims=True)
    h = x * lax.rsqrt(ms + EPS) * g_ref[...] * (1.0 + mod[1:2]) + mod[0:1]
    hb = h.astype(BF16)

    def sec(a, b):
        return _dot(hb, w_ref[:, a:b])

    qa = _rope(_group_rms(sec(0, 384), ones32_ref, gqa_ref, A_QK_DIM), cos_ref, sa_ref, sb_ref)
    qa_ref[0] = qa.astype(BF16)
    qn_ref[0] = _group_rms(sec(384, 768), ones64_ref, gqn_ref, N_HEAD_DIM).astype(BF16)
    ka = _rope(_group_rms(sec(768, 1152), ones32_ref, gka_ref, A_QK_DIM), cos_ref, sa_ref, sb_ref)
    kat_ref[0] = ka.T.astype(BF16)
    va_ref[0] = sec(1152, 1536).astype(BF16)
    kn = _group_rms(sec(1536, 1920), ones64_ref, gkn_ref, N_HEAD_DIM)
    knt_ref[0] = kn.T.astype(BF16)
    vn_ref[0] = sec(1920, 2304).astype(BF16)
    u_ref[0] = sec(2304, 2560)


def _in_projection(xs, modall, g_mix, w_in_bf, tables, gains, lc):
    b, s, d = xs.shape
    tm = ROW_TILE
    cos, sa, sb, ones32, ones64 = tables
    row = lambda i, bb: (bb, i, 0)
    tab = lambda i, bb: (i, 0)
    const2 = lambda i, bb: (0, 0)
    act = lambda w, dt: jax.ShapeDtypeStruct((b, s, w), dt)
    act_t = jax.ShapeDtypeStruct((b, HW, s), BF16)
    return pl.pallas_call(
        _proj_kernel,
        out_shape=(act(HW, BF16), act_t, act(HW, BF16), act(HW, BF16), act_t, act(HW, BF16), act(B_WIDTH, F32)),
        grid=(s // tm, b),
        in_specs=[pl.BlockSpec((1, tm, d), row),
                  pl.BlockSpec((1, 1, 6, d), lambda i, bb: (bb, jnp.minimum(i, 1), 0, 0)),
                  pl.BlockSpec((1, d), const2),
                  pl.BlockSpec((d, IN_COLS), const2),
                  pl.BlockSpec((tm, HW), tab), pl.BlockSpec((tm, HW), tab), pl.BlockSpec((tm, HW), tab),
                  pl.BlockSpec((HW, HW), const2), pl.BlockSpec((HW, HW), const2),
                  pl.BlockSpec((1, HW), const2), pl.BlockSpec((1, HW), const2),
                  pl.BlockSpec((1, HW), const2), pl.BlockSpec((1, HW), const2)],
        out_specs=(pl.BlockSpec((1, tm, HW), row),
                   pl.BlockSpec((1, HW, tm), lambda i, bb: (bb, 0, i)),
                   pl.BlockSpec((1, tm, HW), row),
                   pl.BlockSpec((1, tm, HW), row),
                   pl.BlockSpec((1, HW, tm), lambda i, bb: (bb, 0, i)),
                   pl.BlockSpec((1, tm, HW), row),
                   pl.BlockSpec((1, tm, B_WIDTH), row)),
        compiler_params=_cparams(("arbitrary", "arbitrary")),
        name="in_proj",
    )(xs, modall, g_mix.reshape(1, d), w_in_bf, cos, sa, sb, ones32, ones64, *gains)


def _rope_tables(s, lc):
    p = jnp.arange(s)
    pos = jnp.maximum(p - lc, 0)
    rows = (pos // GRID_W).astype(F32)
    cols = (pos % GRID_W).astype(F32)
    lane = jnp.arange(HW)
    j32 = lane % A_QK_DIM
    half = j32 // 16
    i16 = j32 % 16
    nf = 8
    inv = ROPE_THETA ** (-(i16 % nf).astype(F32) / nf)
    coord = jnp.where(half[None, :] == 0, rows[:, None], cols[:, None])
    ang = coord * inv[None, :]
    is_lat = (p >= lc)[:, None]
    second = (i16 >= nf)[None, :]
    cos = jnp.where(is_lat, jnp.cos(ang), 1.0)
    sin = jnp.where(is_lat, jnp.sin(ang), 0.0)
    sa = jnp.where(second, 0.0, -sin)
    sb = jnp.where(second, sin, 0.0)
    return cos.astype(F32), sa.astype(F32), sb.astype(F32)


def _block_ones(group):
    g = jnp.arange(HW) // group
    return (g[:, None] == g[None, :]).astype(BF16)


def _diff_attend(q_all, kt_ref, v_ref, lam, gsub_ref, sk, use_max):
    outs = []
    for h in range(A_HEADS):
        v = v_ref[0, 0:sk, h * A_V_DIM:(h + 1) * A_V_DIM]
        parts = []
        for sub in range(2):
            hs = 2 * h + sub
            off = hs * A_QK_DIM
            sc = _dot(q_all[:, off:off + A_QK_DIM], kt_ref[0, off:off + A_QK_DIM, 0:sk])
            e = jnp.exp2(sc - jnp.max(sc, axis=-1, keepdims=True)) if use_max else jnp.exp2(sc)
            parts.append(_dot(e.astype(BF16), v) * (1.0 / jnp.sum(e, axis=-1, keepdims=True)))
        o = parts[0] - lam * parts[1]
        outs.append(o * lax.rsqrt(jnp.mean(o * o, axis=-1, keepdims=True) + EPS))
    return (jnp.concatenate(outs, axis=-1) * gsub_ref[...]).astype(BF16)


def _diff_attn_kernel(q_ref, kt_ref, v_ref, lam_ref, gsub_ref, o_ref, *, lc, ctx_first, use_max):
    lam = lam_ref[...]
    s = kt_ref.shape[2]
    if ctx_first:
        i = pl.program_id(1)

        @pl.when(i == 0)
        def _():
            o_ref[0] = _diff_attend(q_ref[0], kt_ref, v_ref, lam, gsub_ref, lc, use_max)

        @pl.when(i > 0)
        def _():
            o_ref[0] = _diff_attend(q_ref[0], kt_ref, v_ref, lam, gsub_ref, s, use_max)
    else:
        o_ref[0] = _diff_attend(q_ref[0], kt_ref, v_ref, lam, gsub_ref, s, use_max)


def _diff_attention(qa, kat, va, lam, gsub_t, lc, with_ctx, use_max):
    b, s, _ = qa.shape
    tq = ROW_TILE
    off = 0 if with_ctx else lc // tq
    rows_out = s - off * tq
    return pl.pallas_call(
        functools.partial(_diff_attn_kernel, lc=lc, ctx_first=with_ctx, use_max=use_max),
        out_shape=jax.ShapeDtypeStruct((b, rows_out, HW), BF16),
        grid=(b, rows_out // tq),
        in_specs=[pl.BlockSpec((1, tq, HW), lambda bb, i: (bb, i + off, 0)),
                  pl.BlockSpec((1, HW, s), lambda bb, i: (bb, 0, 0)),
                  pl.BlockSpec((1, s, HW), lambda bb, i: (bb, 0, 0)),
                  pl.BlockSpec((1, 1), lambda bb, i: (0, 0)),
                  pl.BlockSpec((1, HW), lambda bb, i: (0, 0))],
        out_specs=pl.BlockSpec((1, tq, HW), lambda bb, i: (bb, i, 0)),
        compiler_params=_cparams(("arbitrary", "arbitrary")),
        name="diff_attn_rowmax" if use_max else "diff_attn",
    )(qa, kat, va, lam, gsub_t)


def _na_ctx_attend(q_all, kt_ref, v_ref, lc, use_max):
    outs = []
    for h in range(N_HEADS):
        hs = slice(h * N_HEAD_DIM, (h + 1) * N_HEAD_DIM)
        sc = _dot(q_all[:, hs], kt_ref[0, hs, 0:lc])
        e = jnp.exp2(sc - jnp.max(sc, axis=-1, keepdims=True)) if use_max else jnp.exp2(sc)
        o = _dot(e.astype(BF16), v_ref[0, 0:lc, hs])
        outs.append(o * (1.0 / jnp.sum(e, axis=-1, keepdims=True)))
    return jnp.concatenate(outs, axis=-1).astype(BF16)


def _na_attend(q_all, kt_ref, v_ref, bias_ref, koff, lc, use_max):
    nk = NA_KROWS * GRID_W
    outs = []
    for h in range(N_HEADS):
        hs = slice(h * N_HEAD_DIM, (h + 1) * N_HEAD_DIM)
        q = q_all[:, hs]
        s_loc = _dot(q, kt_ref[0, hs, pl.ds(koff, nk)]) + bias_ref[0, h]
        s_ctx = _dot(q, kt_ref[0, hs, 0:lc])
        if use_max:
            m = jnp.maximum(jnp.max(s_loc, axis=-1, keepdims=True), jnp.max(s_ctx, axis=-1, keepdims=True))
            s_loc, s_ctx = s_loc - m, s_ctx - m
        e_loc = jnp.exp2(s_loc)
        e_ctx = jnp.exp2(s_ctx)
        den = jnp.sum(e_loc, axis=-1, keepdims=True) + jnp.sum(e_ctx, axis=-1, keepdims=True)
        o = _dot(e_loc.astype(BF16), v_ref[0, pl.ds(koff, nk), hs]) + _dot(e_ctx.astype(BF16), v_ref[0, 0:lc, hs])
        outs.append(o * (1.0 / den))
    return jnp.concatenate(outs, axis=-1).astype(BF16)


def _na_kernel(q_ref, kt_ref, v_ref, bias_ref, o_ref, *, lc, n_img_rows, ctx_first, use_max):
    i = pl.program_id(1)
    blk = i - 1 if ctx_first else i
    start_row = jnp.clip(NA_ROWS * blk - WIN_ROWS // 2, 0, n_img_rows - NA_KROWS)
    koff = pl.multiple_of(lc + start_row * GRID_W, 128)
    if ctx_first:
        @pl.when(i == 0)
        def _():
            o_ref[0] = _na_ctx_attend(q_ref[0], kt_ref, v_ref, lc, use_max)

        @pl.when(i > 0)
        def _():
            o_ref[0] = _na_attend(q_ref[0], kt_ref, v_ref, bias_ref, koff, lc, use_max)
    else:
        o_ref[0] = _na_attend(q_ref[0], kt_ref, v_ref, bias_ref, koff, lc, use_max)


def _na_attention(qn, knt, vn, bias, lc, with_ctx, use_max):
    b, s, _ = qn.shape
    tq = NA_ROWS * GRID_W
    assert tq == ROW_TILE and lc % tq == 0
    n_img_rows = (s - lc) // GRID_W
    nblk = n_img_rows // NA_ROWS
    off = 0 if with_ctx else lc // tq
    rows_out = s - off * tq
    first = 1 if with_ctx else 0

    def variant(bb, i):
        blk = i - first
        return (jnp.where(blk <= 0, 0, jnp.where(blk == nblk - 1, 2, 1)), 0, 0, 0)

    return pl.pallas_call(
        functools.partial(_na_kernel, lc=lc, n_img_rows=n_img_rows, ctx_first=with_ctx, use_max=use_max),
        out_shape=jax.ShapeDtypeStruct((b, rows_out, HW), BF16),
        grid=(b, rows_out // tq),
        in_specs=[pl.BlockSpec((1, tq, HW), lambda bb, i: (bb, i + off, 0)),
                  pl.BlockSpec((1, HW, s), lambda bb, i: (bb, 0, 0)),
                  pl.BlockSpec((1, s, HW), lambda bb, i: (bb, 0, 0)),
                  pl.BlockSpec((1, N_HEADS, tq, NA_KROWS * GRID_W), variant)],
        out_specs=pl.BlockSpec((1, tq, HW), lambda bb, i: (bb, i, 0)),
        compiler_params=_cparams(("arbitrary", "arbitrary")),
        name="nbr_attn_rowmax" if use_max else "nbr_attn",
    )(qn, knt, vn, bias)


def _na_bias(rpb, n_img_rows):
    a = jnp.arange(NA_ROWS)[:, None, None, None]
    cq = jnp.arange(GRID_W)[None, :, None, None]
    j = jnp.arange(NA_KROWS)[None, None, :, None]
    ck = jnp.arange(GRID_W)[None, None, None, :]
    cstart = jnp.clip(cq - WIN_COLS // 2, 0, GRID_W - WIN_COLS)
    colmask = (ck >= cstart) & (ck < cstart + WIN_COLS)
    dc = jnp.clip(ck - cq, -(WIN_COLS - 1), WIN_COLS - 1) + (WIN_COLS - 1)
    by_col = jnp.take(rpb.astype(F32), dc.reshape(-1), axis=2).reshape(N_HEADS, 2 * WIN_ROWS - 1, GRID_W, GRID_W)
    out = []
    for r0_minus_k, wstart in ((0, 0 * a), (WIN_ROWS // 2, a), (NA_KROWS - NA_ROWS, NA_KROWS - WIN_ROWS + 0 * a)):
        inwin = (j >= wstart) & (j < wstart + WIN_ROWS)
        dr = jnp.clip(j - r0_minus_k - a + (WIN_ROWS - 1), 0, 2 * WIN_ROWS - 2)
        vals = jnp.take(by_col, dr.reshape(-1), axis=1).reshape(N_HEADS, NA_ROWS, NA_KROWS, GRID_W, GRID_W)
        vals = jnp.transpose(vals, (0, 1, 3, 2, 4))
        vals = jnp.where((inwin & colmask)[None], vals * LOG2E, NEG)
        out.append(vals.reshape(N_HEADS, NA_ROWS * GRID_W, NA_KROWS * GRID_W))
    return jnp.stack(out)


def _s5_kernel(uf_ref, ub_ref, bmf_ref, cmf_ref, arf_ref, aif_ref, bmb_ref, cmb_ref, arb_ref, aib_ref,
               yf_ref, yb_ref, xf_scr, xb_scr, st_scr, io_scr):
    nb, tc, w = uf_ref.shape
    ns = arf_ref.shape[1]

    @pl.when(pl.program_id(0) == 0)
    def _():
        st_scr[...] = jnp.zeros_like(st_scr)

    nl = w // 128

    def drive(u_ref, bm_ref, x_scr):
        for b in range(nb):
            for c in range(nl):
                io_scr[c, pl.ds(b, tc, stride=nb), :] = u_ref[b, :, c * 128:(c + 1) * 128]
        u_tm = jnp.concatenate([io_scr[c] for c in range(nl)], axis=1)
        x_scr[...] = _dot(u_tm.astype(BF16), bm_ref[...])

    drive(uf_ref, bmf_ref, xf_scr)
    drive(ub_ref, bmb_ref, xb_scr)

    def advance(x_scr, a_re, a_im, s_re, s_im, tt):
        rows = pl.ds(pl.multiple_of(tt * nb, nb), nb)
        n_re = a_re * s_re - a_im * s_im + x_scr[rows, 0:ns]
        n_im = a_re * s_im + a_im * s_re + x_scr[rows, ns:2 * ns]
        x_scr[rows, 0:ns] = n_re
        x_scr[rows, ns:2 * ns] = n_im
        return n_re, n_im

    def step(t, carry):
        f_re, f_im, b_re, b_im = carry
        f_re, f_im = advance(xf_scr, arf_ref[...], aif_ref[...], f_re, f_im, t)
        b_re, b_im = advance(xb_scr, arb_ref[...], aib_ref[...], b_re, b_im, tc - 1 - t)
        return f_re, f_im, b_re, b_im

    init = (st_scr[0, :, 0:ns], st_scr[0, :, ns:2 * ns], st_scr[1, :, 0:ns], st_scr[1, :, ns:2 * ns])
    f_re, f_im, b_re, b_im = lax.fori_loop(0, tc, step, init, unroll=2)
    st_scr[0, :, 0:ns] = f_re
    st_scr[0, :, ns:2 * ns] = f_im
    st_scr[1, :, 0:ns] = b_re
    st_scr[1, :, ns:2 * ns] = b_im

    def readout(x_scr, cm_ref, y_ref):
        y_tm = _dot(x_scr[...].astype(BF16), cm_ref[...])
        for c in range(nl):
            io_scr[c] = y_tm[:, c * 128:(c + 1) * 128]
        for b in range(nb):
            for c in range(nl):
                y_ref[b, :, c * 128:(c + 1) * 128] = io_scr[c, pl.ds(b, tc, stride=nb), :]

    readout(xf_scr, cmf_ref, yf_ref)
    readout(xb_scr, cmb_ref, yb_ref)


def _s5_scan(u, fwd, bwd, lc):
    nb, s, w = u.shape
    tc = S5_CHUNK
    nc, ncc = s // tc, lc // tc
    ns = fwd[2].shape[1]
    chunk_f = lambda j: (0, j, 0)
    chunk_b = lambda j: (0, jnp.where(j < ncc, ncc - 1 - j, nc - 1 - (j - ncc)), 0)
    const = lambda j: (0, 0)
    pspecs = [pl.BlockSpec((w, 2 * ns), const), pl.BlockSpec((2 * ns, w), const),
              pl.BlockSpec((nb, ns), const), pl.BlockSpec((nb, ns), const)]
    out = jax.ShapeDtypeStruct((nb, s, w), F32)
    return pl.pallas_call(
        _s5_kernel,
        out_shape=(out, out),
        grid=(nc,),
        in_specs=[pl.BlockSpec((nb, tc, w), chunk_f), pl.BlockSpec((nb, tc, w), chunk_b)] + pspecs + pspecs,
        out_specs=(pl.BlockSpec((nb, tc, w), chunk_f), pl.BlockSpec((nb, tc, w), chunk_b)),
        scratch_shapes=[pltpu.VMEM((tc * nb, 2 * ns), F32), pltpu.VMEM((tc * nb, 2 * ns), F32),
                        pltpu.VMEM((2, nb, 2 * ns), F32), pltpu.VMEM((w // 128, tc * nb, 128), F32)],
        compiler_params=_cparams(("arbitrary",)),
        name="s5_scan",
    )(u, u, *fwd, *bwd)


def _s5_params(a_re, a_im, log_dt, b_re, b_im, c_re, c_im, nb):
    g, n, p = S5_GROUPS, S5_STATE, S5_GROUP_CH
    lr, li = a_re.astype(F32), a_im.astype(F32)
    dt = jnp.exp(log_dt.astype(F32))[..., None]
    mag = jnp.exp(lr * dt)
    ab_r, ab_i = mag * jnp.cos(li * dt), mag * jnp.sin(li * dt)
    den = lr * lr + li * li
    cf_r = ((ab_r - 1.0) * lr + ab_i * li) / den
    cf_i = (ab_i * lr - (ab_r - 1.0) * li) / den
    br, bi = b_re.astype(F32), b_im.astype(F32)
    bb_r = cf_r[..., None] * br - cf_i[..., None] * bi
    bb_i = cf_r[..., None] * bi + cf_i[..., None] * br
    eye = jnp.eye(g, dtype=F32)
    out = []
    for k in range(2):
        b_r = jnp.einsum('gnp,gh->gphn', bb_r[k], eye).reshape(g * p, g * n)
        b_i = jnp.einsum('gnp,gh->gphn', bb_i[k], eye).reshape(g * p, g * n)
        bmat = jnp.concatenate([b_r, b_i], axis=1).astype(BF16)
        ct = jnp.transpose(c_re[k].astype(F32), (0, 2, 1))
        ci = jnp.transpose(c_im[k].astype(F32), (0, 2, 1))
        c_r = jnp.einsum('gnp,gh->gnhp', ct, eye).reshape(g * n, g * p)
        c_i = jnp.einsum('gnp,gh->gnhp', ci, eye).reshape(g * n, g * p)
        cmat = jnp.concatenate([c_r, -c_i], axis=0).astype(BF16)
        ar = jnp.broadcast_to(ab_r[k].reshape(1, g * n), (nb, g * n))
        ai = jnp.broadcast_to(ab_i[k].reshape(1, g * n), (nb, g * n))
        out.append((bmat, cmat, ar, ai))
    return out


def _out_kernel(oa_ref, on_ref, yf_ref, yb_ref, u_ref, x_ref, mod_ref, gffn_ref, wo_ref, dskip_ref, wglu_ref, bglu_ref,
                wr_ref, xo_ref, h2_ref, lg_ref):
    mod = mod_ref[0, 0]
    nt = lambda a, bb: lax.dot_general(a, bb, (((1,), (1,)), ((), ())), preferred_element_type=F32)
    wh, wl = _split_bf16(wr_ref[...])
    half = x_ref.shape[1] // 2
    for r in (slice(0, half), slice(half, 2 * half)):
        g = _gelu_tanh(yf_ref[0, r, :] + yb_ref[0, r, :] + dskip_ref[...] * u_ref[0, r, :])
        ob = g * _sigmoid(_dot(g.astype(BF16), wglu_ref[...]) + bglu_ref[...])
        mix = (_dot(oa_ref[0, r, :], wo_ref[0:HW, :]) + _dot(ob.astype(BF16), wo_ref[HW:HW + B_WIDTH, :])
               + _dot(on_ref[0, r, :], wo_ref[HW + B_WIDTH:, :]))
        x = x_ref[0, r, :] + mod[2:3] * mix
        xo_ref[0, r, :] = x
        ms = jnp.mean(x * x, axis=-1, keepdims=True)
        h2 = x * lax.rsqrt(ms + EPS) * gffn_ref[...] * (1.0 + mod[4:5]) + mod[3:4]
        h2_ref[0, r, :] = h2.astype(BF16)
        hh, hl = _split_bf16(h2)
        lg_ref[0, :, r] = nt(wh, hh) + nt(wh, hl) + nt(wl, hh)


def _out_projection(oa, on, y_f, y_b, u, xs, modall, g_ffn, w_out_bf, d_skip, w_glu_bf, b_glu, w_router_pad, lc,
                    with_ctx):
    b, s, d = xs.shape
    tm = ROW_TILE
    off = 0 if with_ctx else lc // tm
    rows_out = s - off * tm
    full = lambda bb, i: (bb, i + off, 0)
    outr = lambda bb, i: (bb, i, 0)
    const = lambda bb, i: (0, 0)
    ne = w_router_pad.shape[0]
    return pl.pallas_call(
        _out_kernel,
        out_shape=(jax.ShapeDtypeStruct((b, rows_out, d), F32),
                   jax.ShapeDtypeStruct((b, rows_out, d), BF16),
                   jax.ShapeDtypeStruct((b, ne, rows_out), F32)),
        grid=(b, rows_out // tm),
        in_specs=[pl.BlockSpec((1, tm, HW), outr),
                  pl.BlockSpec((1, tm, HW), outr),
                  pl.BlockSpec((1, tm, B_WIDTH), full),
                  pl.BlockSpec((1, tm, B_WIDTH), full),
                  pl.BlockSpec((1, tm, B_WIDTH), full),
                  pl.BlockSpec((1, tm, d), full),
                  pl.BlockSpec((1, 1, 6, d), lambda bb, i: (bb, jnp.minimum(i + off, 1), 0, 0)),
                  pl.BlockSpec((1, d), const),
                  pl.BlockSpec((d, d), const),
                  pl.BlockSpec((1, B_WIDTH), const),
                  pl.BlockSpec((B_WIDTH, B_WIDTH), const),
                  pl.BlockSpec((1, B_WIDTH), const),
                  pl.BlockSpec((ne, d), const)],
        out_specs=(pl.BlockSpec((1, tm, d), outr), pl.BlockSpec((1, tm, d), outr),
                   pl.BlockSpec((1, ne, tm), lambda bb, i: (bb, 0, i))),
        compiler_params=_cparams(("arbitrary", "arbitrary")),
        name="out_proj",
    )(oa, on, y_f, y_b, u, xs, modall, g_ffn.reshape(1, d), w_out_bf, d_skip.reshape(1, -1), w_glu_bf,
      b_glu.reshape(1, -1), w_router_pad)


def _swiglu(h, wg, wu):
    a = _dot(h, wg)
    return a * _sigmoid(a) * _dot(h, wu)


def _moe_kernel(cnt_ref, sub_ref, held_ref, slots_ref, wg_ref, wu_ref, wd_ref, sg_ref, su_ref, sd_ref, y_ref,
                h_scr, g_scr, acc_ref, *, n_routed):
    i = pl.program_id(0)
    e = pl.program_id(1)
    tm = h_scr.shape[0]
    nsub = tm // ROW_TILE

    @pl.when(e == 0)
    def _():
        for j in range(SUBROWS):
            h_scr[:, j * 128:(j + 1) * 128] = slots_ref[pl.ds(j, tm, stride=SLOT_ROWS), :].astype(BF16)
        g_scr[...] = slots_ref[pl.ds(SUBROWS, tm, stride=SLOT_ROWS), :]
        acc_ref[...] = jnp.zeros_like(acc_ref)

    @pl.when(e < n_routed)
    def _():
        shift = lax.rem(128 - e * MOE_EB, 128)
        entry = i * n_routed + e
        count = cnt_ref[entry]
        for k in range(1, nsub + 1):
            @pl.when(count == k)
            def _():
                rows = [pl.ds(pl.multiple_of(sub_ref[entry * nsub + q] * ROW_TILE, ROW_TILE), ROW_TILE)
                        for q in range(k)]
                h = jnp.concatenate([h_scr[r, :] for r in rows], axis=0)
                gsel = pltpu.roll(jnp.concatenate([g_scr[r, :] for r in rows], axis=0), shift, 1)
                hid = [(_swiglu(h, wg_ref[j], wu_ref[j]) * gsel[:, j:j + 1]).astype(BF16) for j in range(MOE_EB)]
                out = _dot(jnp.concatenate(hid, axis=-1), wd_ref[...])
                for q, r in enumerate(rows):
                    acc_ref[r, :] += out[q * ROW_TILE:(q + 1) * ROW_TILE, :]

    @pl.when(e == n_routed)
    def _():
        for s in range(nsub):
            rows = slice(s * ROW_TILE, (s + 1) * ROW_TILE)
            hs = _swiglu(h_scr[rows, :], sg_ref[...], su_ref[...]).astype(BF16)
            y = acc_ref[rows, :] + _dot(hs, sd_ref[...])
            for j in range(SUBROWS):
                y_ref[pl.ds(s * ROW_TILE * SUBROWS + j, ROW_TILE, stride=SUBROWS), :] = y[:, j * 128:(j + 1) * 128]


def _moe_tile(t):
    return max(m for m in range(ROW_TILE, MOE_TILE + 1, ROW_TILE) if t % m == 0)


def _moe(slots, need, wg, wu, wd, sg, su, sd):
    t = slots.shape[0] // SLOT_ROWS
    d = wg.shape[1]
    tm = _moe_tile(t)
    nsub = tm // ROW_TILE
    n_routed = wg.shape[0] // MOE_EB
    flags = jnp.transpose(need.reshape(t // tm, nsub, n_routed), (0, 2, 1))
    count = jnp.sum(flags, axis=-1).astype(jnp.int32).reshape(-1)
    listed = jnp.argsort(jnp.logical_not(flags), axis=-1, stable=True).astype(jnp.int32).reshape(-1)
    busy = jnp.any(flags, axis=-1)
    steps = jnp.arange(n_routed, dtype=jnp.int32)[None, :]
    held = lax.cummax(jnp.where(busy, steps, -1), axis=1)
    held = jnp.where(held >= 0, held, jnp.argmax(busy, axis=1).astype(jnp.int32)[:, None]).reshape(-1)
    wblk = lambda i, e, c, s, w: w[i * n_routed + jnp.minimum(e, n_routed - 1)]
    const = lambda i, e, c, s, w: (0, 0)
    return pl.pallas_call(
        functools.partial(_moe_kernel, n_routed=n_routed),
        out_shape=jax.ShapeDtypeStruct((t * SUBROWS, 128), F32),
        grid_spec=pltpu.PrefetchScalarGridSpec(
            num_scalar_prefetch=3,
            grid=(t // tm, n_routed + 1),
            in_specs=[pl.BlockSpec((tm * SLOT_ROWS, 128), lambda i, e, c, s, w: (i, 0)),
                      pl.BlockSpec((MOE_EB, d, EXPERT_DIM), lambda i, e, c, s, w: (wblk(i, e, c, s, w), 0, 0)),
                      pl.BlockSpec((MOE_EB, d, EXPERT_DIM), lambda i, e, c, s, w: (wblk(i, e, c, s, w), 0, 0)),
                      pl.BlockSpec((MOE_EB * EXPERT_DIM, d), lambda i, e, c, s, w: (wblk(i, e, c, s, w), 0)),
                      pl.BlockSpec((d, EXPERT_DIM), const),
                      pl.BlockSpec((d, EXPERT_DIM), const),
                      pl.BlockSpec((EXPERT_DIM, d), const)],
            out_specs=pl.BlockSpec((tm * SUBROWS, 128), lambda i, e, c, s, w: (i, 0)),
            scratch_shapes=[pltpu.VMEM((tm, d), BF16), pltpu.VMEM((tm, 128), F32), pltpu.VMEM((tm, d), F32)]),
        compiler_params=_cparams(("arbitrary", "arbitrary")),
        name="moe_ffn",
    )(count, listed, held, slots, wg, wu, wd, sg, su, sd)


def _route_kernel(lg_ref, bias_ref, o_ref, cls_ref, cnt_ref):
    gsz = N_EXPERTS // N_GROUPS
    tn = lg_ref.shape[2]
    ninf = -jnp.inf
    jidx = lax.broadcasted_iota(jnp.int32, (gsz, tn), 0)
    scores, biased, gscore = [], [], []
    for g in range(N_GROUPS):
        rows = slice(g * gsz, (g + 1) * gsz)
        sc = _sigmoid(lg_ref[0, rows, :])
        bi = sc + bias_ref[rows, :]
        m1 = jnp.max(bi, axis=0, keepdims=True)
        first = jnp.min(jnp.where(bi == m1, jidx, gsz), axis=0, keepdims=True)
        m2 = jnp.max(jnp.where(jidx == first, ninf, bi), axis=0, keepdims=True)
        scores.append(sc)
        biased.append(bi)
        gscore.append(m1 + m2)
    masked = []
    cls = jnp.zeros((1, tn), F32)
    for g in range(N_GROUPS):
        rank = jnp.zeros((1, tn), F32)
        for g2 in range(N_GROUPS):
            if g2 != g:
                ahead = (gscore[g2] >= gscore[g]) if g2 < g else (gscore[g2] > gscore[g])
                rank = rank + jnp.where(ahead, 1.0, 0.0)
        cls = cls + jnp.where(rank < TOPK_GROUPS, float(1 << g), 0.0)
        keep = jnp.broadcast_to(rank, (gsz, tn)) < TOPK_GROUPS
        masked.append(jnp.where(keep, biased[g], ninf))
    cls_ref[...] = cls
    onehot = jnp.where(lax.broadcasted_iota(jnp.int32, (N_CLASSES, tn), 0).astype(F32)
                       == jnp.broadcast_to(cls, (N_CLASSES, tn)), 1.0, 0.0)

    @pl.when((pl.program_id(0) == 0) & (pl.program_id(1) == 0))
    def _():
        cnt_ref[...] = jnp.zeros_like(cnt_ref)

    cnt_ref[...] += jnp.broadcast_to(jnp.sum(onehot, axis=1, keepdims=True), cnt_ref.shape)
    ranks = [jnp.zeros((gsz, tn), F32) for _ in range(N_GROUPS)]
    for g2 in range(N_GROUPS):
        for j2 in range(gsz):
            other = jnp.broadcast_to(masked[g2][j2:j2 + 1, :], (gsz, tn))
            for g in range(N_GROUPS):
                ge = jnp.where(other >= masked[g], 1.0, 0.0)
                gt = jnp.where(other > masked[g], 1.0, 0.0)
                if g2 < g:
                    ahead = ge
                elif g2 > g:
                    ahead = gt
                else:
                    ahead = jnp.where(jidx > j2, ge, gt)
                ranks[g] = ranks[g] + ahead
    picked = [jnp.where(ranks[g] < TOP_K, scores[g], 0.0) for g in range(N_GROUPS)]
    den = sum(jnp.sum(p, axis=0, keepdims=True) for p in picked)
    scale = ROUTED_SCALE / den
    shared_row = jnp.where(lax.broadcasted_iota(jnp.int32, (128 - N_EXPERTS, tn), 0) == 0, 1.0, 0.0)
    gates_t = jnp.concatenate([p * scale for p in picked] + [shared_row], axis=0)
    o_ref[...] = gates_t.T


def _route(logits_t, e_bias):
    b, ne, rows = logits_t.shape
    tn = ROW_TILE
    nt = rows // tn
    bias = jnp.broadcast_to(e_bias.astype(F32)[:, None], (N_EXPERTS, tn))
    return pl.pallas_call(
        _route_kernel,
        out_shape=(jax.ShapeDtypeStruct((b * rows, ne), F32), jax.ShapeDtypeStruct((1, b * rows), F32),
                   jax.ShapeDtypeStruct((N_CLASSES, 128), F32)),
        grid=(b, nt),
        in_specs=[pl.BlockSpec((1, ne, tn), lambda bb, i: (bb, 0, i)),
                  pl.BlockSpec((N_EXPERTS, tn), lambda bb, i: (0, 0))],
        out_specs=(pl.BlockSpec((tn, ne), lambda bb, i: (bb * nt + i, 0)),
                   pl.BlockSpec((1, tn), lambda bb, i: (0, bb * nt + i)),
                   pl.BlockSpec((N_CLASSES, 128), lambda bb, i: (0, 0))),
        compiler_params=_cparams(("arbitrary", "arbitrary")),
        name="moe_route",
    )(logits_t, bias)


def _sort_pos_kernel(cls_ref, off_ref, pos_ref, base_scr):
    tn = cls_ref.shape[1]

    @pl.when(pl.program_id(0) == 0)
    def _():
        base_scr[...] = jnp.zeros_like(base_scr)

    onehot = jnp.where(lax.broadcasted_iota(jnp.int32, (N_CLASSES, tn), 0).astype(F32)
                       == jnp.broadcast_to(cls_ref[...], (N_CLASSES, tn)), 1.0, 0.0)
    upper = jnp.where(lax.broadcasted_iota(jnp.int32, (tn, tn), 0) <= lax.broadcasted_iota(jnp.int32, (tn, tn), 1),
                      1.0, 0.0).astype(BF16)
    incl = _dot(onehot.astype(BF16), upper)
    posf = off_ref[:, 0:1] + base_scr[:, 0:1] + incl - 1.0
    base_scr[...] += jnp.broadcast_to(incl[:, tn - 1:tn], base_scr.shape)
    pos_ref[0] = jnp.sum(posf * onehot, axis=0, keepdims=True).astype(jnp.int32)


def _sort_positions(cls, offsets):
    t = cls.shape[1]
    tn = ROW_TILE
    nt = t // tn
    off = jnp.broadcast_to(offsets.astype(F32)[:, None], (N_CLASSES, 128))
    pos = pl.pallas_call(
        _sort_pos_kernel,
        out_shape=jax.ShapeDtypeStruct((nt, 1, tn), jnp.int32),
        grid=(nt,),
        in_specs=[pl.BlockSpec((1, tn), lambda i: (0, i)), pl.BlockSpec((N_CLASSES, 128), lambda i: (0, 0))],
        out_specs=pl.BlockSpec((1, 1, tn), lambda i: (i, 0, 0)),
        scratch_shapes=[pltpu.VMEM((N_CLASSES, 128), F32)],
        compiler_params=_cparams(("arbitrary",)),
        name="moe_sort_positions",
    )(cls, off)
    return pos.reshape(nt, tn)


def _row_copies(n, make):
    def issue(jj, c):
        for p in range(2):
            make(jj * 2 + p).start(priority=p)
        return c

    lax.fori_loop(0, n // 2, issue, 0, unroll=4)

    def drain(j, c):
        make(0).wait()
        return c

    lax.fori_loop(0, n, drain, 0, unroll=8)


def _permute_kernel(pos_hbm, h_ref, g_ref, out_hbm, pos_smem, rows_scr, sem_idx, sem_rows):
    i = pl.program_id(0)
    tok = h_ref.shape[0]
    idx_copy = pltpu.make_async_copy(pos_hbm.at[i], pos_smem, sem_idx)
    idx_copy.start()

    @pl.when(i == 0)
    def _():
        rows_scr[...] = jnp.zeros_like(rows_scr)

    for j in range(SUBROWS):
        rows_scr[pl.ds(j, tok, stride=SLOT_ROWS), :] = h_ref[:, j * 128:(j + 1) * 128].astype(F32)
    rows_scr[pl.ds(SUBROWS, tok, stride=SLOT_ROWS), :] = g_ref[...]
    idx_copy.wait()

    def row_copy(j):
        src = pl.multiple_of(j * SLOT_ROWS, SLOT_ROWS)
        dst = pl.multiple_of(pos_smem[j] * SLOT_ROWS, SLOT_ROWS)
        return pltpu.make_async_copy(rows_scr.at[pl.ds(src, SLOT_ROWS), :], out_hbm.at[pl.ds(dst, SLOT_ROWS), :],
                                     sem_rows)

    _row_copies(tok, row_copy)


def _permute(h2, gates, pos):
    t, d = h2.shape
    tok = _moe_tile(t)
    pos = pos.reshape(t // tok, tok)
    assert d == SUBROWS * 128
    return pl.pallas_call(
        _permute_kernel,
        out_shape=jax.ShapeDtypeStruct((t * SLOT_ROWS, 128), F32),
        grid=(t // tok,),
        in_specs=[pl.BlockSpec(memory_space=pl.ANY),
                  pl.BlockSpec((tok, d), lambda i: (i, 0)),
                  pl.BlockSpec((tok, 128), lambda i: (i, 0))],
        out_specs=pl.BlockSpec(memory_space=pl.ANY),
        scratch_shapes=[pltpu.SMEM((tok,), jnp.int32), pltpu.VMEM((tok * SLOT_ROWS, 128), F32),
                        pltpu.SemaphoreType.DMA, pltpu.SemaphoreType.DMA],
        compiler_params=_cparams(("arbitrary",)),
        name="moe_permute",
    )(pos, h2, gates)


def _unpermute_kernel(pos_hbm, y_hbm, x_ref, gt_ref, o_ref, pos_smem, buf, sem_idx, sem_rows):
    i = pl.program_id(0)
    tok = x_ref.shape[0]
    idx_copy = pltpu.make_async_copy(pos_hbm.at[i], pos_smem, sem_idx)
    idx_copy.start()
    idx_copy.wait()

    def row_copy(j):
        src = pl.multiple_of(pos_smem[j] * SUBROWS, SUBROWS)
        dst = pl.multiple_of(j * SUBROWS, SUBROWS)
        return pltpu.make_async_copy(y_hbm.at[pl.ds(src, SUBROWS), :], buf.at[pl.ds(dst, SUBROWS), :], sem_rows)

    _row_copies(tok, row_copy)
    for s in range(tok // ROW_TILE):
        rows = slice(s * ROW_TILE, (s + 1) * ROW_TILE)
        y = jnp.concatenate([buf[pl.ds(s * ROW_TILE * SUBROWS + j, ROW_TILE, stride=SUBROWS), :]
                             for j in range(SUBROWS)], axis=1)
        o_ref[rows, :] = x_ref[rows, :] + gt_ref[s] * y


def _unpermute(pos, y_rows, x, gt_tiles):
    t, d = x.shape
    tok = _moe_tile(t)
    pos = pos.reshape(t // tok, tok)
    nsub = tok // ROW_TILE
    return pl.pallas_call(
        _unpermute_kernel,
        out_shape=jax.ShapeDtypeStruct((t, d), F32),
        grid=(t // tok,),
        in_specs=[pl.BlockSpec(memory_space=pl.ANY), pl.BlockSpec(memory_space=pl.ANY),
                  pl.BlockSpec((tok, d), lambda i: (i, 0)),
                  pl.BlockSpec((nsub, 1, d), lambda i: (i, 0, 0))],
        out_specs=pl.BlockSpec((tok, d), lambda i: (i, 0)),
        scratch_shapes=[pltpu.SMEM((tok,), jnp.int32), pltpu.VMEM((tok * SUBROWS, 128), F32),
                        pltpu.SemaphoreType.DMA, pltpu.SemaphoreType.DMA],
        compiler_params=_cparams(("arbitrary",)),
        name="moe_unpermute",
    )(pos, y_rows, x, gt_tiles)


def _step_needed(class_counts, n_sub, n_steps):
    ends = jnp.cumsum(class_counts)
    starts = ends - class_counts
    lo = jnp.arange(n_sub, dtype=jnp.int32)[:, None] * ROW_TILE
    present = (class_counts[None, :] > 0) & (starts[None, :] < lo + ROW_TILE) & (ends[None, :] > lo)
    bits = ((jnp.arange(N_CLASSES)[:, None] >> jnp.arange(N_GROUPS)[None, :]) & 1).astype(bool)
    group_needed = jnp.any(present[:, :, None] & bits[None, :, :], axis=1)
    steps_per_group = n_steps // N_GROUPS
    return jnp.repeat(group_needed, steps_per_group, axis=1), starts


def kernel(x, c, ctx, c_ctx, w_mod, b_mod, g_mix, g_ffn, w_in, w_out, a_gq, a_gk, a_lambda, a_gsub, n_gq, n_gk, n_rpb, s5_a_re, s5_a_im, s5_log_dt, s5_b_re, s5_b_im, s5_c_re, s5_c_im, s5_d, s5_w_glu, s5_b_glu, w_router, e_bias, w_gate, w_up, w_down, ws_gate, ws_up, ws_down):
    b, l, d = x.shape
    lc = ctx.shape[1]
    s = lc + l
    depth = w_mod.shape[0]
    n_img_rows = l // GRID_W

    xs = jnp.concatenate([ctx, x], axis=1).astype(F32)
    cond_rows = jnp.zeros((16, d), F32).at[:b].set(c.astype(F32)).at[b].set(c_ctx.astype(F32))
    cos, sa, sb = _rope_tables(s, lc)
    tables = (cos, sa, sb, _block_ones(A_QK_DIM), _block_ones(N_HEAD_DIM))

    for layer in range(depth):
        last = layer == depth - 1
        with_ctx = not last
        lam_init = 0.8 - 0.6 * math.exp(-0.3 * layer)

        mod = _modulation(cond_rows, w_mod[layer].astype(F32), b_mod[layer].astype(F32))
        mod_lat = mod[:b].reshape(b, 1, 6, d)
        mod_ctx = jnp.broadcast_to(mod[b].reshape(1, 1, 6, d), (b, 1, 6, d))
        modall = jnp.concatenate([mod_ctx, mod_lat], axis=1)

        gains = ((jnp.tile(a_gq[layer].astype(F32), HW // A_QK_DIM) * (A_QK_DIM ** -0.5 * LOG2E)).reshape(1, HW),
                 jnp.tile(a_gk[layer].astype(F32), HW // A_QK_DIM).reshape(1, HW),
                 (jnp.tile(n_gq[layer].astype(F32), N_HEADS) * (N_HEAD_DIM ** -0.5 * LOG2E)).reshape(1, HW),
                 jnp.tile(n_gk[layer].astype(F32), N_HEADS).reshape(1, HW))
        qa, kat, va, qn, knt, vn, u = _in_projection(xs, modall, g_mix[layer].astype(F32), w_in[layer].astype(BF16),
                                                     tables, gains, lc)

        lv = a_lambda[layer].astype(F32)
        lam = (jnp.exp(jnp.sum(lv[0] * lv[1])) - jnp.exp(jnp.sum(lv[2] * lv[3])) + lam_init).reshape(1, 1)
        gsub_t = (jnp.tile(a_gsub[layer].astype(F32), A_HEADS) * (1.0 - lam_init)).reshape(1, HW)
        a_bound = BOUND_SLACK * A_QK_DIM * jnp.max(jnp.abs(gains[0])) * jnp.max(jnp.abs(gains[1]))
        oa = lax.cond(a_bound <= SCORE_BOUND_LIMIT,
                      lambda *a: _diff_attention(*a, lc, with_ctx, False),
                      lambda *a: _diff_attention(*a, lc, with_ctx, True), qa, kat, va, lam, gsub_t)

        n_bound = (BOUND_SLACK * N_HEAD_DIM * jnp.max(jnp.abs(gains[2])) * jnp.max(jnp.abs(gains[3]))
                   + LOG2E * jnp.max(jnp.abs(n_rpb[layer].astype(F32))))
        on = lax.cond(n_bound <= SCORE_BOUND_LIMIT,
                      lambda *a: _na_attention(*a, lc, with_ctx, False),
                      lambda *a: _na_attention(*a, lc, with_ctx, True),
                      qn, knt, vn, _na_bias(n_rpb[layer], n_img_rows))

        s5_fwd, s5_bwd = _s5_params(s5_a_re[layer], s5_a_im[layer], s5_log_dt[layer], s5_b_re[layer],
                                    s5_b_im[layer], s5_c_re[layer], s5_c_im[layer], b)
        y_f, y_b = _s5_scan(u, s5_fwd, s5_bwd, lc)

        w_router_pad = jnp.zeros((128, d), F32).at[:N_EXPERTS].set(w_router[layer].astype(F32).T)
        x_new, h2, logits_t = _out_projection(oa, on, y_f, y_b, u, xs, modall, g_ffn[layer].astype(F32),
                                              w_out[layer].astype(BF16), s5_d[layer].astype(F32),
                                              s5_w_glu[layer].astype(BF16), s5_b_glu[layer].astype(F32),
                                              w_router_pad, lc, with_ctx)
        rows = x_new.shape[1]
        t = b * rows
        gates, cls, class_cnt = _route(logits_t, e_bias[layer])
        need, class_start = _step_needed(class_cnt[:, 0].astype(jnp.int32), t // ROW_TILE, N_EXPERTS // MOE_EB)
        pos = _sort_positions(cls, class_start)
        slots = _permute(h2.reshape(t, d), gates, pos)
        y_rows = _moe(slots, need, w_gate[layer].astype(BF16), w_up[layer].astype(BF16),
                      w_down[layer].astype(BF16).reshape(-1, d),
                      ws_gate[layer].astype(BF16), ws_up[layer].astype(BF16), ws_down[layer].astype(BF16))

        gt2 = modall[:, :, 5, :]
        tiles_per_batch = rows // ROW_TILE
        if with_ctx:
            nctx = lc // ROW_TILE
            sel = (jnp.arange(tiles_per_batch) >= nctx).astype(jnp.int32)
        else:
            sel = jnp.ones((tiles_per_batch,), jnp.int32)
        gt_tiles = gt2[:, sel, :].reshape(t // ROW_TILE, 1, d)
        out = _unpermute(pos, y_rows, x_new.reshape(t, d), gt_tiles)
        xs = out.reshape(b, rows, d)

    return xs.astype(x.dtype)
```

```python
import functools
import math

import jax
import jax.numpy as jnp
from jax import lax
from jax.experimental import pallas as pl
from jax.experimental.pallas import tpu as pltpu

F32 = jnp.float32
BF16 = jnp.bfloat16

D_MODEL = 1024
GRID_W = 64
EPS = 1e-6
A_HEADS = 6
A_QK_DIM = 32
A_V_DIM = 64
ROPE_THETA = 10000.0
S5_GROUPS = 16
S5_GROUP_CH = 16
S5_STATE = 64
N_HEADS = 6
N_HEAD_DIM = 64
WIN_ROWS = 8
WIN_COLS = 16
HW = 384
B_WIDTH = 256
Q_COLS = 768
IN_COLS = 2560
N_EXPERTS = 64
TOP_K = 8
N_GROUPS = 8
TOPK_GROUPS = 4
EXPERT_DIM = 256
ROUTED_SCALE = 2.5

ROW_TILE = 256
NA_ROWS = 4
NA_KROWS = 12
S5_CHUNK = 128
MOE_TILE = 1024
MOE_EB = 4
N_CLASSES = 1 << N_GROUPS
SUBROWS = 8
SLOT_ROWS = 16
NEG = -1e30
LOG2E = math.log2(math.e)
SCORE_BOUND_LIMIT = 50.0
BOUND_SLACK = 1.05
VMEM_LIMIT = 56 * 1024 * 1024


def _sigmoid(x):
    return 1.0 / (1.0 + jnp.exp(-x))


def _gelu_tanh(x):
    return 0.5 * x * (1.0 + jnp.tanh(math.sqrt(2.0 / math.pi) * (x + 0.044715 * (x * x * x))))


def _split_bf16(a):
    hi = a.astype(BF16)
    lo = (a - hi.astype(F32)).astype(BF16)
    return hi, lo


def _dot(a, b):
    return jnp.dot(a, b, preferred_element_type=F32)


def _dot3(a, b):
    ah, al = _split_bf16(a)
    bh, bl = _split_bf16(b)
    return _dot(ah, bh) + _dot(ah, bl) + _dot(al, bh)


def _cparams(sem):
    return pltpu.CompilerParams(dimension_semantics=sem, vmem_limit_bytes=VMEM_LIMIT)


def _mod_kernel(c_ref, w_ref, b_ref, o_ref):
    c = c_ref[...]
    cond = c * _sigmoid(c)
    o_ref[...] = _dot3(cond, w_ref[...]) + b_ref[...]


def _modulation(cond_rows, w_mod, b_mod):
    r, d = cond_rows.shape
    n = w_mod.shape[1]
    tn = 1536
    return pl.pallas_call(
        _mod_kernel,
        out_shape=jax.ShapeDtypeStruct((r, n), F32),
        grid=(n // tn,),
        in_specs=[pl.BlockSpec((r, d), lambda j: (0, 0)),
                  pl.BlockSpec((d, tn), lambda j: (0, j)),
                  pl.BlockSpec((1, tn), lambda j: (0, j))],
        out_specs=pl.BlockSpec((r, tn), lambda j: (0, j)),
        compiler_params=_cparams(("arbitrary",)),
        name="adaln_mod",
    )(cond_rows, w_mod, b_mod.reshape(1, n))


def _group_rms(t, ones_ref, gain_ref, group):
    ms = _dot((t * t).astype(BF16), ones_ref[...]) * (1.0 / group)
    return t * lax.rsqrt(ms + EPS) * gain_ref[...]


def _rope(t, cos_ref, sa_ref, sb_ref):
    up = pltpu.roll(t, HW - 8, 1)
    dn = pltpu.roll(t, 8, 1)
    return t * cos_ref[...] + up * sa_ref[...] + dn * sb_ref[...]


def _proj_kernel(x_ref, mod_ref, g_ref, w_ref, cos_ref, sa_ref, sb_ref, ones32_ref, ones64_ref,
                 gqa_ref, gka_ref, gqn_ref, gkn_ref,
                 qa_ref, kat_ref, va_ref, qn_ref, knt_ref, vn_ref, u_ref):
    x = x_ref[0]
    mod = mod_ref[0, 0]
    ms = jnp.mean(x * x, axis=-1, keepdims=True)
    h = x * lax.rsqrt(ms + EPS) * g_ref[...] * (1.0 + mod[1:2]) + mod[0:1]
    hb = h.astype(BF16)

    def sec(a, b):
        return _dot(hb, w_ref[:, a:b])

    qa = _rope(_group_rms(sec(0, 384), ones32_ref, gqa_ref, A_QK_DIM), cos_ref, sa_ref, sb_ref)
    qa_ref[0] = qa.astype(BF16)
    qn_ref[0] = _group_rms(sec(384, 768), ones64_ref, gqn_ref, N_HEAD_DIM).astype(BF16)
    ka = _rope(_group_rms(sec(768, 1152), ones32_ref, gka_ref, A_QK_DIM), cos_ref, sa_ref, sb_ref)
    kat_ref[0] = ka.T.astype(BF16)
    va_ref[0] = sec(1152, 1536).astype(BF16)
    kn = _group_rms(sec(1536, 1920), ones64_ref, gkn_ref, N_HEAD_DIM)
    knt_ref[0] = kn.T.astype(BF16)
    vn_ref[0] = sec(1920, 2304).astype(BF16)
    u_ref[0] = sec(2304, 2560)


def _in_projection(xs, modall, g_mix, w_in_bf, tables, gains, lc):
    b, s, d = xs.shape
    tm = ROW_TILE
    cos, sa, sb, ones32, ones64 = tables
    row = lambda i, bb: (bb, i, 0)
    tab = lambda i, bb: (i, 0)
    const2 = lambda i, bb: (0, 0)
    act = lambda w, dt: jax.ShapeDtypeStruct((b, s, w), dt)
    act_t = jax.ShapeDtypeStruct((b, HW, s), BF16)
    return pl.pallas_call(
        _proj_kernel,
        out_shape=(act(HW, BF16), act_t, act(HW, BF16), act(HW, BF16), act_t, act(HW, BF16), act(B_WIDTH, F32)),
        grid=(s // tm, b),
        in_specs=[pl.BlockSpec((1, tm, d), row),
                  pl.BlockSpec((1, 1, 6, d), lambda i, bb: (bb, jnp.minimum(i, 1), 0, 0)),
                  pl.BlockSpec((1, d), const2),
                  pl.BlockSpec((d, IN_COLS), const2),
                  pl.BlockSpec((tm, HW), tab), pl.BlockSpec((tm, HW), tab), pl.BlockSpec((tm, HW), tab),
                  pl.BlockSpec((HW, HW), const2), pl.BlockSpec((HW, HW), const2),
                  pl.BlockSpec((1, HW), const2), pl.BlockSpec((1, HW), const2),
                  pl.BlockSpec((1, HW), const2), pl.BlockSpec((1, HW), const2)],
        out_specs=(pl.BlockSpec((1, tm, HW), row),
                   pl.BlockSpec((1, HW, tm), lambda i, bb: (bb, 0, i)),
                   pl.BlockSpec((1, tm, HW), row),
                   pl.BlockSpec((1, tm, HW), row),
                   pl.BlockSpec((1, HW, tm), lambda i, bb: (bb, 0, i)),
                   pl.BlockSpec((1, tm, HW), row),
                   pl.BlockSpec((1, tm, B_WIDTH), row)),
        compiler_params=_cparams(("arbitrary", "arbitrary")),
        name="in_proj",
    )(xs, modall, g_mix.reshape(1, d), w_in_bf, cos, sa, sb, ones32, ones64, *gains)


def _rope_tables(s, lc):
    p = jnp.arange(s)
    pos = jnp.maximum(p - lc, 0)
    rows = (pos // GRID_W).astype(F32)
    cols = (pos % GRID_W).astype(F32)
    lane = jnp.arange(HW)
    j32 = lane % A_QK_DIM
    half = j32 // 16
    i16 = j32 % 16
    nf = 8
    inv = ROPE_THETA ** (-(i16 % nf).astype(F32) / nf)
    coord = jnp.where(half[None, :] == 0, rows[:, None], cols[:, None])
    ang = coord * inv[None, :]
    is_lat = (p >= lc)[:, None]
    second = (i16 >= nf)[None, :]
    cos = jnp.where(is_lat, jnp.cos(ang), 1.0)
    sin = jnp.where(is_lat, jnp.sin(ang), 0.0)
    sa = jnp.where(second, 0.0, -sin)
    sb = jnp.where(second, sin, 0.0)
    return cos.astype(F32), sa.astype(F32), sb.astype(F32)


def _block_ones(group):
    g = jnp.arange(HW) // group
    return (g[:, None] == g[None, :]).astype(BF16)


def _diff_attend(q_all, kt_ref, v_ref, lam, gsub_ref, sk, use_max):
    outs = []
    for h in range(A_HEADS):
        v = v_ref[0, 0:sk, h * A_V_DIM:(h + 1) * A_V_DIM]
        parts = []
        for sub in range(2):
            hs = 2 * h + sub
            off = hs * A_QK_DIM
            sc = _dot(q_all[:, off:off + A_QK_DIM], kt_ref[0, off:off + A_QK_DIM, 0:sk])
            e = jnp.exp2(sc - jnp.max(sc, axis=-1, keepdims=True)) if use_max else jnp.exp2(sc)
            parts.append(_dot(e.astype(BF16), v) * (1.0 / jnp.sum(e, axis=-1, keepdims=True)))
        o = parts[0] - lam * parts[1]
        outs.append(o * lax.rsqrt(jnp.mean(o * o, axis=-1, keepdims=True) + EPS))
    return (jnp.concatenate(outs, axis=-1) * gsub_ref[...]).astype(BF16)


def _diff_attn_kernel(q_ref, kt_ref, v_ref, lam_ref, gsub_ref, o_ref, *, lc, ctx_first, use_max):
    lam = lam_ref[...]
    s = kt_ref.shape[2]
    if ctx_first:
        i = pl.program_id(1)

        @pl.when(i == 0)
        def _():
            o_ref[0] = _diff_attend(q_ref[0], kt_ref, v_ref, lam, gsub_ref, lc, use_max)

        @pl.when(i > 0)
        def _():
            o_ref[0] = _diff_attend(q_ref[0], kt_ref, v_ref, lam, gsub_ref, s, use_max)
    else:
        o_ref[0] = _diff_attend(q_ref[0], kt_ref, v_ref, lam, gsub_ref, s, use_max)


def _diff_attention(qa, kat, va, lam, gsub_t, lc, with_ctx, use_max):
    b, s, _ = qa.shape
    tq = ROW_TILE
    off = 0 if with_ctx else lc // tq
    rows_out = s - off * tq
    return pl.pallas_call(
        functools.partial(_diff_attn_kernel, lc=lc, ctx_first=with_ctx, use_max=use_max),
        out_shape=jax.ShapeDtypeStruct((b, rows_out, HW), BF16),
        grid=(b, rows_out // tq),
        in_specs=[pl.BlockSpec((1, tq, HW), lambda bb, i: (bb, i + off, 0)),
                  pl.BlockSpec((1, HW, s), lambda bb, i: (bb, 0, 0)),
                  pl.BlockSpec((1, s, HW), lambda bb, i: (bb, 0, 0)),
                  pl.BlockSpec((1, 1), lambda bb, i: (0, 0)),
                  pl.BlockSpec((1, HW), lambda bb, i: (0, 0))],
        out_specs=pl.BlockSpec((1, tq, HW), lambda bb, i: (bb, i, 0)),
        compiler_params=_cparams(("arbitrary", "arbitrary")),
        name="diff_attn_rowmax" if use_max else "diff_attn",
    )(qa, kat, va, lam, gsub_t)


def _na_ctx_attend(q_all, kt_ref, v_ref, lc, use_max):
    outs = []
    for h in range(N_HEADS):
        hs = slice(h * N_HEAD_DIM, (h + 1) * N_HEAD_DIM)
        sc = _dot(q_all[:, hs], kt_ref[0, hs, 0:lc])
        e = jnp.exp2(sc - jnp.max(sc, axis=-1, keepdims=True)) if use_max else jnp.exp2(sc)
        o = _dot(e.astype(BF16), v_ref[0, 0:lc, hs])
        outs.append(o * (1.0 / jnp.sum(e, axis=-1, keepdims=True)))
    return jnp.concatenate(outs, axis=-1).astype(BF16)


def _na_attend(q_all, kt_ref, v_ref, bias_ref, koff, lc, use_max):
    nk = NA_KROWS * GRID_W
    outs = []
    for h in range(N_HEADS):
        hs = slice(h * N_HEAD_DIM, (h + 1) * N_HEAD_DIM)
        q = q_all[:, hs]
        s_loc = _dot(q, kt_ref[0, hs, pl.ds(koff, nk)]) + bias_ref[0, h]
        s_ctx = _dot(q, kt_ref[0, hs, 0:lc])
        if use_max:
            m = jnp.maximum(jnp.max(s_loc, axis=-1, keepdims=True), jnp.max(s_ctx, axis=-1, keepdims=True))
            s_loc, s_ctx = s_loc - m, s_ctx - m
        e_loc = jnp.exp2(s_loc)
        e_ctx = jnp.exp2(s_ctx)
        den = jnp.sum(e_loc, axis=-1, keepdims=True) + jnp.sum(e_ctx, axis=-1, keepdims=True)
        o = _dot(e_loc.astype(BF16), v_ref[0, pl.ds(koff, nk), hs]) + _dot(e_ctx.astype(BF16), v_ref[0, 0:lc, hs])
        outs.append(o * (1.0 / den))
    return jnp.concatenate(outs, axis=-1).astype(BF16)


def _na_kernel(q_ref, kt_ref, v_ref, bias_ref, o_ref, *, lc, n_img_rows, ctx_first, use_max):
    i = pl.program_id(1)
    blk = i - 1 if ctx_first else i
    start_row = jnp.clip(NA_ROWS * blk - WIN_ROWS // 2, 0, n_img_rows - NA_KROWS)
    koff = pl.multiple_of(lc + start_row * GRID_W, 128)
    if ctx_first:
        @pl.when(i == 0)
        def _():
            o_ref[0] = _na_ctx_attend(q_ref[0], kt_ref, v_ref, lc, use_max)

        @pl.when(i > 0)
        def _():
            o_ref[0] = _na_attend(q_ref[0], kt_ref, v_ref, bias_ref, koff, lc, use_max)
    else:
        o_ref[0] = _na_attend(q_ref[0], kt_ref, v_ref, bias_ref, koff, lc, use_max)


def _na_attention(qn, knt, vn, bias, lc, with_ctx, use_max):
    b, s, _ = qn.shape
    tq = NA_ROWS * GRID_W
    assert tq == ROW_TILE and lc % tq == 0
    n_img_rows = (s - lc) // GRID_W
    nblk = n_img_rows // NA_ROWS
    off = 0 if with_ctx else lc // tq
    rows_out = s - off * tq
    first = 1 if with_ctx else 0

    def variant(bb, i):
        blk = i - first
        return (jnp.where(blk <= 0, 0, jnp.where(blk == nblk - 1, 2, 1)), 0, 0, 0)

    return pl.pallas_call(
        functools.partial(_na_kernel, lc=lc, n_img_rows=n_img_rows, ctx_first=with_ctx, use_max=use_max),
        out_shape=jax.ShapeDtypeStruct((b, rows_out, HW), BF16),
        grid=(b, rows_out // tq),
        in_specs=[pl.BlockSpec((1, tq, HW), lambda bb, i: (bb, i + off, 0)),
                  pl.BlockSpec((1, HW, s), lambda bb, i: (bb, 0, 0)),
                  pl.BlockSpec((1, s, HW), lambda bb, i: (bb, 0, 0)),
                  pl.BlockSpec((1, N_HEADS, tq, NA_KROWS * GRID_W), variant)],
        out_specs=pl.BlockSpec((1, tq, HW), lambda bb, i: (bb, i, 0)),
        compiler_params=_cparams(("arbitrary", "arbitrary")),
        name="nbr_attn_rowmax" if use_max else "nbr_attn",
    )(qn, knt, vn, bias)


def _na_bias(rpb, n_img_rows):
    a = jnp.arange(NA_ROWS)[:, None, None, None]
    cq = jnp.arange(GRID_W)[None, :, None, None]
    j = jnp.arange(NA_KROWS)[None, None, :, None]
    ck = jnp.arange(GRID_W)[None, None, None, :]
    cstart = jnp.clip(cq - WIN_COLS // 2, 0, GRID_W - WIN_COLS)
    colmask = (ck >= cstart) & (ck < cstart + WIN_COLS)
    dc = jnp.clip(ck - cq, -(WIN_COLS - 1), WIN_COLS - 1) + (WIN_COLS - 1)
    by_col = jnp.take(rpb.astype(F32), dc.reshape(-1), axis=2).reshape(N_HEADS, 2 * WIN_ROWS - 1, GRID_W, GRID_W)
    out = []
    for r0_minus_k, wstart in ((0, 0 * a), (WIN_ROWS // 2, a), (NA_KROWS - NA_ROWS, NA_KROWS - WIN_ROWS + 0 * a)):
        inwin = (j >= wstart) & (j < wstart + WIN_ROWS)
        dr = jnp.clip(j - r0_minus_k - a + (WIN_ROWS - 1), 0, 2 * WIN_ROWS - 2)
        vals = jnp.take(by_col, dr.reshape(-1), axis=1).reshape(N_HEADS, NA_ROWS, NA_KROWS, GRID_W, GRID_W)
        vals = jnp.transpose(vals, (0, 1, 3, 2, 4))
        vals = jnp.where((inwin & colmask)[None], vals * LOG2E, NEG)
        out.append(vals.reshape(N_HEADS, NA_ROWS * GRID_W, NA_KROWS * GRID_W))
    return jnp.stack(out)


def _s5_kernel(uf_ref, ub_ref, bmf_ref, cmf_ref, arf_ref, aif_ref, bmb_ref, cmb_ref, arb_ref, aib_ref,
               yf_ref, yb_ref, xf_scr, xb_scr, st_scr, io_scr):
    nb, tc, w = uf_ref.shape
    ns = arf_ref.shape[1]

    @pl.when(pl.program_id(0) == 0)
    def _():
        st_scr[...] = jnp.zeros_like(st_scr)

    nl = w // 128

    def drive(u_ref, bm_ref, x_scr):
        for b in range(nb):
            for c in range(nl):
                io_scr[c, pl.ds(b, tc, stride=nb), :] = u_ref[b, :, c * 128:(c + 1) * 128]
        u_tm = jnp.concatenate([io_scr[c] for c in range(nl)], axis=1)
        x_scr[...] = _dot(u_tm.astype(BF16), bm_ref[...])

    drive(uf_ref, bmf_ref, xf_scr)
    drive(ub_ref, bmb_ref, xb_scr)

    def advance(x_scr, a_re, a_im, s_re, s_im, tt):
        rows = pl.ds(pl.multiple_of(tt * nb, nb), nb)
        n_re = a_re * s_re - a_im * s_im + x_scr[rows, 0:ns]
        n_im = a_re * s_im + a_im * s_re + x_scr[rows, ns:2 * ns]
        x_scr[rows, 0:ns] = n_re
        x_scr[rows, ns:2 * ns] = n_im
        return n_re, n_im

    def step(t, carry):
        f_re, f_im, b_re, b_im = carry
        f_re, f_im = advance(xf_scr, arf_ref[...], aif_ref[...], f_re, f_im, t)
        b_re, b_im = advance(xb_scr, arb_ref[...], aib_ref[...], b_re, b_im, tc - 1 - t)
        return f_re, f_im, b_re, b_im

    init = (st_scr[0, :, 0:ns], st_scr[0, :, ns:2 * ns], st_scr[1, :, 0:ns], st_scr[1, :, ns:2 * ns])
    f_re, f_im, b_re, b_im = lax.fori_loop(0, tc, step, init, unroll=2)
    st_scr[0, :, 0:ns] = f_re
    st_scr[0, :, ns:2 * ns] = f_im
    st_scr[1, :, 0:ns] = b_re
    st_scr[1, :, ns:2 * ns] = b_im

    def readout(x_scr, cm_ref, y_ref):
        y_tm = _dot(x_scr[...].astype(BF16), cm_ref[...])
        for c in range(nl):
            io_scr[c] = y_tm[:, c * 128:(c + 1) * 128]
        for b in range(nb):
            for c in range(nl):
                y_ref[b, :, c * 128:(c + 1) * 128] = io_scr[c, pl.ds(b, tc, stride=nb), :]

    readout(xf_scr, cmf_ref, yf_ref)
    readout(xb_scr, cmb_ref, yb_ref)


def _s5_scan(u, fwd, bwd, lc):
    nb, s, w = u.shape
    tc = S5_CHUNK
    nc, ncc = s // tc, lc // tc
    ns = fwd[2].shape[1]
    chunk_f = lambda j: (0, j, 0)
    chunk_b = lambda j: (0, jnp.where(j < ncc, ncc - 1 - j, nc - 1 - (j - ncc)), 0)
    const = lambda j: (0, 0)
    pspecs = [pl.BlockSpec((w, 2 * ns), const), pl.BlockSpec((2 * ns, w), const),
              pl.BlockSpec((nb, ns), const), pl.BlockSpec((nb, ns), const)]
    out = jax.ShapeDtypeStruct((nb, s, w), F32)
    return pl.pallas_call(
        _s5_kernel,
        out_shape=(out, out),
        grid=(nc,),
        in_specs=[pl.BlockSpec((nb, tc, w), chunk_f), pl.BlockSpec((nb, tc, w), chunk_b)] + pspecs + pspecs,
        out_specs=(pl.BlockSpec((nb, tc, w), chunk_f), pl.BlockSpec((nb, tc, w), chunk_b)),
        scratch_shapes=[pltpu.VMEM((tc * nb, 2 * ns), F32), pltpu.VMEM((tc * nb, 2 * ns), F32),
                        pltpu.VMEM((2, nb, 2 * ns), F32), pltpu.VMEM((w // 128, tc * nb, 128), F32)],
        compiler_params=_cparams(("arbitrary",)),
        name="s5_scan",
    )(u, u, *fwd, *bwd)


def _s5_params(a_re, a_im, log_dt, b_re, b_im, c_re, c_im, nb):
    g, n, p = S5_GROUPS, S5_STATE, S5_GROUP_CH
    lr, li = a_re.astype(F32), a_im.astype(F32)
    dt = jnp.exp(log_dt.astype(F32))[..., None]
    mag = jnp.exp(lr * dt)
    ab_r, ab_i = mag * jnp.cos(li * dt), mag * jnp.sin(li * dt)
    den = lr * lr + li * li
    cf_r = ((ab_r - 1.0) * lr + ab_i * li) / den
    cf_i = (ab_i * lr - (ab_r - 1.0) * li) / den
    br, bi = b_re.astype(F32), b_im.astype(F32)
    bb_r = cf_r[..., None] * br - cf_i[..., None] * bi
    bb_i = cf_r[..., None] * bi + cf_i[..., None] * br
    eye = jnp.eye(g, dtype=F32)
    out = []
    for k in range(2):
        b_r = jnp.einsum('gnp,gh->gphn', bb_r[k], eye).reshape(g * p, g * n)
        b_i = jnp.einsum('gnp,gh->gphn', bb_i[k], eye).reshape(g * p, g * n)
        bmat = jnp.concatenate([b_r, b_i], axis=1).astype(BF16)
        ct = jnp.transpose(c_re[k].astype(F32), (0, 2, 1))
        ci = jnp.transpose(c_im[k].astype(F32), (0, 2, 1))
        c_r = jnp.einsum('gnp,gh->gnhp', ct, eye).reshape(g * n, g * p)
        c_i = jnp.einsum('gnp,gh->gnhp', ci, eye).reshape(g * n, g * p)
        cmat = jnp.concatenate([c_r, -c_i], axis=0).astype(BF16)
        ar = jnp.broadcast_to(ab_r[k].reshape(1, g * n), (nb, g * n))
        ai = jnp.broadcast_to(ab_i[k].reshape(1, g * n), (nb, g * n))
        out.append((bmat, cmat, ar, ai))
    return out


def _out_kernel(oa_ref, on_ref, yf_ref, yb_ref, u_ref, x_ref, mod_ref, gffn_ref, wo_ref, dskip_ref, wglu_ref, bglu_ref,
                wr_ref, xo_ref, h2_ref, lg_ref):
    mod = mod_ref[0, 0]
    nt = lambda a, bb: lax.dot_general(a, bb, (((1,), (1,)), ((), ())), preferred_element_type=F32)
    wh, wl = _split_bf16(wr_ref[...])
    half = x_ref.shape[1] // 2
    for r in (slice(0, half), slice(half, 2 * half)):
        g = _gelu_tanh(yf_ref[0, r, :] + yb_ref[0, r, :] + dskip_ref[...] * u_ref[0, r, :])
        ob = g * _sigmoid(_dot(g.astype(BF16), wglu_ref[...]) + bglu_ref[...])
        mix = (_dot(oa_ref[0, r, :], wo_ref[0:HW, :]) + _dot(ob.astype(BF16), wo_ref[HW:HW + B_WIDTH, :])
               + _dot(on_ref[0, r, :], wo_ref[HW + B_WIDTH:, :]))
        x = x_ref[0, r, :] + mod[2:3] * mix
        xo_ref[0, r, :] = x
        ms = jnp.mean(x * x, axis=-1, keepdims=True)
        h2 = x * lax.rsqrt(ms + EPS) * gffn_ref[...] * (1.0 + mod[4:5]) + mod[3:4]
        h2_ref[0, r, :] = h2.astype(BF16)
        hh, hl = _split_bf16(h2)
        lg_ref[0, :, r] = nt(wh, hh) + nt(wh, hl) + nt(wl, hh)


def _out_projection(oa, on, y_f, y_b, u, xs, modall, g_ffn, w_out_bf, d_skip, w_glu_bf, b_glu, w_router_pad, lc,
                    with_ctx):
    b, s, d = xs.shape
    tm = ROW_TILE
    off = 0 if with_ctx else lc // tm
    rows_out = s - off * tm
    full = lambda bb, i: (bb, i + off, 0)
    outr = lambda bb, i: (bb, i, 0)
    const = lambda bb, i: (0, 0)
    ne = w_router_pad.shape[0]
    return pl.pallas_call(
        _out_kernel,
        out_shape=(jax.ShapeDtypeStruct((b, rows_out, d), F32),
                   jax.ShapeDtypeStruct((b, rows_out, d), BF16),
                   jax.ShapeDtypeStruct((b, ne, rows_out), F32)),
        grid=(b, rows_out // tm),
        in_specs=[pl.BlockSpec((1, tm, HW), outr),
                  pl.BlockSpec((1, tm, HW), outr),
                  pl.BlockSpec((1, tm, B_WIDTH), full),
                  pl.BlockSpec((1, tm, B_WIDTH), full),
                  pl.BlockSpec((1, tm, B_WIDTH), full),
                  pl.BlockSpec((1, tm, d), full),
                  pl.BlockSpec((1, 1, 6, d), lambda bb, i: (bb, jnp.minimum(i + off, 1), 0, 0)),
                  pl.BlockSpec((1, d), const),
                  pl.BlockSpec((d, d), const),
                  pl.BlockSpec((1, B_WIDTH), const),
                  pl.BlockSpec((B_WIDTH, B_WIDTH), const),
                  pl.BlockSpec((1, B_WIDTH), const),
                  pl.BlockSpec((ne, d), const)],
        out_specs=(pl.BlockSpec((1, tm, d), outr), pl.BlockSpec((1, tm, d), outr),
                   pl.BlockSpec((1, ne, tm), lambda bb, i: (bb, 0, i))),
        compiler_params=_cparams(("arbitrary", "arbitrary")),
        name="out_proj",
    )(oa, on, y_f, y_b, u, xs, modall, g_ffn.reshape(1, d), w_out_bf, d_skip.reshape(1, -1), w_glu_bf,
      b_glu.reshape(1, -1), w_router_pad)


def _swiglu(h, wg, wu):
    a = _dot(h, wg)
    return a * _sigmoid(a) * _dot(h, wu)


def _moe_kernel(cnt_ref, sub_ref, held_ref, slots_ref, wg_ref, wu_ref, wd_ref, sg_ref, su_ref, sd_ref, y_ref,
                h_scr, g_scr, acc_ref, *, n_routed):
    i = pl.program_id(0)
    e = pl.program_id(1)
    tm = h_scr.shape[0]
    nsub = tm // ROW_TILE

    @pl.when(e == 0)
    def _():
        for j in range(SUBROWS):
            h_scr[:, j * 128:(j + 1) * 128] = slots_ref[pl.ds(j, tm, stride=SLOT_ROWS), :].astype(BF16)
        g_scr[...] = slots_ref[pl.ds(SUBROWS, tm, stride=SLOT_ROWS), :]
        acc_ref[...] = jnp.zeros_like(acc_ref)

    @pl.when(e < n_routed)
    def _():
        shift = lax.rem(128 - e * MOE_EB, 128)
        entry = i * n_routed + e
        count = cnt_ref[entry]
        for k in range(1, nsub + 1):
            @pl.when(count == k)
            def _():
                rows = [pl.ds(pl.multiple_of(sub_ref[entry * nsub + q] * ROW_TILE, ROW_TILE), ROW_TILE)
                        for q in range(k)]
                h = jnp.concatenate([h_scr[r, :] for r in rows], axis=0)
                gsel = pltpu.roll(jnp.concatenate([g_scr[r, :] for r in rows], axis=0), shift, 1)
                hid = [(_swiglu(h, wg_ref[j], wu_ref[j]) * gsel[:, j:j + 1]).astype(BF16) for j in range(MOE_EB)]
                out = _dot(jnp.concatenate(hid, axis=-1), wd_ref[...])
                for q, r in enumerate(rows):
                    acc_ref[r, :] += out[q * ROW_TILE:(q + 1) * ROW_TILE, :]

    @pl.when(e == n_routed)
    def _():
        for s in range(nsub):
            rows = slice(s * ROW_TILE, (s + 1) * ROW_TILE)
            hs = _swiglu(h_scr[rows, :], sg_ref[...], su_ref[...]).astype(BF16)
            y = acc_ref[rows, :] + _dot(hs, sd_ref[...])
            for j in range(SUBROWS):
                y_ref[pl.ds(s * ROW_TILE * SUBROWS + j, ROW_TILE, stride=SUBROWS), :] = y[:, j * 128:(j + 1) * 128]


def _moe_tile(t):
    return max(m for m in range(ROW_TILE, MOE_TILE + 1, ROW_TILE) if t % m == 0)


def _moe(slots, need, wg, wu, wd, sg, su, sd):
    t = slots.shape[0] // SLOT_ROWS
    d = wg.shape[1]
    tm = _moe_tile(t)
    nsub = tm // ROW_TILE
    n_routed = wg.shape[0] // MOE_EB
    flags = jnp.transpose(need.reshape(t // tm, nsub, n_routed), (0, 2, 1))
    count = jnp.sum(flags, axis=-1).astype(jnp.int32).reshape(-1)
    listed = jnp.argsort(jnp.logical_not(flags), axis=-1, stable=True).astype(jnp.int32).reshape(-1)
    busy = jnp.any(flags, axis=-1)
    steps = jnp.arange(n_routed, dtype=jnp.int32)[None, :]
    held = lax.cummax(jnp.where(busy, steps, -1), axis=1)
    held = jnp.where(held >= 0, held, jnp.argmax(busy, axis=1).astype(jnp.int32)[:, None]).reshape(-1)
    wblk = lambda i, e, c, s, w: w[i * n_routed + jnp.minimum(e, n_routed - 1)]
    const = lambda i, e, c, s, w: (0, 0)
    return pl.pallas_call(
        functools.partial(_moe_kernel, n_routed=n_routed),
        out_shape=jax.ShapeDtypeStruct((t * SUBROWS, 128), F32),
        grid_spec=pltpu.PrefetchScalarGridSpec(
            num_scalar_prefetch=3,
            grid=(t // tm, n_routed + 1),
            in_specs=[pl.BlockSpec((tm * SLOT_ROWS, 128), lambda i, e, c, s, w: (i, 0)),
                      pl.BlockSpec((MOE_EB, d, EXPERT_DIM), lambda i, e, c, s, w: (wblk(i, e, c, s, w), 0, 0)),
                      pl.BlockSpec((MOE_EB, d, EXPERT_DIM), lambda i, e, c, s, w: (wblk(i, e, c, s, w), 0, 0)),
                      pl.BlockSpec((MOE_EB * EXPERT_DIM, d), lambda i, e, c, s, w: (wblk(i, e, c, s, w), 0)),
                      pl.BlockSpec((d, EXPERT_DIM), const),
                      pl.BlockSpec((d, EXPERT_DIM), const),
                      pl.BlockSpec((EXPERT_DIM, d), const)],
            out_specs=pl.BlockSpec((tm * SUBROWS, 128), lambda i, e, c, s, w: (i, 0)),
            scratch_shapes=[pltpu.VMEM((tm, d), BF16), pltpu.VMEM((tm, 128), F32), pltpu.VMEM((tm, d), F32)]),
        compiler_params=_cparams(("arbitrary", "arbitrary")),
        name="moe_ffn",
    )(count, listed, held, slots, wg, wu, wd, sg, su, sd)


def _route_kernel(lg_ref, bias_ref, o_ref, cls_ref, cnt_ref):
    gsz = N_EXPERTS // N_GROUPS
    tn = lg_ref.shape[2]
    ninf = -jnp.inf
    jidx = lax.broadcasted_iota(jnp.int32, (gsz, tn), 0)
    scores, biased, gscore = [], [], []
    for g in range(N_GROUPS):
        rows = slice(g * gsz, (g + 1) * gsz)
        sc = _sigmoid(lg_ref[0, rows, :])
        bi = sc + bias_ref[rows, :]
        m1 = jnp.max(bi, axis=0, keepdims=True)
        first = jnp.min(jnp.where(bi == m1, jidx, gsz), axis=0, keepdims=True)
        m2 = jnp.max(jnp.where(jidx == first, ninf, bi), axis=0, keepdims=True)
        scores.append(sc)
        biased.append(bi)
        gscore.append(m1 + m2)
    masked = []
    cls = jnp.zeros((1, tn), F32)
    for g in range(N_GROUPS):
        rank = jnp.zeros((1, tn), F32)
        for g2 in range(N_GROUPS):
            if g2 != g:
                ahead = (gscore[g2] >= gscore[g]) if g2 < g else (gscore[g2] > gscore[g])
                rank = rank + jnp.where(ahead, 1.0, 0.0)
        cls = cls + jnp.where(rank < TOPK_GROUPS, float(1 << g), 0.0)
        keep = jnp.broadcast_to(rank, (gsz, tn)) < TOPK_GROUPS
        masked.append(jnp.where(keep, biased[g], ninf))
    cls_ref[...] = cls
    onehot = jnp.where(lax.broadcasted_iota(jnp.int32, (N_CLASSES, tn), 0).astype(F32)
                       == jnp.broadcast_to(cls, (N_CLASSES, tn)), 1.0, 0.0)

    @pl.when((pl.program_id(0) == 0) & (pl.program_id(1) == 0))
    def _():
        cnt_ref[...] = jnp.zeros_like(cnt_ref)

    cnt_ref[...] += jnp.broadcast_to(jnp.sum(onehot, axis=1, keepdims=True), cnt_ref.shape)
    ranks = [jnp.zeros((gsz, tn), F32) for _ in range(N_GROUPS)]
    for g2 in range(N_GROUPS):
        for j2 in range(gsz):
            other = jnp.broadcast_to(masked[g2][j2:j2 + 1, :], (gsz, tn))
            for g in range(N_GROUPS):
                ge = jnp.where(other >= masked[g], 1.0, 0.0)
                gt = jnp.where(other > masked[g], 1.0, 0.0)
                if g2 < g:
                    ahead = ge
                elif g2 > g:
                    ahead = gt
                else:
                    ahead = jnp.where(jidx > j2, ge, gt)
                ranks[g] = ranks[g] + ahead
    picked = [jnp.where(ranks[g] < TOP_K, scores[g], 0.0) for g in range(N_GROUPS)]
    den = sum(jnp.sum(p, axis=0, keepdims=True) for p in picked)
    scale = ROUTED_SCALE / den
    shared_row = jnp.where(lax.broadcasted_iota(jnp.int32, (128 - N_EXPERTS, tn), 0) == 0, 1.0, 0.0)
    gates_t = jnp.concatenate([p * scale for p in picked] + [shared_row], axis=0)
    o_ref[...] = gates_t.T


def _route(logits_t, e_bias):
    b, ne, rows = logits_t.shape
    tn = ROW_TILE
    nt = rows // tn
    bias = jnp.broadcast_to(e_bias.astype(F32)[:, None], (N_EXPERTS, tn))
    return pl.pallas_call(
        _route_kernel,
        out_shape=(jax.ShapeDtypeStruct((b * rows, ne), F32), jax.ShapeDtypeStruct((1, b * rows), F32),
                   jax.ShapeDtypeStruct((N_CLASSES, 128), F32)),
        grid=(b, nt),
        in_specs=[pl.BlockSpec((1, ne, tn), lambda bb, i: (bb, 0, i)),
                  pl.BlockSpec((N_EXPERTS, tn), lambda bb, i: (0, 0))],
        out_specs=(pl.BlockSpec((tn, ne), lambda bb, i: (bb * nt + i, 0)),
                   pl.BlockSpec((1, tn), lambda bb, i: (0, bb * nt + i)),
                   pl.BlockSpec((N_CLASSES, 128), lambda bb, i: (0, 0))),
        compiler_params=_cparams(("arbitrary", "arbitrary")),
        name="moe_route",
    )(logits_t, bias)


def _sort_pos_kernel(cls_ref, off_ref, pos_ref, base_scr):
    tn = cls_ref.shape[1]

    @pl.when(pl.program_id(0) == 0)
    def _():
        base_scr[...] = jnp.zeros_like(base_scr)

    onehot = jnp.where(lax.broadcasted_iota(jnp.int32, (N_CLASSES, tn), 0).astype(F32)
                       == jnp.broadcast_to(cls_ref[...], (N_CLASSES, tn)), 1.0, 0.0)
    upper = jnp.where(lax.broadcasted_iota(jnp.int32, (tn, tn), 0) <= lax.broadcasted_iota(jnp.int32, (tn, tn), 1),
                      1.0, 0.0).astype(BF16)
    incl = _dot(onehot.astype(BF16), upper)
    posf = off_ref[:, 0:1] + base_scr[:, 0:1] + incl - 1.0
    base_scr[...] += jnp.broadcast_to(incl[:, tn - 1:tn], base_scr.shape)
    pos_ref[0] = jnp.sum(posf * onehot, axis=0, keepdims=True).astype(jnp.int32)


def _sort_positions(cls, offsets):
    t = cls.shape[1]
    tn = ROW_TILE
    nt = t // tn
    off = jnp.broadcast_to(offsets.astype(F32)[:, None], (N_CLASSES, 128))
    pos = pl.pallas_call(
        _sort_pos_kernel,
        out_shape=jax.ShapeDtypeStruct((nt, 1, tn), jnp.int32),
        grid=(nt,),
        in_specs=[pl.BlockSpec((1, tn), lambda i: (0, i)), pl.BlockSpec((N_CLASSES, 128), lambda i: (0, 0))],
        out_specs=pl.BlockSpec((1, 1, tn), lambda i: (i, 0, 0)),
        scratch_shapes=[pltpu.VMEM((N_CLASSES, 128), F32)],
        compiler_params=_cparams(("arbitrary",)),
        name="moe_sort_positions",
    )(cls, off)
    return pos.reshape(nt, tn)


def _row_copies(n, make):
    def issue(jj, c):
        for p in range(2):
            make(jj * 2 + p).start(priority=p)
        return c

    lax.fori_loop(0, n // 2, issue, 0, unroll=4)

    def drain(j, c):
        make(0).wait()
        return c

    lax.fori_loop(0, n, drain, 0, unroll=8)


def _permute_kernel(pos_hbm, h_ref, g_ref, out_hbm, pos_smem, rows_scr, sem_idx, sem_rows):
    i = pl.program_id(0)
    tok = h_ref.shape[0]
    idx_copy = pltpu.make_async_copy(pos_hbm.at[i], pos_smem, sem_idx)
    idx_copy.start()

    @pl.when(i == 0)
    def _():
        rows_scr[...] = jnp.zeros_like(rows_scr)

    for j in range(SUBROWS):
        rows_scr[pl.ds(j, tok, stride=SLOT_ROWS), :] = h_ref[:, j * 128:(j + 1) * 128].astype(F32)
    rows_scr[pl.ds(SUBROWS, tok, stride=SLOT_ROWS), :] = g_ref[...]
    idx_copy.wait()

    def row_copy(j):
        src = pl.multiple_of(j * SLOT_ROWS, SLOT_ROWS)
        dst = pl.multiple_of(pos_smem[j] * SLOT_ROWS, SLOT_ROWS)
        return pltpu.make_async_copy(rows_scr.at[pl.ds(src, SLOT_ROWS), :], out_hbm.at[pl.ds(dst, SLOT_ROWS), :],
                                     sem_rows)

    _row_copies(tok, row_copy)


def _permute(h2, gates, pos):
    t, d = h2.shape
    tok = _moe_tile(t)
    pos = pos.reshape(t // tok, tok)
    assert d == SUBROWS * 128
    return pl.pallas_call(
        _permute_kernel,
        out_shape=jax.ShapeDtypeStruct((t * SLOT_ROWS, 128), F32),
        grid=(t // tok,),
        in_specs=[pl.BlockSpec(memory_space=pl.ANY),
                  pl.BlockSpec((tok, d), lambda i: (i, 0)),
                  pl.BlockSpec((tok, 128), lambda i: (i, 0))],
        out_specs=pl.BlockSpec(memory_space=pl.ANY),
        scratch_shapes=[pltpu.SMEM((tok,), jnp.int32), pltpu.VMEM((tok * SLOT_ROWS, 128), F32),
                        pltpu.SemaphoreType.DMA, pltpu.SemaphoreType.DMA],
        compiler_params=_cparams(("arbitrary",)),
        name="moe_permute",
    )(pos, h2, gates)


def _unpermute_kernel(pos_hbm, y_hbm, x_ref, gt_ref, o_ref, pos_smem, buf, sem_idx, sem_rows):
    i = pl.program_id(0)
    tok = x_ref.shape[0]
    idx_copy = pltpu.make_async_copy(pos_hbm.at[i], pos_smem, sem_idx)
    idx_copy.start()
    idx_copy.wait()

    def row_copy(j):
        src = pl.multiple_of(pos_smem[j] * SUBROWS, SUBROWS)
        dst = pl.multiple_of(j * SUBROWS, SUBROWS)
        return pltpu.make_async_copy(y_hbm.at[pl.ds(src, SUBROWS), :], buf.at[pl.ds(dst, SUBROWS), :], sem_rows)

    _row_copies(tok, row_copy)
    for s in range(tok // ROW_TILE):
        rows = slice(s * ROW_TILE, (s + 1) * ROW_TILE)
        y = jnp.concatenate([buf[pl.ds(s * ROW_TILE * SUBROWS + j, ROW_TILE, stride=SUBROWS), :]
                             for j in range(SUBROWS)], axis=1)
        o_ref[rows, :] = x_ref[rows, :] + gt_ref[s] * y


def _unpermute(pos, y_rows, x, gt_tiles):
    t, d = x.shape
    tok = _moe_tile(t)
    pos = pos.reshape(t // tok, tok)
    nsub = tok // ROW_TILE
    return pl.pallas_call(
        _unpermute_kernel,
        out_shape=jax.ShapeDtypeStruct((t, d), F32),
        grid=(t // tok,),
        in_specs=[pl.BlockSpec(memory_space=pl.ANY), pl.BlockSpec(memory_space=pl.ANY),
                  pl.BlockSpec((tok, d), lambda i: (i, 0)),
                  pl.BlockSpec((nsub, 1, d), lambda i: (i, 0, 0))],
        out_specs=pl.BlockSpec((tok, d), lambda i: (i, 0)),
        scratch_shapes=[pltpu.SMEM((tok,), jnp.int32), pltpu.VMEM((tok * SUBROWS, 128), F32),
                        pltpu.SemaphoreType.DMA, pltpu.SemaphoreType.DMA],
        compiler_params=_cparams(("arbitrary",)),
        name="moe_unpermute",
    )(pos, y_rows, x, gt_tiles)


def _class_order():
    def popcount(v):
        return bin(v).count("1")

    rest = [c for c in range(N_CLASSES) if popcount(c) == TOPK_GROUPS]
    path = [rest.pop(0)]
    while rest:
        nxt = min(rest, key=lambda c: (popcount(c ^ path[-1]), c))
        rest.remove(nxt)
        path.append(nxt)
    return path + [c for c in range(N_CLASSES) if popcount(c) != TOPK_GROUPS]


def _step_needed(class_counts, n_sub, n_steps):
    order = jnp.asarray(_class_order(), dtype=jnp.int32)
    ordered_ends = jnp.cumsum(class_counts[order])
    ends = jnp.zeros_like(class_counts).at[order].set(ordered_ends)
    starts = ends - class_counts
    lo = jnp.arange(n_sub, dtype=jnp.int32)[:, None] * ROW_TILE
    present = (class_counts[None, :] > 0) & (starts[None, :] < lo + ROW_TILE) & (ends[None, :] > lo)
    bits = ((jnp.arange(N_CLASSES)[:, None] >> jnp.arange(N_GROUPS)[None, :]) & 1).astype(bool)
    group_needed = jnp.any(present[:, :, None] & bits[None, :, :], axis=1)
    steps_per_group = n_steps // N_GROUPS
    return jnp.repeat(group_needed, steps_per_group, axis=1), starts


def kernel(x, c, ctx, c_ctx, w_mod, b_mod, g_mix, g_ffn, w_in, w_out, a_gq, a_gk, a_lambda, a_gsub, n_gq, n_gk, n_rpb, s5_a_re, s5_a_im, s5_log_dt, s5_b_re, s5_b_im, s5_c_re, s5_c_im, s5_d, s5_w_glu, s5_b_glu, w_router, e_bias, w_gate, w_up, w_down, ws_gate, ws_up, ws_down):
    b, l, d = x.shape
    lc = ctx.shape[1]
    s = lc + l
    depth = w_mod.shape[0]
    n_img_rows = l // GRID_W

    xs = jnp.concatenate([ctx, x], axis=1).astype(F32)
    cond_rows = jnp.zeros((16, d), F32).at[:b].set(c.astype(F32)).at[b].set(c_ctx.astype(F32))
    cos, sa, sb = _rope_tables(s, lc)
    tables = (cos, sa, sb, _block_ones(A_QK_DIM), _block_ones(N_HEAD_DIM))

    for layer in range(depth):
        last = layer == depth - 1
        with_ctx = not last
        lam_init = 0.8 - 0.6 * math.exp(-0.3 * layer)

        mod = _modulation(cond_rows, w_mod[layer].astype(F32), b_mod[layer].astype(F32))
        mod_lat = mod[:b].reshape(b, 1, 6, d)
        mod_ctx = jnp.broadcast_to(mod[b].reshape(1, 1, 6, d), (b, 1, 6, d))
        modall = jnp.concatenate([mod_ctx, mod_lat], axis=1)

        gains = ((jnp.tile(a_gq[layer].astype(F32), HW // A_QK_DIM) * (A_QK_DIM ** -0.5 * LOG2E)).reshape(1, HW),
                 jnp.tile(a_gk[layer].astype(F32), HW // A_QK_DIM).reshape(1, HW),
                 (jnp.tile(n_gq[layer].astype(F32), N_HEADS) * (N_HEAD_DIM ** -0.5 * LOG2E)).reshape(1, HW),
                 jnp.tile(n_gk[layer].astype(F32), N_HEADS).reshape(1, HW))
        qa, kat, va, qn, knt, vn, u = _in_projection(xs, modall, g_mix[layer].astype(F32), w_in[layer].astype(BF16),
                                                     tables, gains, lc)

        lv = a_lambda[layer].astype(F32)
        lam = (jnp.exp(jnp.sum(lv[0] * lv[1])) - jnp.exp(jnp.sum(lv[2] * lv[3])) + lam_init).reshape(1, 1)
        gsub_t = (jnp.tile(a_gsub[layer].astype(F32), A_HEADS) * (1.0 - lam_init)).reshape(1, HW)
        a_bound = BOUND_SLACK * A_QK_DIM * jnp.max(jnp.abs(gains[0])) * jnp.max(jnp.abs(gains[1]))
        oa = lax.cond(a_bound <= SCORE_BOUND_LIMIT,
                      lambda *a: _diff_attention(*a, lc, with_ctx, False),
                      lambda *a: _diff_attention(*a, lc, with_ctx, True), qa, kat, va, lam, gsub_t)

        n_bound = (BOUND_SLACK * N_HEAD_DIM * jnp.max(jnp.abs(gains[2])) * jnp.max(jnp.abs(gains[3]))
                   + LOG2E * jnp.max(jnp.abs(n_rpb[layer].astype(F32))))
        on = lax.cond(n_bound <= SCORE_BOUND_LIMIT,
                      lambda *a: _na_attention(*a, lc, with_ctx, False),
                      lambda *a: _na_attention(*a, lc, with_ctx, True),
                      qn, knt, vn, _na_bias(n_rpb[layer], n_img_rows))

        s5_fwd, s5_bwd = _s5_params(s5_a_re[layer], s5_a_im[layer], s5_log_dt[layer], s5_b_re[layer],
                                    s5_b_im[layer], s5_c_re[layer], s5_c_im[layer], b)
        y_f, y_b = _s5_scan(u, s5_fwd, s5_bwd, lc)

        w_router_pad = jnp.zeros((128, d), F32).at[:N_EXPERTS].set(w_router[layer].astype(F32).T)
        x_new, h2, logits_t = _out_projection(oa, on, y_f, y_b, u, xs, modall, g_ffn[layer].astype(F32),
                                              w_out[layer].astype(BF16), s5_d[layer].astype(F32),
                                              s5_w_glu[layer].astype(BF16), s5_b_glu[layer].astype(F32),
                                              w_router_pad, lc, with_ctx)
        rows = x_new.shape[1]
        t = b * rows
        gates, cls, class_cnt = _route(logits_t, e_bias[layer])
        need, class_start = _step_needed(class_cnt[:, 0].astype(jnp.int32), t // ROW_TILE, N_EXPERTS // MOE_EB)
        pos = _sort_positions(cls, class_start)
        slots = _permute(h2.reshape(t, d), gates, pos)
        y_rows = _moe(slots, need, w_gate[layer].astype(BF16), w_up[layer].astype(BF16),
                      w_down[layer].astype(BF16).reshape(-1, d),
                      ws_gate[layer].astype(BF16), ws_up[layer].astype(BF16), ws_down[layer].astype(BF16))

        gt2 = modall[:, :, 5, :]
        tiles_per_batch = rows // ROW_TILE
        if with_ctx:
            nctx = lc // ROW_TILE
            sel = (jnp.arange(tiles_per_batch) >= nctx).astype(jnp.int32)
        else:
            sel = jnp.ones((tiles_per_batch,), jnp.int32)
        gt_tiles = gt2[:, sel, :].reshape(t // ROW_TILE, 1, d)
        out = _unpermute(pos, y_rows, x_new.reshape(t, d), gt_tiles)
        xs = out.reshape(b, rows, d)

    return xs.astype(x.dtype)
```

```python
import functools
import math

import jax
import jax.numpy as jnp
from jax import lax
from jax.experimental import pallas as pl
from jax.experimental.pallas import tpu as pltpu

F32 = jnp.float32
BF16 = jnp.bfloat16

D_MODEL = 1024
GRID_W = 64
EPS = 1e-6
A_HEADS = 6
A_QK_DIM = 32
A_V_DIM = 64
ROPE_THETA = 10000.0
S5_GROUPS = 16
S5_GROUP_CH = 16
S5_STATE = 64
N_HEADS = 6
N_HEAD_DIM = 64
WIN_ROWS = 8
WIN_COLS = 16
HW = 384
B_WIDTH = 256
Q_COLS = 768
IN_COLS = 2560
N_EXPERTS = 64
TOP_K = 8
N_GROUPS = 8
TOPK_GROUPS = 4
EXPERT_DIM = 256
ROUTED_SCALE = 2.5

ROW_TILE = 256
NA_ROWS = 4
NA_KROWS = 12
S5_CHUNK = 128
MOE_TILE = 1024
MOE_EB = 4
N_CLASSES = 1 << N_GROUPS
SUBROWS = 8
SLOT_ROWS = 16
NEG = -1e30
LOG2E = math.log2(math.e)
SCORE_BOUND_LIMIT = 50.0
BOUND_SLACK = 1.05
VMEM_LIMIT = 56 * 1024 * 1024
CAST_BLOCK_ELEMS = 1 << 20


def _sigmoid(x):
    return 1.0 / (1.0 + jnp.exp(-x))


def _gelu_tanh(x):
    return 0.5 * x * (1.0 + jnp.tanh(math.sqrt(2.0 / math.pi) * (x + 0.044715 * (x * x * x))))


def _split_bf16(a):
    hi = a.astype(BF16)
    lo = (a - hi.astype(F32)).astype(BF16)
    return hi, lo


def _dot(a, b):
    return jnp.dot(a, b, preferred_element_type=F32)


def _dot3(a, b):
    ah, al = _split_bf16(a)
    bh, bl = _split_bf16(b)
    return _dot(ah, bh) + _dot(ah, bl) + _dot(al, bh)


def _cparams(sem):
    return pltpu.CompilerParams(dimension_semantics=sem, vmem_limit_bytes=VMEM_LIMIT)


def _cast_kernel(w_ref, o_ref):
    o_ref[...] = w_ref[...].astype(o_ref.dtype)


def _to_bf16(w_layers, layer):
    shape = w_layers.shape[1:]
    cols = shape[-1]
    rows = math.prod(shape) // cols
    tr = CAST_BLOCK_ELEMS // cols
    assert rows % tr == 0
    nblk = rows // tr
    out = pl.pallas_call(
        _cast_kernel,
        out_shape=jax.ShapeDtypeStruct((rows, cols), BF16),
        grid=(nblk,),
        in_specs=[pl.BlockSpec((tr, cols), lambda i: (layer * nblk + i, 0))],
        out_specs=pl.BlockSpec((tr, cols), lambda i: (i, 0)),
        compiler_params=_cparams(("arbitrary",)),
        name="cast_bf16",
    )(w_layers.reshape(-1, cols))
    return out.reshape(shape)


def _mod_kernel(c_ref, w_ref, b_ref, o_ref):
    c = c_ref[...]
    cond = c * _sigmoid(c)
    o_ref[...] = _dot3(cond, w_ref[...]) + b_ref[...]


def _modulation(cond_rows, w_mod, b_mod):
    r, d = cond_rows.shape
    n = w_mod.shape[1]
    tn = 1536
    return pl.pallas_call(
        _mod_kernel,
        out_shape=jax.ShapeDtypeStruct((r, n), F32),
        grid=(n // tn,),
        in_specs=[pl.BlockSpec((r, d), lambda j: (0, 0)),
                  pl.BlockSpec((d, tn), lambda j: (0, j)),
                  pl.BlockSpec((1, tn), lambda j: (0, j))],
        out_specs=pl.BlockSpec((r, tn), lambda j: (0, j)),
        compiler_params=_cparams(("arbitrary",)),
        name="adaln_mod",
    )(cond_rows, w_mod, b_mod.reshape(1, n))


def _group_rms(t, ones_ref, gain_ref, group):
    ms = _dot((t * t).astype(BF16), ones_ref[...]) * (1.0 / group)
    return t * lax.rsqrt(ms + EPS) * gain_ref[...]


def _rope(t, cos_ref, sa_ref, sb_ref):
    up = pltpu.roll(t, HW - 8, 1)
    dn = pltpu.roll(t, 8, 1)
    return t * cos_ref[...] + up * sa_ref[...] + dn * sb_ref[...]


def _proj_kernel(x_ref, mod_ref, g_ref, w_ref, cos_ref, sa_ref, sb_ref, ones32_ref, ones64_ref,
                 gqa_ref, gka_ref, gqn_ref, gkn_ref,
                 qa_ref, kat_ref, va_ref, qn_ref, knt_ref, vn_ref, u_ref):
    x = x_ref[0]
    mod = mod_ref[0, 0]
    ms = jnp.mean(x * x, axis=-1, keepdims=True)
    h = x * lax.rsqrt(ms + EPS) * g_ref[...] * (1.0 + mod[1:2]) + mod[0:1]
    hb = h.astype(BF16)

    def sec(a, b):
        return _dot(hb, w_ref[:, a:b])

    qa = _rope(_group_rms(sec(0, 384), ones32_ref, gqa_ref, A_QK_DIM), cos_ref, sa_ref, sb_ref)
    qa_ref[0] = qa.astype(BF16)
    qn_ref[0] = _group_rms(sec(384, 768), ones64_ref, gqn_ref, N_HEAD_DIM).astype(BF16)
    ka = _rope(_group_rms(sec(768, 1152), ones32_ref, gka_ref, A_QK_DIM), cos_ref, sa_ref, sb_ref)
    kat_ref[0] = ka.T.astype(BF16)
    va_ref[0] = sec(1152, 1536).astype(BF16)
    kn = _group_rms(sec(1536, 1920), ones64_ref, gkn_ref, N_HEAD_DIM)
    knt_ref[0] = kn.T.astype(BF16)
    vn_ref[0] = sec(1920, 2304).astype(BF16)
    u_ref[0] = sec(2304, 2560)


def _in_projection(xs, modall, g_mix, w_in_bf, tables, gains, lc):
    b, s, d = xs.shape
    tm = ROW_TILE
    cos, sa, sb, ones32, ones64 = tables
    row = lambda i, bb: (bb, i, 0)
    tab = lambda i, bb: (i, 0)
    const2 = lambda i, bb: (0, 0)
    act = lambda w, dt: jax.ShapeDtypeStruct((b, s, w), dt)
    act_t = jax.ShapeDtypeStruct((b, HW, s), BF16)
    return pl.pallas_call(
        _proj_kernel,
        out_shape=(act(HW, BF16), act_t, act(HW, BF16), act(HW, BF16), act_t, act(HW, BF16), act(B_WIDTH, F32)),
        grid=(s // tm, b),
        in_specs=[pl.BlockSpec((1, tm, d), row),
                  pl.BlockSpec((1, 1, 6, d), lambda i, bb: (bb, jnp.minimum(i, 1), 0, 0)),
                  pl.BlockSpec((1, d), const2),
                  pl.BlockSpec((d, IN_COLS), const2),
                  pl.BlockSpec((tm, HW), tab), pl.BlockSpec((tm, HW), tab), pl.BlockSpec((tm, HW), tab),
                  pl.BlockSpec((HW, HW), const2), pl.BlockSpec((HW, HW), const2),
                  pl.BlockSpec((1, HW), const2), pl.BlockSpec((1, HW), const2),
                  pl.BlockSpec((1, HW), const2), pl.BlockSpec((1, HW), const2)],
        out_specs=(pl.BlockSpec((1, tm, HW), row),
                   pl.BlockSpec((1, HW, tm), lambda i, bb: (bb, 0, i)),
                   pl.BlockSpec((1, tm, HW), row),
                   pl.BlockSpec((1, tm, HW), row),
                   pl.BlockSpec((1, HW, tm), lambda i, bb: (bb, 0, i)),
                   pl.BlockSpec((1, tm, HW), row),
                   pl.BlockSpec((1, tm, B_WIDTH), row)),
        compiler_params=_cparams(("arbitrary", "arbitrary")),
        name="in_proj",
    )(xs, modall, g_mix.reshape(1, d), w_in_bf, cos, sa, sb, ones32, ones64, *gains)


def _rope_tables(s, lc):
    p = jnp.arange(s)
    pos = jnp.maximum(p - lc, 0)
    rows = (pos // GRID_W).astype(F32)
    cols = (pos % GRID_W).astype(F32)
    lane = jnp.arange(HW)
    j32 = lane % A_QK_DIM
    half = j32 // 16
    i16 = j32 % 16
    nf = 8
    inv = ROPE_THETA ** (-(i16 % nf).astype(F32) / nf)
    coord = jnp.where(half[None, :] == 0, rows[:, None], cols[:, None])
    ang = coord * inv[None, :]
    is_lat = (p >= lc)[:, None]
    second = (i16 >= nf)[None, :]
    cos = jnp.where(is_lat, jnp.cos(ang), 1.0)
    sin = jnp.where(is_lat, jnp.sin(ang), 0.0)
    sa = jnp.where(second, 0.0, -sin)
    sb = jnp.where(second, sin, 0.0)
    return cos.astype(F32), sa.astype(F32), sb.astype(F32)


def _block_ones(group):
    g = jnp.arange(HW) // group
    return (g[:, None] == g[None, :]).astype(BF16)


def _diff_attend(q_all, kt_ref, v_ref, lam, gsub_ref, sk, use_max):
    outs = []
    for h in range(A_HEADS):
        v = v_ref[0, 0:sk, h * A_V_DIM:(h + 1) * A_V_DIM]
        parts = []
        for sub in range(2):
            hs = 2 * h + sub
            off = hs * A_QK_DIM
            sc = _dot(q_all[:, off:off + A_QK_DIM], kt_ref[0, off:off + A_QK_DIM, 0:sk])
            e = jnp.exp2(sc - jnp.max(sc, axis=-1, keepdims=True)) if use_max else jnp.exp2(sc)
            parts.append(_dot(e.astype(BF16), v) * (1.0 / jnp.sum(e, axis=-1, keepdims=True)))
        o = parts[0] - lam * parts[1]
        outs.append(o * lax.rsqrt(jnp.mean(o * o, axis=-1, keepdims=True) + EPS))
    return (jnp.concatenate(outs, axis=-1) * gsub_ref[...]).astype(BF16)


def _diff_attn_kernel(q_ref, kt_ref, v_ref, lam_ref, gsub_ref, o_ref, *, lc, ctx_first, use_max):
    lam = lam_ref[...]
    s = kt_ref.shape[2]
    if ctx_first:
        i = pl.program_id(1)

        @pl.when(i == 0)
        def _():
            o_ref[0] = _diff_attend(q_ref[0], kt_ref, v_ref, lam, gsub_ref, lc, use_max)

        @pl.when(i > 0)
        def _():
            o_ref[0] = _diff_attend(q_ref[0], kt_ref, v_ref, lam, gsub_ref, s, use_max)
    else:
        o_ref[0] = _diff_attend(q_ref[0], kt_ref, v_ref, lam, gsub_ref, s, use_max)


def _diff_attention(qa, kat, va, lam, gsub_t, lc, with_ctx, use_max):
    b, s, _ = qa.shape
    tq = ROW_TILE
    off = 0 if with_ctx else lc // tq
    rows_out = s - off * tq
    return pl.pallas_call(
        functools.partial(_diff_attn_kernel, lc=lc, ctx_first=with_ctx, use_max=use_max),
        out_shape=jax.ShapeDtypeStruct((b, rows_out, HW), BF16),
        grid=(b, rows_out // tq),
        in_specs=[pl.BlockSpec((1, tq, HW), lambda bb, i: (bb, i + off, 0)),
                  pl.BlockSpec((1, HW, s), lambda bb, i: (bb, 0, 0)),
                  pl.BlockSpec((1, s, HW), lambda bb, i: (bb, 0, 0)),
                  pl.BlockSpec((1, 1), lambda bb, i: (0, 0)),
                  pl.BlockSpec((1, HW), lambda bb, i: (0, 0))],
        out_specs=pl.BlockSpec((1, tq, HW), lambda bb, i: (bb, i, 0)),
        compiler_params=_cparams(("arbitrary", "arbitrary")),
        name="diff_attn_rowmax" if use_max else "diff_attn",
    )(qa, kat, va, lam, gsub_t)


def _na_ctx_attend(q_all, kt_ref, v_ref, lc, use_max):
    outs = []
    for h in range(N_HEADS):
        hs = slice(h * N_HEAD_DIM, (h + 1) * N_HEAD_DIM)
        sc = _dot(q_all[:, hs], kt_ref[0, hs, 0:lc])
        e = jnp.exp2(sc - jnp.max(sc, axis=-1, keepdims=True)) if use_max else jnp.exp2(sc)
        o = _dot(e.astype(BF16), v_ref[0, 0:lc, hs])
        outs.append(o * (1.0 / jnp.sum(e, axis=-1, keepdims=True)))
    return jnp.concatenate(outs, axis=-1).astype(BF16)


def _na_attend(q_all, kt_ref, v_ref, bias_ref, koff, lc, use_max):
    nk = NA_KROWS * GRID_W
    outs = []
    for h in range(N_HEADS):
        hs = slice(h * N_HEAD_DIM, (h + 1) * N_HEAD_DIM)
        q = q_all[:, hs]
        s_loc = _dot(q, kt_ref[0, hs, pl.ds(koff, nk)]) + bias_ref[0, h]
        s_ctx = _dot(q, kt_ref[0, hs, 0:lc])
        if use_max:
            m = jnp.maximum(jnp.max(s_loc, axis=-1, keepdims=True), jnp.max(s_ctx, axis=-1, keepdims=True))
            s_loc, s_ctx = s_loc - m, s_ctx - m
        e_loc = jnp.exp2(s_loc)
        e_ctx = jnp.exp2(s_ctx)
        den = jnp.sum(e_loc, axis=-1, keepdims=True) + jnp.sum(e_ctx, axis=-1, keepdims=True)
        o = _dot(e_loc.astype(BF16), v_ref[0, pl.ds(koff, nk), hs]) + _dot(e_ctx.astype(BF16), v_ref[0, 0:lc, hs])
        outs.append(o * (1.0 / den))
    return jnp.concatenate(outs, axis=-1).astype(BF16)


def _na_kernel(q_ref, kt_ref, v_ref, bias_ref, o_ref, *, lc, n_img_rows, ctx_first, use_max):
    i = pl.program_id(1)
    blk = i - 1 if ctx_first else i
    start_row = jnp.clip(NA_ROWS * blk - WIN_ROWS // 2, 0, n_img_rows - NA_KROWS)
    koff = pl.multiple_of(lc + start_row * GRID_W, 128)
    if ctx_first:
        @pl.when(i == 0)
        def _():
            o_ref[0] = _na_ctx_attend(q_ref[0], kt_ref, v_ref, lc, use_max)

        @pl.when(i > 0)
        def _():
            o_ref[0] = _na_attend(q_ref[0], kt_ref, v_ref, bias_ref, koff, lc, use_max)
    else:
        o_ref[0] = _na_attend(q_ref[0], kt_ref, v_ref, bias_ref, koff, lc, use_max)


def _na_attention(qn, knt, vn, bias, lc, with_ctx, use_max):
    b, s, _ = qn.shape
    tq = NA_ROWS * GRID_W
    assert tq == ROW_TILE and lc % tq == 0
    n_img_rows = (s - lc) // GRID_W
    nblk = n_img_rows // NA_ROWS
    off = 0 if with_ctx else lc // tq
    rows_out = s - off * tq
    first = 1 if with_ctx else 0

    def variant(bb, i):
        blk = i - first
        return (jnp.where(blk <= 0, 0, jnp.where(blk == nblk - 1, 2, 1)), 0, 0, 0)

    return pl.pallas_call(
        functools.partial(_na_kernel, lc=lc, n_img_rows=n_img_rows, ctx_first=with_ctx, use_max=use_max),
        out_shape=jax.ShapeDtypeStruct((b, rows_out, HW), BF16),
        grid=(b, rows_out // tq),
        in_specs=[pl.BlockSpec((1, tq, HW), lambda bb, i: (bb, i + off, 0)),
                  pl.BlockSpec((1, HW, s), lambda bb, i: (bb, 0, 0)),
                  pl.BlockSpec((1, s, HW), lambda bb, i: (bb, 0, 0)),
                  pl.BlockSpec((1, N_HEADS, tq, NA_KROWS * GRID_W), variant)],
        out_specs=pl.BlockSpec((1, tq, HW), lambda bb, i: (bb, i, 0)),
        compiler_params=_cparams(("arbitrary", "arbitrary")),
        name="nbr_attn_rowmax" if use_max else "nbr_attn",
    )(qn, knt, vn, bias)


def _na_bias(rpb, n_img_rows):
    a = jnp.arange(NA_ROWS)[:, None, None, None]
    cq = jnp.arange(GRID_W)[None, :, None, None]
    j = jnp.arange(NA_KROWS)[None, None, :, None]
    ck = jnp.arange(GRID_W)[None, None, None, :]
    cstart = jnp.clip(cq - WIN_COLS // 2, 0, GRID_W - WIN_COLS)
    colmask = (ck >= cstart) & (ck < cstart + WIN_COLS)
    dc = jnp.clip(ck - cq, -(WIN_COLS - 1), WIN_COLS - 1) + (WIN_COLS - 1)
    by_col = jnp.take(rpb.astype(F32), dc.reshape(-1), axis=2).reshape(N_HEADS, 2 * WIN_ROWS - 1, GRID_W, GRID_W)
    out = []
    for r0_minus_k, wstart in ((0, 0 * a), (WIN_ROWS // 2, a), (NA_KROWS - NA_ROWS, NA_KROWS - WIN_ROWS + 0 * a)):
        inwin = (j >= wstart) & (j < wstart + WIN_ROWS)
        dr = jnp.clip(j - r0_minus_k - a + (WIN_ROWS - 1), 0, 2 * WIN_ROWS - 2)
        vals = jnp.take(by_col, dr.reshape(-1), axis=1).reshape(N_HEADS, NA_ROWS, NA_KROWS, GRID_W, GRID_W)
        vals = jnp.transpose(vals, (0, 1, 3, 2, 4))
        vals = jnp.where((inwin & colmask)[None], vals * LOG2E, NEG)
        out.append(vals.reshape(N_HEADS, NA_ROWS * GRID_W, NA_KROWS * GRID_W))
    return jnp.stack(out)


def _s5_kernel(uf_ref, ub_ref, bmf_ref, cmf_ref, arf_ref, aif_ref, bmb_ref, cmb_ref, arb_ref, aib_ref,
               yf_ref, yb_ref, xf_scr, xb_scr, st_scr, io_scr):
    nb, tc, w = uf_ref.shape
    ns = arf_ref.shape[1]

    @pl.when(pl.program_id(0) == 0)
    def _():
        st_scr[...] = jnp.zeros_like(st_scr)

    nl = w // 128

    def drive(u_ref, bm_ref, x_scr):
        for b in range(nb):
            for c in range(nl):
                io_scr[c, pl.ds(b, tc, stride=nb), :] = u_ref[b, :, c * 128:(c + 1) * 128]
        u_tm = jnp.concatenate([io_scr[c] for c in range(nl)], axis=1)
        x_scr[...] = _dot(u_tm.astype(BF16), bm_ref[...])

    drive(uf_ref, bmf_ref, xf_scr)
    drive(ub_ref, bmb_ref, xb_scr)

    def advance(x_scr, a_re, a_im, s_re, s_im, tt):
        rows = pl.ds(pl.multiple_of(tt * nb, nb), nb)
        n_re = a_re * s_re - a_im * s_im + x_scr[rows, 0:ns]
        n_im = a_re * s_im + a_im * s_re + x_scr[rows, ns:2 * ns]
        x_scr[rows, 0:ns] = n_re
        x_scr[rows, ns:2 * ns] = n_im
        return n_re, n_im

    def step(t, carry):
        f_re, f_im, b_re, b_im = carry
        f_re, f_im = advance(xf_scr, arf_ref[...], aif_ref[...], f_re, f_im, t)
        b_re, b_im = advance(xb_scr, arb_ref[...], aib_ref[...], b_re, b_im, tc - 1 - t)
        return f_re, f_im, b_re, b_im

    init = (st_scr[0, :, 0:ns], st_scr[0, :, ns:2 * ns], st_scr[1, :, 0:ns], st_scr[1, :, ns:2 * ns])
    f_re, f_im, b_re, b_im = lax.fori_loop(0, tc, step, init, unroll=2)
    st_scr[0, :, 0:ns] = f_re
    st_scr[0, :, ns:2 * ns] = f_im
    st_scr[1, :, 0:ns] = b_re
    st_scr[1, :, ns:2 * ns] = b_im

    def readout(x_scr, cm_ref, y_ref):
        y_tm = _dot(x_scr[...].astype(BF16), cm_ref[...])
        for c in range(nl):
            io_scr[c] = y_tm[:, c * 128:(c + 1) * 128]
        for b in range(nb):
            for c in range(nl):
                y_ref[b, :, c * 128:(c + 1) * 128] = io_scr[c, pl.ds(b, tc, stride=nb), :]

    readout(xf_scr, cmf_ref, yf_ref)
    readout(xb_scr, cmb_ref, yb_ref)


def _s5_scan(u, fwd, bwd, lc):
    nb, s, w = u.shape
    tc = S5_CHUNK
    nc, ncc = s // tc, lc // tc
    ns = fwd[2].shape[1]
    chunk_f = lambda j: (0, j, 0)
    chunk_b = lambda j: (0, jnp.where(j < ncc, ncc - 1 - j, nc - 1 - (j - ncc)), 0)
    const = lambda j: (0, 0)
    pspecs = [pl.BlockSpec((w, 2 * ns), const), pl.BlockSpec((2 * ns, w), const),
              pl.BlockSpec((nb, ns), const), pl.BlockSpec((nb, ns), const)]
    out = jax.ShapeDtypeStruct((nb, s, w), F32)
    return pl.pallas_call(
        _s5_kernel,
        out_shape=(out, out),
        grid=(nc,),
        in_specs=[pl.BlockSpec((nb, tc, w), chunk_f), pl.BlockSpec((nb, tc, w), chunk_b)] + pspecs + pspecs,
        out_specs=(pl.BlockSpec((nb, tc, w), chunk_f), pl.BlockSpec((nb, tc, w), chunk_b)),
        scratch_shapes=[pltpu.VMEM((tc * nb, 2 * ns), F32), pltpu.VMEM((tc * nb, 2 * ns), F32),
                        pltpu.VMEM((2, nb, 2 * ns), F32), pltpu.VMEM((w // 128, tc * nb, 128), F32)],
        compiler_params=_cparams(("arbitrary",)),
        name="s5_scan",
    )(u, u, *fwd, *bwd)


def _s5_params(a_re, a_im, log_dt, b_re, b_im, c_re, c_im, nb):
    g, n, p = S5_GROUPS, S5_STATE, S5_GROUP_CH
    lr, li = a_re.astype(F32), a_im.astype(F32)
    dt = jnp.exp(log_dt.astype(F32))[..., None]
    mag = jnp.exp(lr * dt)
    ab_r, ab_i = mag * jnp.cos(li * dt), mag * jnp.sin(li * dt)
    den = lr * lr + li * li
    cf_r = ((ab_r - 1.0) * lr + ab_i * li) / den
    cf_i = (ab_i * lr - (ab_r - 1.0) * li) / den
    br, bi = b_re.astype(F32), b_im.astype(F32)
    bb_r = cf_r[..., None] * br - cf_i[..., None] * bi
    bb_i = cf_r[..., None] * bi + cf_i[..., None] * br
    eye = jnp.eye(g, dtype=F32)
    out = []
    for k in range(2):
        b_r = jnp.einsum('gnp,gh->gphn', bb_r[k], eye).reshape(g * p, g * n)
        b_i = jnp.einsum('gnp,gh->gphn', bb_i[k], eye).reshape(g * p, g * n)
        bmat = jnp.concatenate([b_r, b_i], axis=1).astype(BF16)
        ct = jnp.transpose(c_re[k].astype(F32), (0, 2, 1))
        ci = jnp.transpose(c_im[k].astype(F32), (0, 2, 1))
        c_r = jnp.einsum('gnp,gh->gnhp', ct, eye).reshape(g * n, g * p)
        c_i = jnp.einsum('gnp,gh->gnhp', ci, eye).reshape(g * n, g * p)
        cmat = jnp.concatenate([c_r, -c_i], axis=0).astype(BF16)
        ar = jnp.broadcast_to(ab_r[k].reshape(1, g * n), (nb, g * n))
        ai = jnp.broadcast_to(ab_i[k].reshape(1, g * n), (nb, g * n))
        out.append((bmat, cmat, ar, ai))
    return out


def _out_kernel(oa_ref, on_ref, yf_ref, yb_ref, u_ref, x_ref, mod_ref, gffn_ref, wo_ref, dskip_ref, wglu_ref, bglu_ref,
                wr_ref, xo_ref, h2_ref, lg_ref):
    mod = mod_ref[0, 0]
    nt = lambda a, bb: lax.dot_general(a, bb, (((1,), (1,)), ((), ())), preferred_element_type=F32)
    wh, wl = _split_bf16(wr_ref[...])
    half = x_ref.shape[1] // 2
    for r in (slice(0, half), slice(half, 2 * half)):
        g = _gelu_tanh(yf_ref[0, r, :] + yb_ref[0, r, :] + dskip_ref[...] * u_ref[0, r, :])
        ob = g * _sigmoid(_dot(g.astype(BF16), wglu_ref[...]) + bglu_ref[...])
        mix = (_dot(oa_ref[0, r, :], wo_ref[0:HW, :]) + _dot(ob.astype(BF16), wo_ref[HW:HW + B_WIDTH, :])
               + _dot(on_ref[0, r, :], wo_ref[HW + B_WIDTH:, :]))
        x = x_ref[0, r, :] + mod[2:3] * mix
        xo_ref[0, r, :] = x
        ms = jnp.mean(x * x, axis=-1, keepdims=True)
        h2 = x * lax.rsqrt(ms + EPS) * gffn_ref[...] * (1.0 + mod[4:5]) + mod[3:4]
        h2_ref[0, r, :] = h2.astype(BF16)
        hh, hl = _split_bf16(h2)
        lg_ref[0, :, r] = nt(wh, hh) + nt(wh, hl) + nt(wl, hh)


def _out_projection(oa, on, y_f, y_b, u, xs, modall, g_ffn, w_out_bf, d_skip, w_glu_bf, b_glu, w_router_pad, lc,
                    with_ctx):
    b, s, d = xs.shape
    tm = ROW_TILE
    off = 0 if with_ctx else lc // tm
    rows_out = s - off * tm
    full = lambda bb, i: (bb, i + off, 0)
    outr = lambda bb, i: (bb, i, 0)
    const = lambda bb, i: (0, 0)
    ne = w_router_pad.shape[0]
    return pl.pallas_call(
        _out_kernel,
        out_shape=(jax.ShapeDtypeStruct((b, rows_out, d), F32),
                   jax.ShapeDtypeStruct((b, rows_out, d), BF16),
                   jax.ShapeDtypeStruct((b, ne, rows_out), F32)),
        grid=(b, rows_out // tm),
        in_specs=[pl.BlockSpec((1, tm, HW), outr),
                  pl.BlockSpec((1, tm, HW), outr),
                  pl.BlockSpec((1, tm, B_WIDTH), full),
                  pl.BlockSpec((1, tm, B_WIDTH), full),
                  pl.BlockSpec((1, tm, B_WIDTH), full),
                  pl.BlockSpec((1, tm, d), full),
                  pl.BlockSpec((1, 1, 6, d), lambda bb, i: (bb, jnp.minimum(i + off, 1), 0, 0)),
                  pl.BlockSpec((1, d), const),
                  pl.BlockSpec((d, d), const),
                  pl.BlockSpec((1, B_WIDTH), const),
                  pl.BlockSpec((B_WIDTH, B_WIDTH), const),
                  pl.BlockSpec((1, B_WIDTH), const),
                  pl.BlockSpec((ne, d), const)],
        out_specs=(pl.BlockSpec((1, tm, d), outr), pl.BlockSpec((1, tm, d), outr),
                   pl.BlockSpec((1, ne, tm), lambda bb, i: (bb, 0, i))),
        compiler_params=_cparams(("arbitrary", "arbitrary")),
        name="out_proj",
    )(oa, on, y_f, y_b, u, xs, modall, g_ffn.reshape(1, d), w_out_bf, d_skip.reshape(1, -1), w_glu_bf,
      b_glu.reshape(1, -1), w_router_pad)


def _swiglu(h, wg, wu):
    a = _dot(h, wg)
    return a * _sigmoid(a) * _dot(h, wu)


def _moe_kernel(cnt_ref, sub_ref, held_ref, slots_ref, wg_ref, wu_ref, wd_ref, sg_ref, su_ref, sd_ref, y_ref,
                h_scr, g_scr, acc_ref, *, n_routed):
    i = pl.program_id(0)
    e = pl.program_id(1)
    tm = h_scr.shape[0]
    nsub = tm // ROW_TILE

    @pl.when(e == 0)
    def _():
        for j in range(SUBROWS):
            h_scr[:, j * 128:(j + 1) * 128] = slots_ref[pl.ds(j, tm, stride=SLOT_ROWS), :].astype(BF16)
        g_scr[...] = slots_ref[pl.ds(SUBROWS, tm, stride=SLOT_ROWS), :]
        acc_ref[...] = jnp.zeros_like(acc_ref)

    @pl.when(e < n_routed)
    def _():
        shift = lax.rem(128 - e * MOE_EB, 128)
        entry = i * n_routed + e
        count = cnt_ref[entry]
        for k in range(1, nsub + 1):
            @pl.when(count == k)
            def _():
                rows = [pl.ds(pl.multiple_of(sub_ref[entry * nsub + q] * ROW_TILE, ROW_TILE), ROW_TILE)
                        for q in range(k)]
                h = jnp.concatenate([h_scr[r, :] for r in rows], axis=0)
                gsel = pltpu.roll(jnp.concatenate([g_scr[r, :] for r in rows], axis=0), shift, 1)
                hid = [(_swiglu(h, wg_ref[j], wu_ref[j]) * gsel[:, j:j + 1]).astype(BF16) for j in range(MOE_EB)]
                out = _dot(jnp.concatenate(hid, axis=-1), wd_ref[...])
                for q, r in enumerate(rows):
                    acc_ref[r, :] += out[q * ROW_TILE:(q + 1) * ROW_TILE, :]

    @pl.when(e == n_routed)
    def _():
        for s in range(nsub):
            rows = slice(s * ROW_TILE, (s + 1) * ROW_TILE)
            hs = _swiglu(h_scr[rows, :], sg_ref[...], su_ref[...]).astype(BF16)
            y = acc_ref[rows, :] + _dot(hs, sd_ref[...])
            for j in range(SUBROWS):
                y_ref[pl.ds(s * ROW_TILE * SUBROWS + j, ROW_TILE, stride=SUBROWS), :] = y[:, j * 128:(j + 1) * 128]


def _moe_tile(t):
    return max(m for m in range(ROW_TILE, MOE_TILE + 1, ROW_TILE) if t % m == 0)


def _moe(slots, need, wg, wu, wd, sg, su, sd):
    t = slots.shape[0] // SLOT_ROWS
    d = wg.shape[1]
    tm = _moe_tile(t)
    nsub = tm // ROW_TILE
    n_routed = wg.shape[0] // MOE_EB
    flags = jnp.transpose(need.reshape(t // tm, nsub, n_routed), (0, 2, 1))
    count = jnp.sum(flags, axis=-1).astype(jnp.int32).reshape(-1)
    listed = jnp.argsort(jnp.logical_not(flags), axis=-1, stable=True).astype(jnp.int32).reshape(-1)
    busy = jnp.any(flags, axis=-1)
    steps = jnp.arange(n_routed, dtype=jnp.int32)[None, :]
    held = lax.cummax(jnp.where(busy, steps, -1), axis=1)
    held = jnp.where(held >= 0, held, jnp.argmax(busy, axis=1).astype(jnp.int32)[:, None]).reshape(-1)
    wblk = lambda i, e, c, s, w: w[i * n_routed + jnp.minimum(e, n_routed - 1)]
    const = lambda i, e, c, s, w: (0, 0)
    return pl.pallas_call(
        functools.partial(_moe_kernel, n_routed=n_routed),
        out_shape=jax.ShapeDtypeStruct((t * SUBROWS, 128), F32),
        grid_spec=pltpu.PrefetchScalarGridSpec(
            num_scalar_prefetch=3,
            grid=(t // tm, n_routed + 1),
            in_specs=[pl.BlockSpec((tm * SLOT_ROWS, 128), lambda i, e, c, s, w: (i, 0)),
                      pl.BlockSpec((MOE_EB, d, EXPERT_DIM), lambda i, e, c, s, w: (wblk(i, e, c, s, w), 0, 0)),
                      pl.BlockSpec((MOE_EB, d, EXPERT_DIM), lambda i, e, c, s, w: (wblk(i, e, c, s, w), 0, 0)),
                      pl.BlockSpec((MOE_EB * EXPERT_DIM, d), lambda i, e, c, s, w: (wblk(i, e, c, s, w), 0)),
                      pl.BlockSpec((d, EXPERT_DIM), const),
                      pl.BlockSpec((d, EXPERT_DIM), const),
                      pl.BlockSpec((EXPERT_DIM, d), const)],
            out_specs=pl.BlockSpec((tm * SUBROWS, 128), lambda i, e, c, s, w: (i, 0)),
            scratch_shapes=[pltpu.VMEM((tm, d), BF16), pltpu.VMEM((tm, 128), F32), pltpu.VMEM((tm, d), F32)]),
        compiler_params=_cparams(("arbitrary", "arbitrary")),
        name="moe_ffn",
    )(count, listed, held, slots, wg, wu, wd, sg, su, sd)


def _route_kernel(lg_ref, bias_ref, o_ref, cls_ref, cnt_ref):
    gsz = N_EXPERTS // N_GROUPS
    tn = lg_ref.shape[2]
    ninf = -jnp.inf
    jidx = lax.broadcasted_iota(jnp.int32, (gsz, tn), 0)
    scores, biased, gscore = [], [], []
    for g in range(N_GROUPS):
        rows = slice(g * gsz, (g + 1) * gsz)
        sc = _sigmoid(lg_ref[0, rows, :])
        bi = sc + bias_ref[rows, :]
        m1 = jnp.max(bi, axis=0, keepdims=True)
        first = jnp.min(jnp.where(bi == m1, jidx, gsz), axis=0, keepdims=True)
        m2 = jnp.max(jnp.where(jidx == first, ninf, bi), axis=0, keepdims=True)
        scores.append(sc)
        biased.append(bi)
        gscore.append(m1 + m2)
    masked = []
    cls = jnp.zeros((1, tn), F32)
    for g in range(N_GROUPS):
        rank = jnp.zeros((1, tn), F32)
        for g2 in range(N_GROUPS):
            if g2 != g:
                ahead = (gscore[g2] >= gscore[g]) if g2 < g else (gscore[g2] > gscore[g])
                rank = rank + jnp.where(ahead, 1.0, 0.0)
        cls = cls + jnp.where(rank < TOPK_GROUPS, float(1 << g), 0.0)
        keep = jnp.broadcast_to(rank, (gsz, tn)) < TOPK_GROUPS
        masked.append(jnp.where(keep, biased[g], ninf))
    cls_ref[...] = cls
    onehot = jnp.where(lax.broadcasted_iota(jnp.int32, (N_CLASSES, tn), 0).astype(F32)
                       == jnp.broadcast_to(cls, (N_CLASSES, tn)), 1.0, 0.0)

    @pl.when((pl.program_id(0) == 0) & (pl.program_id(1) == 0))
    def _():
        cnt_ref[...] = jnp.zeros_like(cnt_ref)

    cnt_ref[...] += jnp.broadcast_to(jnp.sum(onehot, axis=1, keepdims=True), cnt_ref.shape)
    ranks = [jnp.zeros((gsz, tn), F32) for _ in range(N_GROUPS)]
    for g2 in range(N_GROUPS):
        for j2 in range(gsz):
            other = jnp.broadcast_to(masked[g2][j2:j2 + 1, :], (gsz, tn))
            for g in range(N_GROUPS):
                ge = jnp.where(other >= masked[g], 1.0, 0.0)
                gt = jnp.where(other > masked[g], 1.0, 0.0)
                if g2 < g:
                    ahead = ge
                elif g2 > g:
                    ahead = gt
                else:
                    ahead = jnp.where(jidx > j2, ge, gt)
                ranks[g] = ranks[g] + ahead
    picked = [jnp.where(ranks[g] < TOP_K, scores[g], 0.0) for g in range(N_GROUPS)]
    den = sum(jnp.sum(p, axis=0, keepdims=True) for p in picked)
    scale = ROUTED_SCALE / den
    shared_row = jnp.where(lax.broadcasted_iota(jnp.int32, (128 - N_EXPERTS, tn), 0) == 0, 1.0, 0.0)
    gates_t = jnp.concatenate([p * scale for p in picked] + [shared_row], axis=0)
    o_ref[...] = gates_t.T


def _route(logits_t, e_bias):
    b, ne, rows = logits_t.shape
    tn = ROW_TILE
    nt = rows // tn
    bias = jnp.broadcast_to(e_bias.astype(F32)[:, None], (N_EXPERTS, tn))
    return pl.pallas_call(
        _route_kernel,
        out_shape=(jax.ShapeDtypeStruct((b * rows, ne), F32), jax.ShapeDtypeStruct((1, b * rows), F32),
                   jax.ShapeDtypeStruct((N_CLASSES, 128), F32)),
        grid=(b, nt),
        in_specs=[pl.BlockSpec((1, ne, tn), lambda bb, i: (bb, 0, i)),
                  pl.BlockSpec((N_EXPERTS, tn), lambda bb, i: (0, 0))],
        out_specs=(pl.BlockSpec((tn, ne), lambda bb, i: (bb * nt + i, 0)),
                   pl.BlockSpec((1, tn), lambda bb, i: (0, bb * nt + i)),
                   pl.BlockSpec((N_CLASSES, 128), lambda bb, i: (0, 0))),
        compiler_params=_cparams(("arbitrary", "arbitrary")),
        name="moe_route",
    )(logits_t, bias)


def _sort_pos_kernel(cls_ref, off_ref, pos_ref, base_scr):
    tn = cls_ref.shape[1]

    @pl.when(pl.program_id(0) == 0)
    def _():
        base_scr[...] = jnp.zeros_like(base_scr)

    onehot = jnp.where(lax.broadcasted_iota(jnp.int32, (N_CLASSES, tn), 0).astype(F32)
                       == jnp.broadcast_to(cls_ref[...], (N_CLASSES, tn)), 1.0, 0.0)
    upper = jnp.where(lax.broadcasted_iota(jnp.int32, (tn, tn), 0) <= lax.broadcasted_iota(jnp.int32, (tn, tn), 1),
                      1.0, 0.0).astype(BF16)
    incl = _dot(onehot.astype(BF16), upper)
    posf = off_ref[:, 0:1] + base_scr[:, 0:1] + incl - 1.0
    base_scr[...] += jnp.broadcast_to(incl[:, tn - 1:tn], base_scr.shape)
    pos_ref[0] = jnp.sum(posf * onehot, axis=0, keepdims=True).astype(jnp.int32)


def _sort_positions(cls, offsets):
    t = cls.shape[1]
    tn = ROW_TILE
    nt = t // tn
    off = jnp.broadcast_to(offsets.astype(F32)[:, None], (N_CLASSES, 128))
    pos = pl.pallas_call(
        _sort_pos_kernel,
        out_shape=jax.ShapeDtypeStruct((nt, 1, tn), jnp.int32),
        grid=(nt,),
        in_specs=[pl.BlockSpec((1, tn), lambda i: (0, i)), pl.BlockSpec((N_CLASSES, 128), lambda i: (0, 0))],
        out_specs=pl.BlockSpec((1, 1, tn), lambda i: (i, 0, 0)),
        scratch_shapes=[pltpu.VMEM((N_CLASSES, 128), F32)],
        compiler_params=_cparams(("arbitrary",)),
        name="moe_sort_positions",
    )(cls, off)
    return pos.reshape(nt, tn)


def _row_copies(n, make):
    def issue(jj, c):
        for p in range(2):
            make(jj * 2 + p).start(priority=p)
        return c

    lax.fori_loop(0, n // 2, issue, 0, unroll=4)

    def drain(j, c):
        make(0).wait()
        return c

    lax.fori_loop(0, n, drain, 0, unroll=8)


def _permute_kernel(pos_hbm, h_ref, g_ref, out_hbm, pos_smem, rows_scr, sem_idx, sem_rows):
    i = pl.program_id(0)
    tok = h_ref.shape[0]
    idx_copy = pltpu.make_async_copy(pos_hbm.at[i], pos_smem, sem_idx)
    idx_copy.start()

    @pl.when(i == 0)
    def _():
        rows_scr[...] = jnp.zeros_like(rows_scr)

    for j in range(SUBROWS):
        rows_scr[pl.ds(j, tok, stride=SLOT_ROWS), :] = h_ref[:, j * 128:(j + 1) * 128].astype(F32)
    rows_scr[pl.ds(SUBROWS, tok, stride=SLOT_ROWS), :] = g_ref[...]
    idx_copy.wait()

    def row_copy(j):
        src = pl.multiple_of(j * SLOT_ROWS, SLOT_ROWS)
        dst = pl.multiple_of(pos_smem[j] * SLOT_ROWS, SLOT_ROWS)
        return pltpu.make_async_copy(rows_scr.at[pl.ds(src, SLOT_ROWS), :], out_hbm.at[pl.ds(dst, SLOT_ROWS), :],
                                     sem_rows)

    _row_copies(tok, row_copy)


def _permute(h2, gates, pos):
    t, d = h2.shape
    tok = _moe_tile(t)
    pos = pos.reshape(t // tok, tok)
    assert d == SUBROWS * 128
    return pl.pallas_call(
        _permute_kernel,
        out_shape=jax.ShapeDtypeStruct((t * SLOT_ROWS, 128), F32),
        grid=(t // tok,),
        in_specs=[pl.BlockSpec(memory_space=pl.ANY),
                  pl.BlockSpec((tok, d), lambda i: (i, 0)),
                  pl.BlockSpec((tok, 128), lambda i: (i, 0))],
        out_specs=pl.BlockSpec(memory_space=pl.ANY),
        scratch_shapes=[pltpu.SMEM((tok,), jnp.int32), pltpu.VMEM((tok * SLOT_ROWS, 128), F32),
                        pltpu.SemaphoreType.DMA, pltpu.SemaphoreType.DMA],
        compiler_params=_cparams(("arbitrary",)),
        name="moe_permute",
    )(pos, h2, gates)


def _unpermute_kernel(pos_hbm, y_hbm, x_ref, gt_ref, o_ref, pos_smem, buf, sem_idx, sem_rows):
    i = pl.program_id(0)
    tok = x_ref.shape[0]
    idx_copy = pltpu.make_async_copy(pos_hbm.at[i], pos_smem, sem_idx)
    idx_copy.start()
    idx_copy.wait()

    def row_copy(j):
        src = pl.multiple_of(pos_smem[j] * SUBROWS, SUBROWS)
        dst = pl.multiple_of(j * SUBROWS, SUBROWS)
        return pltpu.make_async_copy(y_hbm.at[pl.ds(src, SUBROWS), :], buf.at[pl.ds(dst, SUBROWS), :], sem_rows)

    _row_copies(tok, row_copy)
    for s in range(tok // ROW_TILE):
        rows = slice(s * ROW_TILE, (s + 1) * ROW_TILE)
        y = jnp.concatenate([buf[pl.ds(s * ROW_TILE * SUBROWS + j, ROW_TILE, stride=SUBROWS), :]
                             for j in range(SUBROWS)], axis=1)
        o_ref[rows, :] = x_ref[rows, :] + gt_ref[s] * y


def _unpermute(pos, y_rows, x, gt_tiles):
    t, d = x.shape
    tok = _moe_tile(t)
    pos = pos.reshape(t // tok, tok)
    nsub = tok // ROW_TILE
    return pl.pallas_call(
        _unpermute_kernel,
        out_shape=jax.ShapeDtypeStruct((t, d), F32),
        grid=(t // tok,),
        in_specs=[pl.BlockSpec(memory_space=pl.ANY), pl.BlockSpec(memory_space=pl.ANY),
                  pl.BlockSpec((tok, d), lambda i: (i, 0)),
                  pl.BlockSpec((nsub, 1, d), lambda i: (i, 0, 0))],
        out_specs=pl.BlockSpec((tok, d), lambda i: (i, 0)),
        scratch_shapes=[pltpu.SMEM((tok,), jnp.int32), pltpu.VMEM((tok * SUBROWS, 128), F32),
                        pltpu.SemaphoreType.DMA, pltpu.SemaphoreType.DMA],
        compiler_params=_cparams(("arbitrary",)),
        name="moe_unpermute",
    )(pos, y_rows, x, gt_tiles)


def _class_order():
    def popcount(v):
        return bin(v).count("1")

    rest = [c for c in range(N_CLASSES) if popcount(c) == TOPK_GROUPS]
    path = [rest.pop(0)]
    while rest:
        nxt = min(rest, key=lambda c: (popcount(c ^ path[-1]), c))
        rest.remove(nxt)
        path.append(nxt)
    return path + [c for c in range(N_CLASSES) if popcount(c) != TOPK_GROUPS]


def _step_needed(class_counts, n_sub, n_steps):
    order = jnp.asarray(_class_order(), dtype=jnp.int32)
    ordered_ends = jnp.cumsum(class_counts[order])
    ends = jnp.zeros_like(class_counts).at[order].set(ordered_ends)
    starts = ends - class_counts
    lo = jnp.arange(n_sub, dtype=jnp.int32)[:, None] * ROW_TILE
    present = (class_counts[None, :] > 0) & (starts[None, :] < lo + ROW_TILE) & (ends[None, :] > lo)
    bits = ((jnp.arange(N_CLASSES)[:, None] >> jnp.arange(N_GROUPS)[None, :]) & 1).astype(bool)
    group_needed = jnp.any(present[:, :, None] & bits[None, :, :], axis=1)
    steps_per_group = n_steps // N_GROUPS
    return jnp.repeat(group_needed, steps_per_group, axis=1), starts


def kernel(x, c, ctx, c_ctx, w_mod, b_mod, g_mix, g_ffn, w_in, w_out, a_gq, a_gk, a_lambda, a_gsub, n_gq, n_gk, n_rpb, s5_a_re, s5_a_im, s5_log_dt, s5_b_re, s5_b_im, s5_c_re, s5_c_im, s5_d, s5_w_glu, s5_b_glu, w_router, e_bias, w_gate, w_up, w_down, ws_gate, ws_up, ws_down):
    b, l, d = x.shape
    lc = ctx.shape[1]
    s = lc + l
    depth = w_mod.shape[0]
    n_img_rows = l // GRID_W

    xs = jnp.concatenate([ctx, x], axis=1).astype(F32)
    cond_rows = jnp.zeros((16, d), F32).at[:b].set(c.astype(F32)).at[b].set(c_ctx.astype(F32))
    cos, sa, sb = _rope_tables(s, lc)
    tables = (cos, sa, sb, _block_ones(A_QK_DIM), _block_ones(N_HEAD_DIM))

    for layer in range(depth):
        last = layer == depth - 1
        with_ctx = not last
        lam_init = 0.8 - 0.6 * math.exp(-0.3 * layer)

        mod = _modulation(cond_rows, w_mod[layer].astype(F32), b_mod[layer].astype(F32))
        mod_lat = mod[:b].reshape(b, 1, 6, d)
        mod_ctx = jnp.broadcast_to(mod[b].reshape(1, 1, 6, d), (b, 1, 6, d))
        modall = jnp.concatenate([mod_ctx, mod_lat], axis=1)

        gains = ((jnp.tile(a_gq[layer].astype(F32), HW // A_QK_DIM) * (A_QK_DIM ** -0.5 * LOG2E)).reshape(1, HW),
                 jnp.tile(a_gk[layer].astype(F32), HW // A_QK_DIM).reshape(1, HW),
                 (jnp.tile(n_gq[layer].astype(F32), N_HEADS) * (N_HEAD_DIM ** -0.5 * LOG2E)).reshape(1, HW),
                 jnp.tile(n_gk[layer].astype(F32), N_HEADS).reshape(1, HW))
        qa, kat, va, qn, knt, vn, u = _in_projection(xs, modall, g_mix[layer].astype(F32), w_in[layer].astype(BF16),
                                                     tables, gains, lc)

        lv = a_lambda[layer].astype(F32)
        lam = (jnp.exp(jnp.sum(lv[0] * lv[1])) - jnp.exp(jnp.sum(lv[2] * lv[3])) + lam_init).reshape(1, 1)
        gsub_t = (jnp.tile(a_gsub[layer].astype(F32), A_HEADS) * (1.0 - lam_init)).reshape(1, HW)
        a_bound = BOUND_SLACK * A_QK_DIM * jnp.max(jnp.abs(gains[0])) * jnp.max(jnp.abs(gains[1]))
        oa = lax.cond(a_bound <= SCORE_BOUND_LIMIT,
                      lambda *a: _diff_attention(*a, lc, with_ctx, False),
                      lambda *a: _diff_attention(*a, lc, with_ctx, True), qa, kat, va, lam, gsub_t)

        n_bound = (BOUND_SLACK * N_HEAD_DIM * jnp.max(jnp.abs(gains[2])) * jnp.max(jnp.abs(gains[3]))
                   + LOG2E * jnp.max(jnp.abs(n_rpb[layer].astype(F32))))
        on = lax.cond(n_bound <= SCORE_BOUND_LIMIT,
                      lambda *a: _na_attention(*a, lc, with_ctx, False),
                      lambda *a: _na_attention(*a, lc, with_ctx, True),
                      qn, knt, vn, _na_bias(n_rpb[layer], n_img_rows))

        s5_fwd, s5_bwd = _s5_params(s5_a_re[layer], s5_a_im[layer], s5_log_dt[layer], s5_b_re[layer],
                                    s5_b_im[layer], s5_c_re[layer], s5_c_im[layer], b)
        y_f, y_b = _s5_scan(u, s5_fwd, s5_bwd, lc)

        w_router_pad = jnp.zeros((128, d), F32).at[:N_EXPERTS].set(w_router[layer].astype(F32).T)
        x_new, h2, logits_t = _out_projection(oa, on, y_f, y_b, u, xs, modall, g_ffn[layer].astype(F32),
                                              w_out[layer].astype(BF16), s5_d[layer].astype(F32),
                                              s5_w_glu[layer].astype(BF16), s5_b_glu[layer].astype(F32),
                                              w_router_pad, lc, with_ctx)
        rows = x_new.shape[1]
        t = b * rows
        gates, cls, class_cnt = _route(logits_t, e_bias[layer])
        need, class_start = _step_needed(class_cnt[:, 0].astype(jnp.int32), t // ROW_TILE, N_EXPERTS // MOE_EB)
        pos = _sort_positions(cls, class_start)
        slots = _permute(h2.reshape(t, d), gates, pos)
        y_rows = _moe(slots, need, _to_bf16(w_gate, layer), _to_bf16(w_up, layer),
                      _to_bf16(w_down, layer).reshape(-1, d),
                      ws_gate[layer].astype(BF16), ws_up[layer].astype(BF16), ws_down[layer].astype(BF16))

        gt2 = modall[:, :, 5, :]
        tiles_per_batch = rows // ROW_TILE
        if with_ctx:
            nctx = lc // ROW_TILE
            sel = (jnp.arange(tiles_per_batch) >= nctx).astype(jnp.int32)
        else:
            sel = jnp.ones((tiles_per_batch,), jnp.int32)
        gt_tiles = gt2[:, sel, :].reshape(t // ROW_TILE, 1, d)
        out = _unpermute(pos, y_rows, x_new.reshape(t, d), gt_tiles)
        xs = out.reshape(b, rows, d)

    return xs.astype(x.dtype)
```
